```python
import jax, jax.numpy as jnp
from jax import lax
import numpy as np

D_MODEL = 2048
BATCH = 1
SEQ = 8192
DEPTH = 1

D_MIX = D_MODEL
HEAD_DIM = 128
ATTN_WIDTH = D_MIX // 2
N_ATTN_HEADS = ATTN_WIDTH // HEAD_DIM
ROPE_DIM = HEAD_DIM // 4
ROPE_THETA = 500000.0
DILATED_PATTERNS = ((128, 1), (512, 4), (2048, 16))
ATTN_BLOCK = 128
RNN_WIDTH = D_MIX - ATTN_WIDTH
RNN_BLOCKS = 16
RNN_BLOCK_DIM = RNN_WIDTH // RNN_BLOCKS
CONV_WIDTH = 4
RGLRU_C = 8.0
PEER_HEADS = 8
PEER_NKEYS = 128
PEER_EXPERTS = PEER_NKEYS * PEER_NKEYS
PEER_TOPK = 16
PEER_KEY_DIM = 128
PEER_TOKEN_BLOCK = 128
PLE_DIM = 256
PROJ_IN = 3 * ATTN_WIDTH + 2 * RNN_WIDTH
EPS = 1e-6

kernel_name = "hybrid_dilated_attn_rglru_peer_block"


def rmsnorm(x, g):
    xf = x.astype(jnp.float32)
    y = xf * lax.rsqrt(jnp.mean(xf * xf, axis=-1, keepdims=True) + EPS)
    return (y * g.astype(jnp.float32)).astype(x.dtype)


def partial_rope(t, positions):
    half = ROPE_DIM // 2
    inv_freq = ROPE_THETA ** (-jnp.arange(half, dtype=jnp.float32) * 2.0 / ROPE_DIM)
    ang = positions.astype(jnp.float32)[..., None] * inv_freq
    cos = jnp.cos(ang)[:, :, None, :]
    sin = jnp.sin(ang)[:, :, None, :]
    tf = t.astype(jnp.float32)
    x1 = tf[..., :half]
    x2 = tf[..., half:ROPE_DIM]
    out = jnp.concatenate([x1 * cos - x2 * sin, x2 * cos + x1 * sin, tf[..., ROPE_DIM:]], axis=-1)
    return out.astype(t.dtype)


def dilated_window_attention(q, k, v, window, dilation):
    B, Sp, H, Dh = q.shape
    steps = window // dilation
    assert steps <= ATTN_BLOCK
    Q = Sp // dilation
    nb = Q // ATTN_BLOCK

    def to_blocks(t):
        t = t.reshape(B, Q, dilation, H, Dh).transpose(0, 3, 2, 1, 4)
        return t.reshape(B, H, dilation, nb, ATTN_BLOCK, Dh)

    def with_prev(t):
        prev = jnp.pad(t, ((0, 0), (0, 0), (0, 0), (1, 0), (0, 0), (0, 0)))[:, :, :, :-1]
        return jnp.concatenate([prev, t], axis=4)

    qb = to_blocks(q)
    kk = with_prev(to_blocks(k))
    vv = with_prev(to_blocks(v))
    s = jnp.einsum('bhrnqd,bhrnkd->bhrnqk', qb, kk,
                   preferred_element_type=jnp.float32) * (Dh ** -0.5)
    qi = jnp.arange(ATTN_BLOCK)[:, None]
    kj = jnp.arange(2 * ATTN_BLOCK)[None, :]
    dist = ATTN_BLOCK + qi - kj
    band = (dist >= 0) & (dist <= steps)
    has_prev = (jnp.arange(nb) > 0)[:, None, None] | (kj >= ATTN_BLOCK)[None]
    mask = band[None] & has_prev
    s = jnp.where(mask, s, -jnp.inf)
    m = jnp.max(s, axis=-1)
    e = jnp.exp(s - m[..., None])
    l = jnp.sum(e, axis=-1)
    o = jnp.einsum('bhrnqk,bhrnkd->bhrnqd', e, vv.astype(jnp.float32)) / l[..., None]
    o = o.reshape(B, H, dilation, Q, Dh).transpose(0, 3, 2, 1, 4).reshape(B, Sp, H, Dh)

    def stat_back(t):
        return t.reshape(B, H, dilation, Q).transpose(0, 3, 2, 1).reshape(B, Sp, H)

    return o, stat_back(m), stat_back(l)


def attention_mixer(q, k, v, positions):
    B, S, H, Dh = q.shape
    q = partial_rope(q, positions)
    k = partial_rope(k, positions)
    span = ATTN_BLOCK * int(np.lcm.reduce(np.array([d for _, d in DILATED_PATTERNS])))
    Sp = -(-S // span) * span
    pad = ((0, 0), (0, Sp - S), (0, 0), (0, 0))
    q, k, v = jnp.pad(q, pad), jnp.pad(k, pad), jnp.pad(v, pad)
    outs = [dilated_window_attention(q, k, v, w, d) for w, d in DILATED_PATTERNS]
    o_all = jnp.stack([o for o, _, _ in outs])
    m_all = jnp.stack([m for _, m, _ in outs])
    l_all = jnp.stack([l for _, _, l in outs])
    m_max = jnp.max(m_all, axis=0)
    wts = l_all * jnp.exp(m_all - m_max)
    o = jnp.sum(wts[..., None] * o_all, axis=0) / jnp.sum(wts, axis=0)[..., None]
    return o[:, :S].reshape(B, S, H * Dh)


def causal_depthwise_conv(x, w, b):
    S = x.shape[1]
    xp = jnp.pad(x, ((0, 0), (CONV_WIDTH - 1, 0), (0, 0)))
    y = b
    for tap in range(CONV_WIDTH):
        y = y + xp[:, tap:tap + S] * w[tap]
    return y


def rglru(x, w_a, b_a, w_x, b_x, lam):
    B, S, W = x.shape
    xb = x.reshape(B, S, RNN_BLOCKS, RNN_BLOCK_DIM)
    r = jax.nn.sigmoid(jnp.einsum('bsnd,nde->bsne', xb, w_a).reshape(B, S, W).astype(jnp.float32) + b_a)
    i = jax.nn.sigmoid(jnp.einsum('bsnd,nde->bsne', xb, w_x).reshape(B, S, W).astype(jnp.float32) + b_x)
    log_a = -RGLRU_C * r * jax.nn.softplus(-lam.astype(jnp.float32))
    a = jnp.exp(log_a)
    mult = jnp.sqrt(-jnp.expm1(2.0 * log_a))
    u = mult * i * x.astype(jnp.float32)

    def combine(c1, c2):
        a1, b1 = c1
        a2, b2 = c2
        return a1 * a2, a2 * b1 + b2

    _, h = lax.associative_scan(combine, (a, u), axis=1)
    return h.astype(x.dtype)


def peer_ffn(xn, w_q, subkeys, u, v):
    B, S, D = xn.shape
    T = B * S
    xt = xn.reshape(T // PEER_TOKEN_BLOCK, PEER_TOKEN_BLOCK, D)

    def block(xb):
        q = (xb @ w_q).reshape(PEER_TOKEN_BLOCK, PEER_HEADS, 2, PEER_KEY_DIM)
        s = jnp.einsum('thcd,hckd->thck', q, subkeys, preferred_element_type=jnp.float32)
        s_top, i_top = lax.top_k(s, PEER_TOPK)
        cand = (s_top[:, :, 0, :, None] + s_top[:, :, 1, None, :]).reshape(
            PEER_TOKEN_BLOCK, PEER_HEADS, PEER_TOPK * PEER_TOPK)
        best, pos = lax.top_k(cand, PEER_TOPK)
        i1 = jnp.take_along_axis(i_top[:, :, 0], pos // PEER_TOPK, axis=-1)
        i2 = jnp.take_along_axis(i_top[:, :, 1], pos % PEER_TOPK, axis=-1)
        expert = i1 * PEER_NKEYS + i2
        g = jax.nn.softmax(best, axis=-1)
        act = jax.nn.gelu(jnp.einsum('thkd,td->thk', u[expert], xb,
                                     preferred_element_type=jnp.float32))
        return jnp.einsum('thk,thkd->td', (g * act).astype(xb.dtype), v[expert])

    return lax.map(block, xt).reshape(B, S, D)


def setup_inputs(seed: int = 0) -> dict:
    key = jax.random.key(seed)
    ks = jax.random.split(key, 26)

    def nrm(k, shape, scale):
        return jax.random.normal(k, shape, jnp.float32) * scale

    def gain(k, shape):
        return 1.0 + 0.05 * jax.random.normal(k, shape, jnp.float32)

    a0 = jax.random.uniform(ks[11], (DEPTH, RNN_WIDTH), jnp.float32, 0.9, 0.999) ** (1.0 / RGLRU_C)
    return {
        "x": nrm(ks[0], (BATCH, SEQ, D_MODEL), 1.0),
        "p": nrm(ks[1], (DEPTH, BATCH, SEQ, PLE_DIM), 1.0),
        "positions": jnp.broadcast_to(jnp.arange(SEQ, dtype=jnp.int32), (BATCH, SEQ)),
        "mix_norm": gain(ks[2], (DEPTH, D_MODEL)),
        "w_in": nrm(ks[3], (DEPTH, D_MODEL, PROJ_IN), D_MODEL ** -0.5),
        "conv_w": nrm(ks[4], (DEPTH, CONV_WIDTH, RNN_WIDTH), CONV_WIDTH ** -0.5),
        "conv_b": nrm(ks[5], (DEPTH, RNN_WIDTH), 0.02),
        "rg_wa": nrm(ks[6], (DEPTH, RNN_BLOCKS, RNN_BLOCK_DIM, RNN_BLOCK_DIM), RNN_BLOCK_DIM ** -0.5),
        "rg_ba": nrm(ks[7], (DEPTH, RNN_WIDTH), 0.02),
        "rg_wx": nrm(ks[8], (DEPTH, RNN_BLOCKS, RNN_BLOCK_DIM, RNN_BLOCK_DIM), RNN_BLOCK_DIM ** -0.5),
        "rg_bx": nrm(ks[9], (DEPTH, RNN_WIDTH), 0.02),
        "rg_lambda": jnp.log(a0) - jnp.log1p(-a0),
        "attn_out_norm": gain(ks[12], (DEPTH, ATTN_WIDTH)),
        "rnn_out_norm": gain(ks[13], (DEPTH, RNN_WIDTH)),
        "w_out": nrm(ks[14], (DEPTH, D_MIX, D_MODEL), D_MIX ** -0.5),
        "ffn_norm": gain(ks[15], (DEPTH, D_MODEL)),
        "peer_wq": nrm(ks[16], (DEPTH, D_MODEL, PEER_HEADS * 2 * PEER_KEY_DIM), D_MODEL ** -0.5),
        "peer_subkeys": nrm(ks[17], (DEPTH, PEER_HEADS, 2, PEER_NKEYS, PEER_KEY_DIM), PEER_KEY_DIM ** -0.5),
        "peer_u": nrm(ks[18], (DEPTH, PEER_EXPERTS, D_MODEL), D_MODEL ** -0.5),
        "peer_v": nrm(ks[19], (DEPTH, PEER_EXPERTS, D_MODEL), 0.5),
        "ple_norm": gain(ks[20], (DEPTH, D_MODEL)),
        "ple_w_gate": nrm(ks[21], (DEPTH, D_MODEL, D_MODEL), D_MODEL ** -0.5),
        "ple_b_gate": nrm(ks[22], (DEPTH, D_MODEL), 0.02),
        "ple_proj": nrm(ks[23], (DEPTH, PLE_DIM, D_MODEL), PLE_DIM ** -0.5),
        "final_norm": gain(ks[24], (D_MODEL,)),
    }


def reference(x, p, positions, mix_norm, w_in, conv_w, conv_b, rg_wa, rg_ba, rg_wx, rg_bx,
              rg_lambda, attn_out_norm, rnn_out_norm, w_out, ffn_norm, peer_wq, peer_subkeys,
              peer_u, peer_v, ple_norm, ple_w_gate, ple_b_gate, ple_proj, final_norm):
    B, S, _ = x.shape
    A = ATTN_WIDTH
    h = x
    for i in range(DEPTH):
        xn = rmsnorm(h, mix_norm[i])
        proj = xn @ w_in[i]
        q = proj[..., 0:A].reshape(B, S, N_ATTN_HEADS, HEAD_DIM)
        k = proj[..., A:2 * A].reshape(B, S, N_ATTN_HEADS, HEAD_DIM)
        v = proj[..., 2 * A:3 * A].reshape(B, S, N_ATTN_HEADS, HEAD_DIM)
        xr = proj[..., 3 * A:3 * A + RNN_WIDTH]
        gate = proj[..., 3 * A + RNN_WIDTH:]
        attn = attention_mixer(q, k, v, positions).astype(h.dtype)
        xr = causal_depthwise_conv(xr, conv_w[i], conv_b[i])
        rnn = rglru(xr, rg_wa[i], rg_ba[i], rg_wx[i], rg_bx[i], rg_lambda[i]) * jax.nn.gelu(gate)
        mixed = jnp.concatenate([rmsnorm(attn, attn_out_norm[i]), rmsnorm(rnn, rnn_out_norm[i])], axis=-1)
        h = h + mixed @ w_out[i]
        hn = rmsnorm(h, ffn_norm[i])
        h = h + peer_ffn(hn, peer_wq[i], peer_subkeys[i], peer_u[i], peer_v[i])
        g = jax.nn.sigmoid(rmsnorm(h, ple_norm[i]) @ ple_w_gate[i] + ple_b_gate[i])
        h = h + g * (p[i] @ ple_proj[i])
    return rmsnorm(h, final_norm)
```

```python
import functools

import jax
import jax.numpy as jnp
from jax import lax
from jax.experimental import pallas as pl
from jax.experimental.pallas import tpu as pltpu

F32 = jnp.float32
BF16 = jnp.bfloat16

D_MODEL = 2048
SEQ = 8192
HEAD_DIM = 128
ATTN_WIDTH = 1024
N_HEADS = 8
ROPE_DIM = 32
ROPE_THETA = 500000.0
ATTN_BLOCK = 128
DILATIONS = (1, 4, 16)
SPAN = ATTN_BLOCK * 16
RNN_WIDTH = 1024
CONV_WIDTH = 4
RGLRU_C = 8.0
PEER_HEADS = 8
PEER_NKEYS = 128
PEER_EXPERTS = PEER_NKEYS * PEER_NKEYS
PEER_TOPK = 16
PLE_DIM = 256
EPS = 1e-6

VMEM_LIMIT = 56 * 1024 * 1024

IN_TM = 512
RNN_TM = 256
OUT_TM = 256
SEL_TB = 256
PEER_TM = 512
PEER_EC = 512
PLE_TM = 512


def _params(sem):
    return pltpu.CompilerParams(dimension_semantics=sem, vmem_limit_bytes=VMEM_LIMIT)


def _rms(x, g):
    return x * lax.rsqrt(jnp.mean(x * x, axis=-1, keepdims=True) + EPS) * g


def _resident(shape):
    nd = len(shape)
    return pl.BlockSpec(shape, lambda *_: (0,) * nd, pipeline_mode=pl.Buffered(1))


def _inproj_body(x_ref, pos_ref, invf_ref, g_ref, w_ref, q_ref, k_ref, v_ref, xr_ref, gate_ref):
    tm = x_ref.shape[0]
    xb = _rms(x_ref[...], g_ref[...]).astype(BF16)
    ang = pos_ref[...].astype(F32) * invf_ref[...]
    cosf = jnp.cos(ang)
    sinf = jnp.sin(ang)
    lane = lax.broadcasted_iota(jnp.int32, (tm, HEAD_DIM), 1)
    first_half = lane < ROPE_DIM // 2

    def rope(c):
        partner = jnp.where(first_half,
                            -pltpu.roll(c, HEAD_DIM - ROPE_DIM // 2, 1),
                            pltpu.roll(c, ROPE_DIM // 2, 1))
        return c * cosf + partner * sinf

    outs = (q_ref, k_ref, v_ref, xr_ref, gate_ref)
    for n, o_ref in enumerate(outs):
        y = jnp.dot(xb, w_ref[:, n * 1024:(n + 1) * 1024], preferred_element_type=F32)
        if n < 2:
            scale = HEAD_DIM ** -0.5 if n == 0 else 1.0
            for h in range(N_HEADS):
                sl = slice(h * HEAD_DIM, (h + 1) * HEAD_DIM)
                o_ref[:, sl] = rope(y[:, sl]) * scale
        else:
            o_ref[...] = y


def _inproj(x, pos, invf, g, w):
    S = x.shape[0]
    tm = IN_TM
    row = lambda n: pl.BlockSpec((tm, n), lambda i: (i, 0))
    out = jax.ShapeDtypeStruct((S, 1024), F32)
    return pl.pallas_call(
        _inproj_body,
        grid=(S // tm,),
        in_specs=[row(D_MODEL), row(1), _resident((1, HEAD_DIM)), _resident((1, D_MODEL)),
                  _resident(w.shape)],
        out_specs=[row(1024)] * 5,
        out_shape=[out] * 5,
        compiler_params=_params(("parallel",)),
    )(x, pos, invf, g, w)


def _attn_body(q_ref, k_ref, v_ref, o_ref, t4, q4, q16, k1, k4, k16, v1, v4, v16, op, ls):
    c = pl.program_id(1)
    kv = {1: (k1, v1), 4: (k4, v4), 16: (k16, v16)}

    @pl.when(c == 0)
    def _():
        for d in DILATIONS:
            for ref in kv[d]:
                ref[:, 0:ATTN_BLOCK, :] = jnp.zeros((d, ATTN_BLOCK, HEAD_DIM), BF16)

    @pl.when(c > 0)
    def _():
        for d in DILATIONS:
            rows = SPAN // d
            for ref in kv[d]:
                ref[:, 0:ATTN_BLOCK, :] = ref[:, rows:rows + ATTN_BLOCK, :]

    def deinterleave(x_ref, x1, x4, x16, off):
        if x1 is not None:
            x1[0, off:off + SPAN, :] = x_ref[...].astype(BF16)
        for r4 in range(4):
            t = x_ref[pl.ds(r4, SPAN // 4, stride=4), :]
            t4[r4] = t
            x4[r4, off:off + SPAN // 4, :] = t.astype(BF16)
        for r4 in range(4):
            for r2 in range(4):
                t = t4[r4, pl.ds(r2, SPAN // 16, stride=4), :]
                x16[4 * r2 + r4, off:off + SPAN // 16, :] = t.astype(BF16)

    deinterleave(q_ref, None, q4, q16, 0)
    deinterleave(k_ref, k1, k4, k16, ATTN_BLOCK)
    deinterleave(v_ref, v1, v4, v16, ATTN_BLOCK)

    qi = lax.broadcasted_iota(jnp.int32, (ATTN_BLOCK, 2 * ATTN_BLOCK), 0)
    kj = lax.broadcasted_iota(jnp.int32, (ATTN_BLOCK, 2 * ATTN_BLOCK), 1)
    dist = ATTN_BLOCK + qi - kj
    band = (dist >= 0) & (dist <= ATTN_BLOCK)

    def tile(qt, kk, vv, has_prev):
        s = lax.dot_general(qt, kk, (((1,), (1,)), ((), ())), preferred_element_type=F32)
        first_key = jnp.where(has_prev, 0, ATTN_BLOCK)
        mask = band & (kj >= first_key)
        s = jnp.where(mask, s, -jnp.inf)
        m = jnp.max(s, axis=-1, keepdims=True)
        e = jnp.exp(s - m)
        l = jnp.sum(e, axis=-1, keepdims=True)
        o = jnp.dot(e.astype(BF16), vv, preferred_element_type=F32)
        o = o / l
        lse = jnp.broadcast_to(m + jnp.log(l), (ATTN_BLOCK, HEAD_DIM))
        return o, lse

    prev_span = c > 0

    def run_pattern(p_idx, d, q_tile):
        k_s, v_s = kv[d]
        nblk = SPAN // d // ATTN_BLOCK

        def body(idx, carry):
            r = idx // nblk
            n = idx % nblk
            row0 = pl.multiple_of(n * ATTN_BLOCK, ATTN_BLOCK)
            kk = k_s[r, pl.ds(row0, 2 * ATTN_BLOCK), :]
            vv = v_s[r, pl.ds(row0, 2 * ATTN_BLOCK), :]
            o, lse = tile(q_tile(r, row0), kk, vv, (n > 0) | prev_span)
            start = n * (ATTN_BLOCK * d) + r
            if d == 1:
                dst = pl.ds(pl.multiple_of(start, ATTN_BLOCK), ATTN_BLOCK)
            else:
                dst = pl.ds(start, ATTN_BLOCK, stride=d)
            op[p_idx, dst, :] = o
            ls[p_idx, dst, :] = lse
            return carry

        lax.fori_loop(0, d * nblk, body, 0)

    run_pattern(0, 1, lambda r, row0: q_ref[pl.ds(row0, ATTN_BLOCK), :].astype(BF16))
    run_pattern(1, 4, lambda r, row0: q4[r, pl.ds(row0, ATTN_BLOCK), :])
    run_pattern(2, 16, lambda r, row0: q16[r, pl.ds(row0, ATTN_BLOCK), :])

    l0, l1, l2 = ls[0], ls[1], ls[2]
    top = jnp.maximum(jnp.maximum(l0, l1), l2)
    w0 = jnp.exp(l0 - top)
    w1 = jnp.exp(l1 - top)
    w2 = jnp.exp(l2 - top)
    o_ref[...] = (w0 * op[0] + w1 * op[1] + w2 * op[2]) / (w0 + w1 + w2)


def _attention(q, k, v):
    S = q.shape[0]
    spec = pl.BlockSpec((SPAN, HEAD_DIM), lambda h, c: (c, h))
    slab = lambda d, off, dt: pltpu.VMEM((d, off + SPAN // d, HEAD_DIM), dt)
    scratch = [
        slab(4, 0, F32), slab(4, 0, BF16), slab(16, 0, BF16),
        slab(1, ATTN_BLOCK, BF16), slab(4, ATTN_BLOCK, BF16), slab(16, ATTN_BLOCK, BF16),
        slab(1, ATTN_BLOCK, BF16), slab(4, ATTN_BLOCK, BF16), slab(16, ATTN_BLOCK, BF16),
        pltpu.VMEM((3, SPAN, HEAD_DIM), F32), pltpu.VMEM((3, SPAN, HEAD_DIM), F32),
    ]
    return pl.pallas_call(
        _attn_body,
        grid=(N_HEADS, S // SPAN),
        in_specs=[spec] * 3,
        out_specs=spec,
        out_shape=jax.ShapeDtypeStruct((S, ATTN_WIDTH), F32),
        scratch_shapes=scratch,
        compiler_params=_params(("arbitrary", "arbitrary")),
    )(q, k, v)


def _rnn_body(xr_ref, gate_ref, cw_ref, cb_ref, wa_ref, ba_ref, wx_ref, bx_ref, lam_ref, g_ref,
              o_ref, xext, hcar, a_s, u_s):
    tm = xr_ref.shape[0]
    i = pl.program_id(0)

    @pl.when(i == 0)
    def _():
        xext[0:8, :] = jnp.zeros((8, RNN_WIDTH), F32)
        hcar[...] = jnp.zeros((8, RNN_WIDTH), F32)

    x = xr_ref[...]
    xext[8:8 + tm, :] = x
    xc = cb_ref[...]
    for tap in range(CONV_WIDTH):
        sh = CONV_WIDTH - 1 - tap
        xc = xc + xext[8 - sh:8 - sh + tm, :] * cw_ref[tap:tap + 1, :]
    xext[0:8, :] = x[tm - 8:tm, :]

    xcb = xc.astype(BF16)
    for j in range(RNN_WIDTH // 128):
        sl = slice(j * 128, (j + 1) * 128)
        blk = xcb[:, sl]
        r = jax.nn.sigmoid(jnp.dot(blk, wa_ref[j], preferred_element_type=F32) + ba_ref[:, sl])
        ig = jax.nn.sigmoid(jnp.dot(blk, wx_ref[j], preferred_element_type=F32) + bx_ref[:, sl])
        log_a = -RGLRU_C * r * jax.nn.softplus(-lam_ref[:, sl])
        a = jnp.exp(log_a)
        a_s[:, sl] = a
        u_s[:, sl] = jnp.sqrt(-jnp.tanh(log_a) * (a * a + 1.0)) * ig * xc[:, sl]

    row = lax.broadcasted_iota(jnp.int32, (8, RNN_WIDTH), 0)

    def scan(t, h):
        base = pl.multiple_of(t * 8, 8)
        A = a_s[pl.ds(base, 8), :]
        B = u_s[pl.ds(base, 8), :]
        for sft in (1, 2, 4):
            valid = row >= sft
            A_sh = pltpu.roll(A, sft, 0)
            B_sh = pltpu.roll(B, sft, 0)
            B = jnp.where(valid, A * B_sh + B, B)
            A = jnp.where(valid, A * A_sh, A)
        hs = A * h + B
        u_s[pl.ds(base, 8), :] = hs
        return jnp.broadcast_to(hs[7:8, :], (8, RNN_WIDTH))

    hcar[...] = lax.fori_loop(0, tm // 8, scan, hcar[...])

    rnn = u_s[...] * jax.nn.gelu(gate_ref[...])
    o_ref[...] = _rms(rnn, g_ref[...]).astype(BF16)


def _rnn(xr, gate, cw, cb, wa, ba, wx, bx, lam, g):
    S = xr.shape[0]
    tm = RNN_TM
    row = pl.BlockSpec((tm, RNN_WIDTH), lambda i: (i, 0))
    vec = _resident((1, RNN_WIDTH))
    return pl.pallas_call(
        _rnn_body,
        grid=(S // tm,),
        in_specs=[row, row, _resident(cw.shape), vec, _resident(wa.shape), vec,
                  _resident(wx.shape), vec, vec, vec],
        out_specs=row,
        out_shape=jax.ShapeDtypeStruct((S, RNN_WIDTH), BF16),
        scratch_shapes=[pltpu.VMEM((tm + 8, RNN_WIDTH), F32), pltpu.VMEM((8, RNN_WIDTH), F32),
                        pltpu.VMEM((tm, RNN_WIDTH), F32), pltpu.VMEM((tm, RNN_WIDTH), F32)],
        compiler_params=_params(("arbitrary",)),
    )(xr, gate, cw, cb, wa, ba, wx, bx, lam, g)


def _outproj_body(attn_ref, rnn_ref, x_ref, ga_ref, wo_ref, gf_ref, wq_ref, sk_ref,
                  h1_ref, hn_ref, st_ref):
    an = _rms(attn_ref[...], ga_ref[...]).astype(BF16)
    mix = jnp.dot(an, wo_ref[0:ATTN_WIDTH, :], preferred_element_type=F32)
    mix = mix + jnp.dot(rnn_ref[...], wo_ref[ATTN_WIDTH:, :], preferred_element_type=F32)
    h1 = x_ref[...] + mix
    h1_ref[...] = h1
    hn = _rms(h1, gf_ref[...]).astype(BF16)
    hn_ref[...] = hn
    qp = jnp.dot(hn, wq_ref[...], preferred_element_type=F32).astype(BF16)
    for hc in range(2 * PEER_HEADS):
        st_ref[hc] = lax.dot_general(sk_ref[hc], qp[:, hc * 128:(hc + 1) * 128],
                                     (((1,), (1,)), ((), ())), preferred_element_type=F32)


def _outproj(attn, rnn, x, ga, wo, gf, wq, sk):
    S = x.shape[0]
    tm = OUT_TM
    row = lambda n: pl.BlockSpec((tm, n), lambda i: (i, 0))
    return pl.pallas_call(
        _outproj_body,
        grid=(S // tm,),
        in_specs=[row(ATTN_WIDTH), row(RNN_WIDTH), row(D_MODEL), _resident((1, ATTN_WIDTH)),
                  _resident(wo.shape), _resident((1, D_MODEL)), _resident(wq.shape),
                  _resident(sk.shape)],
        out_specs=[row(D_MODEL), row(D_MODEL),
                   pl.BlockSpec((2 * PEER_HEADS, PEER_NKEYS, tm), lambda i: (0, 0, i))],
        out_shape=[jax.ShapeDtypeStruct((S, D_MODEL), F32), jax.ShapeDtypeStruct((S, D_MODEL), BF16),
                   jax.ShapeDtypeStruct((2 * PEER_HEADS, PEER_NKEYS, S), F32)],
        compiler_params=_params(("parallel",)),
    )(attn, rnn, x, ga, wo, gf, wq, sk)


def _select_body(st_ref, thr_ref, e1n_ref, e2_ref, a_s, b_s):
    tb = st_ref.shape[-1]
    K = PEER_TOPK
    ninf = -jnp.inf
    row8 = lax.broadcasted_iota(jnp.int32, (8, tb), 0)

    def top_values(s, out_s):
        def rnd(k, s):
            mx = jnp.max(s, axis=0, keepdims=True)
            out_s[pl.ds(k, 1), :] = mx
            return jnp.where(s == mx, ninf, s)
        lax.fori_loop(0, K, rnd, s)

    def head(h, carry):
        s1 = st_ref[2 * h]
        s2 = st_ref[2 * h + 1]
        top_values(s1, a_s)
        top_values(s2, b_s)
        A = a_s[...]
        B = b_s[...]
        B8 = B[0:8, :]
        limits = (16, 8, 5, 4, 3, 2, 2, 2)
        cands = [A[0:1, :] + B, A[1:2, :] + B8]
        for i in range(2, 8):
            cands.append(jnp.where(row8 < limits[i], A[i:i + 1, :] + B8, ninf))
        cands.append(A[8:16, :] + B[0:1, :])
        cand = jnp.concatenate(cands, axis=0)

        def rnd(k, st):
            cnd, _ = st
            mx = jnp.max(cnd, axis=0, keepdims=True)
            return jnp.where(cnd == mx, ninf, cnd), mx
        _, tau = lax.fori_loop(0, K, rnd, (cand, jnp.zeros((1, tb), F32)))

        cmax = A[0:1, :] + B[0:1, :]
        z = jnp.sum(jnp.where(cand >= tau, jnp.exp(cand - cmax), 0.0), axis=0, keepdims=True)

        inf = jnp.inf
        tb_rows = [jnp.min(jnp.where(cands[0] >= tau, B, inf), axis=0, keepdims=True)]
        for i in range(1, 8):
            tb_rows.append(jnp.min(jnp.where(cands[i] >= tau, B8, inf), axis=0, keepdims=True))
        tail = jnp.where(cands[8] >= tau, B[0:1, :], inf)
        for i in range(8):
            tb_rows.append(tail[i:i + 1, :])

        thr = jnp.full((PEER_NKEYS, tb), inf, F32)
        for i in range(K):
            thr = jnp.where(s1 == A[i:i + 1, :], tb_rows[i], thr)
        thr_ref[h] = thr
        e1n_ref[h] = jnp.exp(s1 - A[0:1, :]) / z
        e2_ref[h] = jnp.exp(s2 - B[0:1, :])
        return carry

    lax.fori_loop(0, PEER_HEADS, head, 0)


def _select(st):
    S = st.shape[-1]
    tb = SEL_TB
    out = jax.ShapeDtypeStruct((PEER_HEADS, PEER_NKEYS, S), F32)
    ospec = pl.BlockSpec((PEER_HEADS, PEER_NKEYS, tb), lambda i: (0, 0, i))
    return pl.pallas_call(
        _select_body,
        grid=(S // tb,),
        in_specs=[pl.BlockSpec((2 * PEER_HEADS, PEER_NKEYS, tb), lambda i: (0, 0, i))],
        out_specs=[ospec] * 3,
        out_shape=[out] * 3,
        scratch_shapes=[pltpu.VMEM((PEER_TOPK, tb), F32), pltpu.VMEM((PEER_TOPK, tb), F32)],
        compiler_params=_params(("parallel",)),
    )(st)


def _peer_body(hn_ref, h1_ref, u_ref, vt_ref, s2_ref, e2_ref, thr_ref, e1n_ref, o_ref, acc, p_s):
    j = pl.program_id(1)
    tm = hn_ref.shape[0]
    ec = u_ref.shape[0]
    n_i1 = ec // PEER_NKEYS

    @pl.when(j == 0)
    def _():
        acc[...] = jnp.zeros(acc.shape, F32)

    act = jax.nn.gelu(lax.dot_general(u_ref[...], hn_ref[...], (((1,), (1,)), ((), ())),
                                      preferred_element_type=F32))
    for kk in range(n_i1):
        i1 = j * n_i1 + kk
        w = jnp.zeros((PEER_NKEYS, tm), F32)
        for h in range(PEER_HEADS):
            thr_b = thr_ref[h, pl.ds(i1, 1), :]
            e1_b = e1n_ref[h, pl.ds(i1, 1), :]
            w = w + jnp.where(s2_ref[h] >= thr_b, e2_ref[h], 0.0) * e1_b
        sl = slice(kk * PEER_NKEYS, (kk + 1) * PEER_NKEYS)
        p_s[sl, :] = (w * act[sl, :]).astype(BF16)
    acc[...] += jnp.dot(vt_ref[...], p_s[...], preferred_element_type=F32)

    @pl.when(j == pl.num_programs(1) - 1)
    def _():
        o_ref[...] = h1_ref[...] + acc[...].T


def _peer(hn, h1, u, vt, st4, e2, thr, e1n):
    S = hn.shape[0]
    tm, ec = PEER_TM, PEER_EC
    tok = lambda n: pl.BlockSpec((tm, n), lambda i, j: (i, 0))
    sel = pl.BlockSpec((PEER_HEADS, PEER_NKEYS, tm), lambda i, j: (0, 0, i))
    s2 = pl.BlockSpec((PEER_HEADS, None, PEER_NKEYS, tm), lambda i, j: (0, 1, 0, i))
    return pl.pallas_call(
        _peer_body,
        grid=(S // tm, PEER_EXPERTS // ec),
        in_specs=[tok(D_MODEL), tok(D_MODEL),
                  pl.BlockSpec((ec, D_MODEL), lambda i, j: (j, 0)),
                  pl.BlockSpec((D_MODEL, ec), lambda i, j: (0, j)),
                  s2, sel, sel, sel],
        out_specs=tok(D_MODEL),
        out_shape=jax.ShapeDtypeStruct((S, D_MODEL), F32),
        scratch_shapes=[pltpu.VMEM((D_MODEL, tm), F32), pltpu.VMEM((ec, tm), BF16)],
        compiler_params=_params(("parallel", "arbitrary")),
    )(hn, h1, u, vt, st4, e2, thr, e1n)


def _ple_body(h_ref, p_ref, gp_ref, wg_ref, bg_ref, wp_ref, gf_ref, o_ref):
    h = h_ref[...]
    hn = _rms(h, gp_ref[...]).astype(BF16)
    g = jax.nn.sigmoid(jnp.dot(hn, wg_ref[...], preferred_element_type=F32) + bg_ref[...])
    pp = jnp.dot(p_ref[...].astype(BF16), wp_ref[...], preferred_element_type=F32)
    o_ref[...] = _rms(h + g * pp, gf_ref[...])


def _ple(h, p, gp, wg, bg, wp, gf):
    S = h.shape[0]
    tm = PLE_TM
    row = lambda n: pl.BlockSpec((tm, n), lambda i: (i, 0))
    vec = _resident((1, D_MODEL))
    return pl.pallas_call(
        _ple_body,
        grid=(S // tm,),
        in_specs=[row(D_MODEL), row(PLE_DIM), vec, _resident(wg.shape), vec, _resident(wp.shape), vec],
        out_specs=row(D_MODEL),
        out_shape=jax.ShapeDtypeStruct((S, D_MODEL), F32),
        compiler_params=_params(("parallel",)),
    )(h, p, gp, wg, bg, wp, gf)


def _pair_blocks(w):
    w = w.reshape(8, 2, 64, 64)
    z = jnp.zeros((8, 64, 64), w.dtype)
    top = jnp.concatenate([w[:, 0], z], axis=-1)
    bot = jnp.concatenate([z, w[:, 1]], axis=-1)
    return jnp.concatenate([top, bot], axis=-2).astype(BF16)


def kernel(x, p, positions, mix_norm, w_in, conv_w, conv_b, rg_wa, rg_ba, rg_wx, rg_bx, rg_lambda,
           attn_out_norm, rnn_out_norm, w_out, ffn_norm, peer_wq, peer_subkeys, peer_u, peer_v,
           ple_norm, ple_w_gate, ple_b_gate, ple_proj, final_norm):
    B, S, D = x.shape
    assert (B, S, D) == (1, SEQ, D_MODEL) and S % SPAN == 0 and w_in.shape[0] == 1
    vec = lambda a: a.reshape(1, -1).astype(F32)

    half = ROPE_DIM // 2
    inv_freq = ROPE_THETA ** (-jnp.arange(half, dtype=F32) * 2.0 / ROPE_DIM)
    invf = jnp.zeros((1, HEAD_DIM), F32).at[0, :ROPE_DIM].set(jnp.tile(inv_freq, 2))

    q, k, v, xr, gate = _inproj(x[0], positions.reshape(S, 1), invf, vec(mix_norm[0]),
                                w_in[0].astype(BF16))
    attn = _attention(q, k, v)
    rnn = _rnn(xr, gate, conv_w[0], vec(conv_b[0]), _pair_blocks(rg_wa[0]), vec(rg_ba[0]),
               _pair_blocks(rg_wx[0]), vec(rg_bx[0]), vec(rg_lambda[0]), vec(rnn_out_norm[0]))
    sk = peer_subkeys[0].reshape(2 * PEER_HEADS, PEER_NKEYS, -1).astype(BF16)
    h1, hn, st = _outproj(attn, rnn, x[0], vec(attn_out_norm[0]), w_out[0].astype(BF16),
                          vec(ffn_norm[0]), peer_wq[0].astype(BF16), sk)
    thr, e1n, e2 = _select(st)
    h2 = _peer(hn, h1, peer_u[0].astype(BF16), peer_v[0].T.astype(BF16),
               st.reshape(PEER_HEADS, 2, PEER_NKEYS, S), e2, thr, e1n)
    out = _ple(h2, p[0, 0], vec(ple_norm[0]), ple_w_gate[0].astype(BF16), vec(ple_b_gate[0]),
               ple_proj[0].astype(BF16), vec(final_norm))
    return out.reshape(B, S, D)
```

```python
import functools

import jax
import jax.numpy as jnp
from jax import lax
from jax.experimental import pallas as pl
from jax.experimental.pallas import tpu as pltpu

F32 = jnp.float32
BF16 = jnp.bfloat16

D_MODEL = 2048
SEQ = 8192
HEAD_DIM = 128
ATTN_WIDTH = 1024
N_HEADS = 8
ROPE_DIM = 32
ROPE_THETA = 500000.0
ATTN_BLOCK = 128
DILATIONS = (1, 4, 16)
SPAN = ATTN_BLOCK * 16
RNN_WIDTH = 1024
CONV_WIDTH = 4
RGLRU_C = 8.0
PEER_HEADS = 8
PEER_NKEYS = 128
PEER_EXPERTS = PEER_NKEYS * PEER_NKEYS
PEER_TOPK = 16
PLE_DIM = 256
EPS = 1e-6

VMEM_LIMIT = 56 * 1024 * 1024

IN_TM = 512
RNN_TM = 256
OUT_TM = 256
SEL_TB = 256
PEER_TM = 512
PEER_EC = 512
PLE_TM = 512


def _params(sem):
    return pltpu.CompilerParams(dimension_semantics=sem, vmem_limit_bytes=VMEM_LIMIT)


def _rms(x, g):
    return x * lax.rsqrt(jnp.mean(x * x, axis=-1, keepdims=True) + EPS) * g


def _resident(shape):
    nd = len(shape)
    return pl.BlockSpec(shape, lambda *_: (0,) * nd, pipeline_mode=pl.Buffered(1))


def _inproj_body(x_ref, pos_ref, invf_ref, g_ref, w_ref, q_ref, k_ref, v_ref, xr_ref, gate_ref):
    tm = x_ref.shape[0]
    xb = _rms(x_ref[...], g_ref[...]).astype(BF16)
    ang = pos_ref[...].astype(F32) * invf_ref[...]
    cosf = jnp.cos(ang)
    sinf = jnp.sin(ang)
    lane = lax.broadcasted_iota(jnp.int32, (tm, HEAD_DIM), 1)
    first_half = lane < ROPE_DIM // 2

    def rope(c):
        partner = jnp.where(first_half,
                            -pltpu.roll(c, HEAD_DIM - ROPE_DIM // 2, 1),
                            pltpu.roll(c, ROPE_DIM // 2, 1))
        return c * cosf + partner * sinf

    outs = (q_ref, k_ref, v_ref, xr_ref, gate_ref)
    for n, o_ref in enumerate(outs):
        y = jnp.dot(xb, w_ref[:, n * 1024:(n + 1) * 1024], preferred_element_type=F32)
        if n < 2:
            scale = HEAD_DIM ** -0.5 if n == 0 else 1.0
            for h in range(N_HEADS):
                sl = slice(h * HEAD_DIM, (h + 1) * HEAD_DIM)
                o_ref[:, sl] = rope(y[:, sl]) * scale
        else:
            o_ref[...] = y


def _inproj(x, pos, invf, g, w):
    S = x.shape[0]
    tm = IN_TM
    row = lambda n: pl.BlockSpec((tm, n), lambda i: (i, 0))
    out = jax.ShapeDtypeStruct((S, 1024), F32)
    return pl.pallas_call(
        _inproj_body,
        grid=(S // tm,),
        in_specs=[row(D_MODEL), row(1), _resident((1, HEAD_DIM)), _resident((1, D_MODEL)),
                  _resident(w.shape)],
        out_specs=[row(1024)] * 5,
        out_shape=[out] * 5,
        compiler_params=_params(("parallel",)),
    )(x, pos, invf, g, w)


def _attn_body(q_ref, k_ref, v_ref, o_ref, t4, q4, q16, k1, k4, k16, v1, v4, v16, op, ls):
    c = pl.program_id(1)
    kv = {1: (k1, v1), 4: (k4, v4), 16: (k16, v16)}

    @pl.when(c == 0)
    def _():
        for d in DILATIONS:
            for ref in kv[d]:
                ref[:, 0:ATTN_BLOCK, :] = jnp.zeros((d, ATTN_BLOCK, HEAD_DIM), BF16)

    @pl.when(c > 0)
    def _():
        for d in DILATIONS:
            rows = SPAN // d
            for ref in kv[d]:
                ref[:, 0:ATTN_BLOCK, :] = ref[:, rows:rows + ATTN_BLOCK, :]

    def deinterleave(x_ref, x1, x4, x16, off):
        if x1 is not None:
            x1[0, off:off + SPAN, :] = x_ref[...].astype(BF16)
        for r4 in range(4):
            t = x_ref[pl.ds(r4, SPAN // 4, stride=4), :]
            t4[r4] = t
            x4[r4, off:off + SPAN // 4, :] = t.astype(BF16)
        for r4 in range(4):
            for r2 in range(4):
                t = t4[r4, pl.ds(r2, SPAN // 16, stride=4), :]
                x16[4 * r2 + r4, off:off + SPAN // 16, :] = t.astype(BF16)

    deinterleave(q_ref, None, q4, q16, 0)
    deinterleave(k_ref, k1, k4, k16, ATTN_BLOCK)
    deinterleave(v_ref, v1, v4, v16, ATTN_BLOCK)

    qi = lax.broadcasted_iota(jnp.int32, (ATTN_BLOCK, 2 * ATTN_BLOCK), 0)
    kj = lax.broadcasted_iota(jnp.int32, (ATTN_BLOCK, 2 * ATTN_BLOCK), 1)
    dist = ATTN_BLOCK + qi - kj
    band = (dist >= 0) & (dist <= ATTN_BLOCK)

    def tile(qt, kk, vv, has_prev):
        s = lax.dot_general(qt, kk, (((1,), (1,)), ((), ())), preferred_element_type=F32)
        first_key = jnp.where(has_prev, 0, ATTN_BLOCK)
        mask = band & (kj >= first_key)
        s = jnp.where(mask, s, -jnp.inf)
        m = jnp.max(s, axis=-1, keepdims=True)
        e = jnp.exp(s - m)
        l = jnp.sum(e, axis=-1, keepdims=True)
        o = jnp.dot(e.astype(BF16), vv, preferred_element_type=F32)
        o = o / l
        lse = jnp.broadcast_to(m + jnp.log(l), (ATTN_BLOCK, HEAD_DIM))
        return o, lse

    prev_span = c > 0

    def run_pattern(p_idx, d, q_tile):
        k_s, v_s = kv[d]
        nblk = SPAN // d // ATTN_BLOCK

        def body(idx, carry):
            r = idx // nblk
            n = idx % nblk
            row0 = pl.multiple_of(n * ATTN_BLOCK, ATTN_BLOCK)
            kk = k_s[r, pl.ds(row0, 2 * ATTN_BLOCK), :]
            vv = v_s[r, pl.ds(row0, 2 * ATTN_BLOCK), :]
            o, lse = tile(q_tile(r, row0), kk, vv, (n > 0) | prev_span)
            start = n * (ATTN_BLOCK * d) + r
            if d == 1:
                dst = pl.ds(pl.multiple_of(start, ATTN_BLOCK), ATTN_BLOCK)
            else:
                dst = pl.ds(start, ATTN_BLOCK, stride=d)
            op[p_idx, dst, :] = o
            ls[p_idx, dst, :] = lse
            return carry

        lax.fori_loop(0, d * nblk, body, 0)

    run_pattern(0, 1, lambda r, row0: q_ref[pl.ds(row0, ATTN_BLOCK), :].astype(BF16))
    run_pattern(1, 4, lambda r, row0: q4[r, pl.ds(row0, ATTN_BLOCK), :])
    run_pattern(2, 16, lambda r, row0: q16[r, pl.ds(row0, ATTN_BLOCK), :])

    l0, l1, l2 = ls[0], ls[1], ls[2]
    top = jnp.maximum(jnp.maximum(l0, l1), l2)
    w0 = jnp.exp(l0 - top)
    w1 = jnp.exp(l1 - top)
    w2 = jnp.exp(l2 - top)
    o_ref[...] = (w0 * op[0] + w1 * op[1] + w2 * op[2]) / (w0 + w1 + w2)


def _attention(q, k, v):
    S = q.shape[0]
    spec = pl.BlockSpec((SPAN, HEAD_DIM), lambda h, c: (c, h))
    slab = lambda d, off, dt: pltpu.VMEM((d, off + SPAN // d, HEAD_DIM), dt)
    scratch = [
        slab(4, 0, F32), slab(4, 0, BF16), slab(16, 0, BF16),
        slab(1, ATTN_BLOCK, BF16), slab(4, ATTN_BLOCK, BF16), slab(16, ATTN_BLOCK, BF16),
        slab(1, ATTN_BLOCK, BF16), slab(4, ATTN_BLOCK, BF16), slab(16, ATTN_BLOCK, BF16),
        pltpu.VMEM((3, SPAN, HEAD_DIM), F32), pltpu.VMEM((3, SPAN, HEAD_DIM), F32),
    ]
    return pl.pallas_call(
        _attn_body,
        grid=(N_HEADS, S // SPAN),
        in_specs=[spec] * 3,
        out_specs=spec,
        out_shape=jax.ShapeDtypeStruct((S, ATTN_WIDTH), F32),
        scratch_shapes=scratch,
        compiler_params=_params(("arbitrary", "arbitrary")),
    )(q, k, v)


def _rnn_body(xr_ref, gate_ref, cw_ref, cb_ref, wa_ref, ba_ref, wx_ref, bx_ref, lam_ref, g_ref,
              o_ref, xext, hcar, a_s, u_s):
    tm = xr_ref.shape[0]
    i = pl.program_id(0)

    @pl.when(i == 0)
    def _():
        xext[0:8, :] = jnp.zeros((8, RNN_WIDTH), F32)
        hcar[...] = jnp.zeros((8, RNN_WIDTH), F32)

    x = xr_ref[...]
    xext[8:8 + tm, :] = x
    xc = cb_ref[...]
    for tap in range(CONV_WIDTH):
        sh = CONV_WIDTH - 1 - tap
        xc = xc + xext[8 - sh:8 - sh + tm, :] * cw_ref[tap:tap + 1, :]
    xext[0:8, :] = x[tm - 8:tm, :]

    xcb = xc.astype(BF16)
    for j in range(RNN_WIDTH // 128):
        sl = slice(j * 128, (j + 1) * 128)
        blk = xcb[:, sl]
        r = jax.nn.sigmoid(jnp.dot(blk, wa_ref[j], preferred_element_type=F32) + ba_ref[:, sl])
        ig = jax.nn.sigmoid(jnp.dot(blk, wx_ref[j], preferred_element_type=F32) + bx_ref[:, sl])
        log_a = -RGLRU_C * r * jax.nn.softplus(-lam_ref[:, sl])
        a = jnp.exp(log_a)
        a_s[:, sl] = a
        u_s[:, sl] = jnp.sqrt(-jnp.tanh(log_a) * (a * a + 1.0)) * ig * xc[:, sl]

    row = lax.broadcasted_iota(jnp.int32, (8, RNN_WIDTH), 0)

    def scan(t, h):
        base = pl.multiple_of(t * 8, 8)
        A = a_s[pl.ds(base, 8), :]
        B = u_s[pl.ds(base, 8), :]
        for sft in (1, 2, 4):
            valid = row >= sft
            A_sh = pltpu.roll(A, sft, 0)
            B_sh = pltpu.roll(B, sft, 0)
            B = jnp.where(valid, A * B_sh + B, B)
            A = jnp.where(valid, A * A_sh, A)
        hs = A * h + B
        u_s[pl.ds(base, 8), :] = hs
        return jnp.broadcast_to(hs[7:8, :], (8, RNN_WIDTH))

    hcar[...] = lax.fori_loop(0, tm // 8, scan, hcar[...])

    rnn = u_s[...] * jax.nn.gelu(gate_ref[...])
    o_ref[...] = _rms(rnn, g_ref[...]).astype(BF16)


def _rnn(xr, gate, cw, cb, wa, ba, wx, bx, lam, g):
    S = xr.shape[0]
    tm = RNN_TM
    row = pl.BlockSpec((tm, RNN_WIDTH), lambda i: (i, 0))
    vec = _resident((1, RNN_WIDTH))
    return pl.pallas_call(
        _rnn_body,
        grid=(S // tm,),
        in_specs=[row, row, _resident(cw.shape), vec, _resident(wa.shape), vec,
                  _resident(wx.shape), vec, vec, vec],
        out_specs=row,
        out_shape=jax.ShapeDtypeStruct((S, RNN_WIDTH), BF16),
        scratch_shapes=[pltpu.VMEM((tm + 8, RNN_WIDTH), F32), pltpu.VMEM((8, RNN_WIDTH), F32),
                        pltpu.VMEM((tm, RNN_WIDTH), F32), pltpu.VMEM((tm, RNN_WIDTH), F32)],
        compiler_params=_params(("arbitrary",)),
    )(xr, gate, cw, cb, wa, ba, wx, bx, lam, g)


def _outproj_body(attn_ref, rnn_ref, x_ref, ga_ref, wo_ref, gf_ref, wq_ref, sk_ref,
                  h1_ref, hnt_ref, st_ref):
    an = _rms(attn_ref[...], ga_ref[...]).astype(BF16)
    mix = jnp.dot(an, wo_ref[0:ATTN_WIDTH, :], preferred_element_type=F32)
    mix = mix + jnp.dot(rnn_ref[...], wo_ref[ATTN_WIDTH:, :], preferred_element_type=F32)
    h1 = x_ref[...] + mix
    h1_ref[...] = h1
    hn32 = _rms(h1, gf_ref[...])
    hn = hn32.astype(BF16)
    hnt_ref[...] = hn32.T.astype(BF16)
    qp = jnp.dot(hn, wq_ref[...], preferred_element_type=F32).astype(BF16)
    for hc in range(2 * PEER_HEADS):
        st_ref[hc] = lax.dot_general(sk_ref[hc], qp[:, hc * 128:(hc + 1) * 128],
                                     (((1,), (1,)), ((), ())), preferred_element_type=F32)


def _outproj(attn, rnn, x, ga, wo, gf, wq, sk):
    S = x.shape[0]
    tm = OUT_TM
    row = lambda n: pl.BlockSpec((tm, n), lambda i: (i, 0))
    return pl.pallas_call(
        _outproj_body,
        grid=(S // tm,),
        in_specs=[row(ATTN_WIDTH), row(RNN_WIDTH), row(D_MODEL), _resident((1, ATTN_WIDTH)),
                  _resident(wo.shape), _resident((1, D_MODEL)), _resident(wq.shape),
                  _resident(sk.shape)],
        out_specs=[row(D_MODEL), pl.BlockSpec((D_MODEL, tm), lambda i: (0, i)),
                   pl.BlockSpec((2 * PEER_HEADS, PEER_NKEYS, tm), lambda i: (0, 0, i))],
        out_shape=[jax.ShapeDtypeStruct((S, D_MODEL), F32), jax.ShapeDtypeStruct((D_MODEL, S), BF16),
                   jax.ShapeDtypeStruct((2 * PEER_HEADS, PEER_NKEYS, S), F32)],
        compiler_params=_params(("parallel",)),
    )(attn, rnn, x, ga, wo, gf, wq, sk)


def _select_body(st_ref, r2_ref, e2_ref, jb_ref, e1n_ref, a_s, b_s):
    tb = st_ref.shape[-1]
    K = PEER_TOPK
    ninf = -jnp.inf
    row8 = lax.broadcasted_iota(jnp.int32, (8, tb), 0)

    def top_values(s, out_s):
        def rnd(k, s):
            mx = jnp.max(s, axis=0, keepdims=True)
            out_s[pl.ds(k, 1), :] = mx
            return jnp.where(s == mx, ninf, s)
        lax.fori_loop(0, K, rnd, s)

    def head(h, carry):
        s1 = st_ref[2 * h]
        s2 = st_ref[2 * h + 1]
        top_values(s1, a_s)
        top_values(s2, b_s)
        A = a_s[...]
        B = b_s[...]
        B8 = B[0:8, :]
        limits = (16, 8, 5, 4, 3, 2, 2, 2)
        cands = [A[0:1, :] + B, A[1:2, :] + B8]
        for i in range(2, 8):
            cands.append(jnp.where(row8 < limits[i], A[i:i + 1, :] + B8, ninf))
        cands.append(A[8:16, :] + B[0:1, :])
        cand = jnp.concatenate(cands, axis=0)

        def rnd(k, st):
            cnd, _ = st
            mx = jnp.max(cnd, axis=0, keepdims=True)
            return jnp.where(cnd == mx, ninf, cnd), mx
        _, tau = lax.fori_loop(0, K, rnd, (cand, jnp.zeros((1, tb), F32)))

        cmax = A[0:1, :] + B[0:1, :]
        z = jnp.sum(jnp.where(cand >= tau, jnp.exp(cand - cmax), 0.0), axis=0, keepdims=True)

        j_rows = [jnp.sum(jnp.where(cands[i] >= tau, 1.0, 0.0), axis=0, keepdims=True)
                  for i in range(8)]
        tail = jnp.where(cands[8] >= tau, 1.0, 0.0)
        for i in range(8):
            j_rows.append(tail[i:i + 1, :])

        jb = jnp.zeros((PEER_NKEYS, tb), F32)
        r2 = jnp.full((PEER_NKEYS, tb), float(K), F32)
        for i in range(K):
            jb = jnp.where(s1 == A[i:i + 1, :], j_rows[i], jb)
            r2 = jnp.where(s2 == B[i:i + 1, :], float(i), r2)
        jb_ref[h] = jb
        r2_ref[h] = r2.astype(BF16)
        e1n_ref[h] = jnp.exp(s1 - A[0:1, :]) / z
        e2_ref[h] = jnp.exp(s2 - B[0:1, :]).astype(BF16)
        return carry

    lax.fori_loop(0, PEER_HEADS, head, 0)


def _select(st):
    S = st.shape[-1]
    tb = SEL_TB
    out = lambda dt: jax.ShapeDtypeStruct((PEER_HEADS, PEER_NKEYS, S), dt)
    ospec = pl.BlockSpec((PEER_HEADS, PEER_NKEYS, tb), lambda i: (0, 0, i))
    return pl.pallas_call(
        _select_body,
        grid=(S // tb,),
        in_specs=[pl.BlockSpec((2 * PEER_HEADS, PEER_NKEYS, tb), lambda i: (0, 0, i))],
        out_specs=[ospec] * 4,
        out_shape=[out(BF16), out(BF16), out(F32), out(F32)],
        scratch_shapes=[pltpu.VMEM((PEER_TOPK, tb), F32), pltpu.VMEM((PEER_TOPK, tb), F32)],
        compiler_params=_params(("parallel",)),
    )(st)


def _peer_body(hnt_ref, h1_ref, u_ref, vt_ref, r2_ref, e2_ref, jb_ref, e1n_ref, o_ref, acc, w_s):
    j = pl.program_id(1)
    tm = hnt_ref.shape[1]
    ec = u_ref.shape[0]
    n_i1 = ec // PEER_NKEYS

    @pl.when(j == 0)
    def _():
        acc[...] = jnp.zeros(acc.shape, F32)

    zero = jnp.zeros((), BF16)
    for kk in range(n_i1):
        i1 = j * n_i1 + kk
        w = None
        for h in range(PEER_HEADS):
            jb_b = jb_ref[h, pl.ds(i1, 1), :].astype(BF16)
            e1_b = e1n_ref[h, pl.ds(i1, 1), :].astype(BF16)
            t = jnp.where(r2_ref[h] < jb_b, e2_ref[h], zero) * e1_b
            w = t if w is None else w + t
        w_s[kk * PEER_NKEYS:(kk + 1) * PEER_NKEYS, :] = w

    half = tm // 2
    for th in range(2):
        tsl = slice(th * half, (th + 1) * half)
        act = jnp.dot(u_ref[...], hnt_ref[:, tsl], preferred_element_type=F32)
        p = w_s[:, tsl] * jax.nn.gelu(act).astype(BF16)
        acc[:, tsl] += jnp.dot(vt_ref[...], p, preferred_element_type=F32)

    @pl.when(j == pl.num_programs(1) - 1)
    def _():
        o_ref[...] = h1_ref[...] + acc[...].T


def _peer(hnt, h1, u, vt, r2, e2, jb, e1n):
    S = h1.shape[0]
    tm, ec = PEER_TM, PEER_EC
    tok = pl.BlockSpec((tm, D_MODEL), lambda i, j: (i, 0))
    sel = pl.BlockSpec((PEER_HEADS, PEER_NKEYS, tm), lambda i, j: (0, 0, i))
    return pl.pallas_call(
        _peer_body,
        grid=(S // tm, PEER_EXPERTS // ec),
        in_specs=[pl.BlockSpec((D_MODEL, tm), lambda i, j: (0, i)), tok,
                  pl.BlockSpec((ec, D_MODEL), lambda i, j: (j, 0)),
                  pl.BlockSpec((D_MODEL, ec), lambda i, j: (0, j)),
                  sel, sel, sel, sel],
        out_specs=tok,
        out_shape=jax.ShapeDtypeStruct((S, D_MODEL), F32),
        scratch_shapes=[pltpu.VMEM((D_MODEL, tm), F32), pltpu.VMEM((ec, tm), BF16)],
        compiler_params=_params(("parallel", "arbitrary")),
    )(hnt, h1, u, vt, r2, e2, jb, e1n)


def _ple_body(h_ref, p_ref, gp_ref, wg_ref, bg_ref, wp_ref, gf_ref, o_ref):
    h = h_ref[...]
    hn = _rms(h, gp_ref[...]).astype(BF16)
    g = jax.nn.sigmoid(jnp.dot(hn, wg_ref[...], preferred_element_type=F32) + bg_ref[...])
    pp = jnp.dot(p_ref[...].astype(BF16), wp_ref[...], preferred_element_type=F32)
    o_ref[...] = _rms(h + g * pp, gf_ref[...])


def _ple(h, p, gp, wg, bg, wp, gf):
    S = h.shape[0]
    tm = PLE_TM
    row = lambda n: pl.BlockSpec((tm, n), lambda i: (i, 0))
    vec = _resident((1, D_MODEL))
    return pl.pallas_call(
        _ple_body,
        grid=(S // tm,),
        in_specs=[row(D_MODEL), row(PLE_DIM), vec, _resident(wg.shape), vec, _resident(wp.shape), vec],
        out_specs=row(D_MODEL),
        out_shape=jax.ShapeDtypeStruct((S, D_MODEL), F32),
        compiler_params=_params(("parallel",)),
    )(h, p, gp, wg, bg, wp, gf)


def _pair_blocks(w):
    w = w.reshape(8, 2, 64, 64)
    z = jnp.zeros((8, 64, 64), w.dtype)
    top = jnp.concatenate([w[:, 0], z], axis=-1)
    bot = jnp.concatenate([z, w[:, 1]], axis=-1)
    return jnp.concatenate([top, bot], axis=-2).astype(BF16)


def kernel(x, p, positions, mix_norm, w_in, conv_w, conv_b, rg_wa, rg_ba, rg_wx, rg_bx, rg_lambda,
           attn_out_norm, rnn_out_norm, w_out, ffn_norm, peer_wq, peer_subkeys, peer_u, peer_v,
           ple_norm, ple_w_gate, ple_b_gate, ple_proj, final_norm):
    B, S, D = x.shape
    assert (B, S, D) == (1, SEQ, D_MODEL) and S % SPAN == 0 and w_in.shape[0] == 1
    vec = lambda a: a.reshape(1, -1).astype(F32)

    half = ROPE_DIM // 2
    inv_freq = ROPE_THETA ** (-jnp.arange(half, dtype=F32) * 2.0 / ROPE_DIM)
    invf = jnp.zeros((1, HEAD_DIM), F32).at[0, :ROPE_DIM].set(jnp.tile(inv_freq, 2))

    q, k, v, xr, gate = _inproj(x[0], positions.reshape(S, 1), invf, vec(mix_norm[0]),
                                w_in[0].astype(BF16))
    attn = _attention(q, k, v)
    rnn = _rnn(xr, gate, conv_w[0], vec(conv_b[0]), _pair_blocks(rg_wa[0]), vec(rg_ba[0]),
               _pair_blocks(rg_wx[0]), vec(rg_bx[0]), vec(rg_lambda[0]), vec(rnn_out_norm[0]))
    sk = peer_subkeys[0].reshape(2 * PEER_HEADS, PEER_NKEYS, -1).astype(BF16)
    h1, hnt, st = _outproj(attn, rnn, x[0], vec(attn_out_norm[0]), w_out[0].astype(BF16),
                           vec(ffn_norm[0]), peer_wq[0].astype(BF16), sk)
    r2, e2, jb, e1n = _select(st)
    h2 = _peer(hnt, h1, peer_u[0].astype(BF16), peer_v[0].T.astype(BF16), r2, e2, jb, e1n)
    out = _ple(h2, p[0, 0], vec(ple_norm[0]), ple_w_gate[0].astype(BF16), vec(ple_b_gate[0]),
               ple_proj[0].astype(BF16), vec(final_norm))
    return out.reshape(B, S, D)
```

```python
import functools

import jax
import jax.numpy as jnp
from jax import lax
from jax.experimental import pallas as pl
from jax.experimental.pallas import tpu as pltpu

F32 = jnp.float32
BF16 = jnp.bfloat16

D_MODEL = 2048
SEQ = 8192
HEAD_DIM = 128
ATTN_WIDTH = 1024
N_HEADS = 8
ROPE_DIM = 32
ROPE_THETA = 500000.0
ATTN_BLOCK = 128
DILATIONS = (1, 4, 16)
SPAN = ATTN_BLOCK * 16
RNN_WIDTH = 1024
CONV_WIDTH = 4
RGLRU_C = 8.0
PEER_HEADS = 8
PEER_NKEYS = 128
PEER_EXPERTS = PEER_NKEYS * PEER_NKEYS
PEER_TOPK = 16
PLE_DIM = 256
EPS = 1e-6

VMEM_LIMIT = 56 * 1024 * 1024

IN_TM = 512
RNN_TM = 256
OUT_TM = 256
SEL_TB = 256
PEER_TM = 512
PEER_EC = 512
PEER_CHUNKS = 2
PLE_TM = 512


def _params(sem):
    return pltpu.CompilerParams(dimension_semantics=sem, vmem_limit_bytes=VMEM_LIMIT)


def _rms(x, g):
    return x * lax.rsqrt(jnp.mean(x * x, axis=-1, keepdims=True) + EPS) * g


def _resident(shape):
    nd = len(shape)
    return pl.BlockSpec(shape, lambda *_: (0,) * nd, pipeline_mode=pl.Buffered(1))


def _inproj_body(x_ref, pos_ref, invf_ref, g_ref, w_ref, q_ref, k_ref, v_ref, xr_ref, gate_ref):
    tm = x_ref.shape[0]
    xb = _rms(x_ref[...], g_ref[...]).astype(BF16)
    ang = pos_ref[...].astype(F32) * invf_ref[...]
    cosf = jnp.cos(ang)
    sinf = jnp.sin(ang)
    lane = lax.broadcasted_iota(jnp.int32, (tm, HEAD_DIM), 1)
    first_half = lane < ROPE_DIM // 2

    def rope(c):
        partner = jnp.where(first_half,
                            -pltpu.roll(c, HEAD_DIM - ROPE_DIM // 2, 1),
                            pltpu.roll(c, ROPE_DIM // 2, 1))
        return c * cosf + partner * sinf

    outs = (q_ref, k_ref, v_ref, xr_ref, gate_ref)
    for n, o_ref in enumerate(outs):
        y = jnp.dot(xb, w_ref[:, n * 1024:(n + 1) * 1024], preferred_element_type=F32)
        if n < 2:
            scale = HEAD_DIM ** -0.5 if n == 0 else 1.0
            for h in range(N_HEADS):
                sl = slice(h * HEAD_DIM, (h + 1) * HEAD_DIM)
                o_ref[:, sl] = rope(y[:, sl]) * scale
        else:
            o_ref[...] = y


def _inproj(x, pos, invf, g, w):
    S = x.shape[0]
    tm = IN_TM
    row = lambda n: pl.BlockSpec((tm, n), lambda i: (i, 0))
    out = jax.ShapeDtypeStruct((S, 1024), F32)
    return pl.pallas_call(
        _inproj_body,
        grid=(S // tm,),
        in_specs=[row(D_MODEL), row(1), _resident((1, HEAD_DIM)), _resident((1, D_MODEL)),
                  _resident(w.shape)],
        out_specs=[row(1024)] * 5,
        out_shape=[out] * 5,
        compiler_params=_params(("parallel",)),
    )(x, pos, invf, g, w)


def _attn_body(q_ref, k_ref, v_ref, o_ref, t4, q4, q16, k1, k4, k16, v1, v4, v16, op, ls):
    c = pl.program_id(1)
    kv = {1: (k1, v1), 4: (k4, v4), 16: (k16, v16)}

    @pl.when(c == 0)
    def _():
        for d in DILATIONS:
            for ref in kv[d]:
                ref[:, 0:ATTN_BLOCK, :] = jnp.zeros((d, ATTN_BLOCK, HEAD_DIM), BF16)

    @pl.when(c > 0)
    def _():
        for d in DILATIONS:
            rows = SPAN // d
            for ref in kv[d]:
                ref[:, 0:ATTN_BLOCK, :] = ref[:, rows:rows + ATTN_BLOCK, :]

    def deinterleave(x_ref, x1, x4, x16, off):
        if x1 is not None:
            x1[0, off:off + SPAN, :] = x_ref[...].astype(BF16)
        for r4 in range(4):
            t = x_ref[pl.ds(r4, SPAN // 4, stride=4), :]
            t4[r4] = t
            x4[r4, off:off + SPAN // 4, :] = t.astype(BF16)
        for r4 in range(4):
            for r2 in range(4):
                t = t4[r4, pl.ds(r2, SPAN // 16, stride=4), :]
                x16[4 * r2 + r4, off:off + SPAN // 16, :] = t.astype(BF16)

    deinterleave(q_ref, None, q4, q16, 0)
    deinterleave(k_ref, k1, k4, k16, ATTN_BLOCK)
    deinterleave(v_ref, v1, v4, v16, ATTN_BLOCK)

    qi = lax.broadcasted_iota(jnp.int32, (ATTN_BLOCK, 2 * ATTN_BLOCK), 0)
    kj = lax.broadcasted_iota(jnp.int32, (ATTN_BLOCK, 2 * ATTN_BLOCK), 1)
    dist = ATTN_BLOCK + qi - kj
    band = (dist >= 0) & (dist <= ATTN_BLOCK)

    def tile(qt, kk, vv, has_prev):
        s = lax.dot_general(qt, kk, (((1,), (1,)), ((), ())), preferred_element_type=F32)
        first_key = jnp.where(has_prev, 0, ATTN_BLOCK)
        mask = band & (kj >= first_key)
        s = jnp.where(mask, s, -jnp.inf)
        m = jnp.max(s, axis=-1, keepdims=True)
        e = jnp.exp(s - m)
        l = jnp.sum(e, axis=-1, keepdims=True)
        o = jnp.dot(e.astype(BF16), vv, preferred_element_type=F32)
        o = o / l
        lse = jnp.broadcast_to(m + jnp.log(l), (ATTN_BLOCK, HEAD_DIM))
        return o, lse

    prev_span = c > 0

    def run_pattern(p_idx, d, q_tile):
        k_s, v_s = kv[d]
        nblk = SPAN // d // ATTN_BLOCK

        def body(idx, carry):
            r = idx // nblk
            n = idx % nblk
            row0 = pl.multiple_of(n * ATTN_BLOCK, ATTN_BLOCK)
            kk = k_s[r, pl.ds(row0, 2 * ATTN_BLOCK), :]
            vv = v_s[r, pl.ds(row0, 2 * ATTN_BLOCK), :]
            o, lse = tile(q_tile(r, row0), kk, vv, (n > 0) | prev_span)
            start = n * (ATTN_BLOCK * d) + r
            if d == 1:
                dst = pl.ds(pl.multiple_of(start, ATTN_BLOCK), ATTN_BLOCK)
            else:
                dst = pl.ds(start, ATTN_BLOCK, stride=d)
            op[p_idx, dst, :] = o
            ls[p_idx, dst, :] = lse
            return carry

        lax.fori_loop(0, d * nblk, body, 0)

    run_pattern(0, 1, lambda r, row0: q_ref[pl.ds(row0, ATTN_BLOCK), :].astype(BF16))
    run_pattern(1, 4, lambda r, row0: q4[r, pl.ds(row0, ATTN_BLOCK), :])
    run_pattern(2, 16, lambda r, row0: q16[r, pl.ds(row0, ATTN_BLOCK), :])

    l0, l1, l2 = ls[0], ls[1], ls[2]
    top = jnp.maximum(jnp.maximum(l0, l1), l2)
    w0 = jnp.exp(l0 - top)
    w1 = jnp.exp(l1 - top)
    w2 = jnp.exp(l2 - top)
    o_ref[...] = (w0 * op[0] + w1 * op[1] + w2 * op[2]) / (w0 + w1 + w2)


def _attention(q, k, v):
    S = q.shape[0]
    spec = pl.BlockSpec((SPAN, HEAD_DIM), lambda h, c: (c, h))
    slab = lambda d, off, dt: pltpu.VMEM((d, off + SPAN // d, HEAD_DIM), dt)
    scratch = [
        slab(4, 0, F32), slab(4, 0, BF16), slab(16, 0, BF16),
        slab(1, ATTN_BLOCK, BF16), slab(4, ATTN_BLOCK, BF16), slab(16, ATTN_BLOCK, BF16),
        slab(1, ATTN_BLOCK, BF16), slab(4, ATTN_BLOCK, BF16), slab(16, ATTN_BLOCK, BF16),
        pltpu.VMEM((3, SPAN, HEAD_DIM), F32), pltpu.VMEM((3, SPAN, HEAD_DIM), F32),
    ]
    return pl.pallas_call(
        _attn_body,
        grid=(N_HEADS, S // SPAN),
        in_specs=[spec] * 3,
        out_specs=spec,
        out_shape=jax.ShapeDtypeStruct((S, ATTN_WIDTH), F32),
        scratch_shapes=scratch,
        compiler_params=_params(("arbitrary", "arbitrary")),
    )(q, k, v)


def _rnn_body(xr_ref, gate_ref, cw_ref, cb_ref, wa_ref, ba_ref, wx_ref, bx_ref, lam_ref, g_ref,
              o_ref, xext, hcar, a_s, u_s):
    tm = xr_ref.shape[0]
    i = pl.program_id(0)

    @pl.when(i == 0)
    def _():
        xext[0:8, :] = jnp.zeros((8, RNN_WIDTH), F32)
        hcar[...] = jnp.zeros((8, RNN_WIDTH), F32)

    x = xr_ref[...]
    xext[8:8 + tm, :] = x
    xc = cb_ref[...]
    for tap in range(CONV_WIDTH):
        sh = CONV_WIDTH - 1 - tap
        xc = xc + xext[8 - sh:8 - sh + tm, :] * cw_ref[tap:tap + 1, :]
    xext[0:8, :] = x[tm - 8:tm, :]

    xcb = xc.astype(BF16)
    for j in range(RNN_WIDTH // 128):
        sl = slice(j * 128, (j + 1) * 128)
        blk = xcb[:, sl]
        r = jax.nn.sigmoid(jnp.dot(blk, wa_ref[j], preferred_element_type=F32) + ba_ref[:, sl])
        ig = jax.nn.sigmoid(jnp.dot(blk, wx_ref[j], preferred_element_type=F32) + bx_ref[:, sl])
        log_a = -RGLRU_C * r * jax.nn.softplus(-lam_ref[:, sl])
        a = jnp.exp(log_a)
        a_s[:, sl] = a
        u_s[:, sl] = jnp.sqrt(-jnp.tanh(log_a) * (a * a + 1.0)) * ig * xc[:, sl]

    row = lax.broadcasted_iota(jnp.int32, (8, RNN_WIDTH), 0)

    def scan(t, h):
        base = pl.multiple_of(t * 8, 8)
        A = a_s[pl.ds(base, 8), :]
        B = u_s[pl.ds(base, 8), :]
        for sft in (1, 2, 4):
            valid = row >= sft
            A_sh = pltpu.roll(A, sft, 0)
            B_sh = pltpu.roll(B, sft, 0)
            B = jnp.where(valid, A * B_sh + B, B)
            A = jnp.where(valid, A * A_sh, A)
        hs = A * h + B
        u_s[pl.ds(base, 8), :] = hs
        return jnp.broadcast_to(hs[7:8, :], (8, RNN_WIDTH))

    hcar[...] = lax.fori_loop(0, tm // 8, scan, hcar[...])

    rnn = u_s[...] * jax.nn.gelu(gate_ref[...])
    o_ref[...] = _rms(rnn, g_ref[...]).astype(BF16)


def _rnn(xr, gate, cw, cb, wa, ba, wx, bx, lam, g):
    S = xr.shape[0]
    tm = RNN_TM
    row = pl.BlockSpec((tm, RNN_WIDTH), lambda i: (i, 0))
    vec = _resident((1, RNN_WIDTH))
    return pl.pallas_call(
        _rnn_body,
        grid=(S // tm,),
        in_specs=[row, row, _resident(cw.shape), vec, _resident(wa.shape), vec,
                  _resident(wx.shape), vec, vec, vec],
        out_specs=row,
        out_shape=jax.ShapeDtypeStruct((S, RNN_WIDTH), BF16),
        scratch_shapes=[pltpu.VMEM((tm + 8, RNN_WIDTH), F32), pltpu.VMEM((8, RNN_WIDTH), F32),
                        pltpu.VMEM((tm, RNN_WIDTH), F32), pltpu.VMEM((tm, RNN_WIDTH), F32)],
        compiler_params=_params(("arbitrary",)),
    )(xr, gate, cw, cb, wa, ba, wx, bx, lam, g)


def _outproj_body(attn_ref, rnn_ref, x_ref, ga_ref, wo_ref, gf_ref, wq_ref, sk_ref,
                  h1_ref, hnt_ref, st_ref):
    an = _rms(attn_ref[...], ga_ref[...]).astype(BF16)
    mix = jnp.dot(an, wo_ref[0:ATTN_WIDTH, :], preferred_element_type=F32)
    mix = mix + jnp.dot(rnn_ref[...], wo_ref[ATTN_WIDTH:, :], preferred_element_type=F32)
    h1 = x_ref[...] + mix
    h1_ref[...] = h1
    hn32 = _rms(h1, gf_ref[...])
    hn = hn32.astype(BF16)
    hnt_ref[...] = hn32.T.astype(BF16)
    qp = jnp.dot(hn, wq_ref[...], preferred_element_type=F32).astype(BF16)
    for hc in range(2 * PEER_HEADS):
        st_ref[hc] = lax.dot_general(sk_ref[hc], qp[:, hc * 128:(hc + 1) * 128],
                                     (((1,), (1,)), ((), ())), preferred_element_type=F32)


def _outproj(attn, rnn, x, ga, wo, gf, wq, sk):
    S = x.shape[0]
    tm = OUT_TM
    row = lambda n: pl.BlockSpec((tm, n), lambda i: (i, 0))
    return pl.pallas_call(
        _outproj_body,
        grid=(S // tm,),
        in_specs=[row(ATTN_WIDTH), row(RNN_WIDTH), row(D_MODEL), _resident((1, ATTN_WIDTH)),
                  _resident(wo.shape), _resident((1, D_MODEL)), _resident(wq.shape),
                  _resident(sk.shape)],
        out_specs=[row(D_MODEL), pl.BlockSpec((D_MODEL, tm), lambda i: (0, i)),
                   pl.BlockSpec((2 * PEER_HEADS, PEER_NKEYS, tm), lambda i: (0, 0, i))],
        out_shape=[jax.ShapeDtypeStruct((S, D_MODEL), F32), jax.ShapeDtypeStruct((D_MODEL, S), BF16),
                   jax.ShapeDtypeStruct((2 * PEER_HEADS, PEER_NKEYS, S), F32)],
        compiler_params=_params(("parallel",)),
    )(attn, rnn, x, ga, wo, gf, wq, sk)


def _select_body(st_ref, r2_ref, e2_ref, jb_ref, e1n_ref, a_s, b_s):
    tb = st_ref.shape[-1]
    K = PEER_TOPK
    ninf = -jnp.inf
    row8 = lax.broadcasted_iota(jnp.int32, (8, tb), 0)

    def top_values(s, out_s):
        def rnd(k, s):
            mx = jnp.max(s, axis=0, keepdims=True)
            out_s[pl.ds(k, 1), :] = mx
            return jnp.where(s == mx, ninf, s)
        lax.fori_loop(0, K, rnd, s)

    def head(h, carry):
        s1 = st_ref[2 * h]
        s2 = st_ref[2 * h + 1]
        top_values(s1, a_s)
        top_values(s2, b_s)
        A = a_s[...]
        B = b_s[...]
        B8 = B[0:8, :]
        limits = (16, 8, 5, 4, 3, 2, 2, 2)
        cands = [A[0:1, :] + B, A[1:2, :] + B8]
        for i in range(2, 8):
            cands.append(jnp.where(row8 < limits[i], A[i:i + 1, :] + B8, ninf))
        cands.append(A[8:16, :] + B[0:1, :])
        cand = jnp.concatenate(cands, axis=0)

        def rnd(k, st):
            cnd, _ = st
            mx = jnp.max(cnd, axis=0, keepdims=True)
            return jnp.where(cnd == mx, ninf, cnd), mx
        _, tau = lax.fori_loop(0, K, rnd, (cand, jnp.zeros((1, tb), F32)))

        cmax = A[0:1, :] + B[0:1, :]
        z = jnp.sum(jnp.where(cand >= tau, jnp.exp(cand - cmax), 0.0), axis=0, keepdims=True)

        j_rows = [jnp.sum(jnp.where(cands[i] >= tau, 1.0, 0.0), axis=0, keepdims=True)
                  for i in range(8)]
        tail = jnp.where(cands[8] >= tau, 1.0, 0.0)
        for i in range(8):
            j_rows.append(tail[i:i + 1, :])

        jb = jnp.zeros((PEER_NKEYS, tb), F32)
        r2 = jnp.full((PEER_NKEYS, tb), float(K), F32)
        for i in range(K):
            jb = jnp.where(s1 == A[i:i + 1, :], j_rows[i], jb)
            r2 = jnp.where(s2 == B[i:i + 1, :], float(i), r2)
        jb_ref[h] = jb
        r2_ref[h] = r2.astype(BF16)
        e1n_ref[h] = jnp.exp(s1 - A[0:1, :]) / z
        e2_ref[h] = jnp.exp(s2 - B[0:1, :]).astype(BF16)
        return carry

    lax.fori_loop(0, PEER_HEADS, head, 0)


def _select(st):
    S = st.shape[-1]
    tb = SEL_TB
    out = lambda dt: jax.ShapeDtypeStruct((PEER_HEADS, PEER_NKEYS, S), dt)
    ospec = pl.BlockSpec((PEER_HEADS, PEER_NKEYS, tb), lambda i: (0, 0, i))
    return pl.pallas_call(
        _select_body,
        grid=(S // tb,),
        in_specs=[pl.BlockSpec((2 * PEER_HEADS, PEER_NKEYS, tb), lambda i: (0, 0, i))],
        out_specs=[ospec] * 4,
        out_shape=[out(BF16), out(BF16), out(F32), out(F32)],
        scratch_shapes=[pltpu.VMEM((PEER_TOPK, tb), F32), pltpu.VMEM((PEER_TOPK, tb), F32)],
        compiler_params=_params(("parallel",)),
    )(st)


def _peer_body(hnt_ref, u_ref, vt_ref, r2_ref, e2_ref, jb_ref, e1n_ref, o_ref, w_even, w_odd):
    j = pl.program_id(1)
    ec = PEER_EC
    n_i1 = ec // PEER_NKEYS

    zero = jnp.zeros((), BF16)

    def build_gates(step, w_s, c, kk):
        i1 = jnp.minimum((step * PEER_CHUNKS + c) * n_i1 + kk, PEER_NKEYS - 1)
        w = None
        for h in range(PEER_HEADS):
            jb_b = jb_ref[h, pl.ds(i1, 1), :].astype(BF16)
            e1_b = e1n_ref[h, pl.ds(i1, 1), :].astype(BF16)
            t = jnp.where(r2_ref[h] < jb_b, e2_ref[h], zero) * e1_b
            w = t if w is None else w + t
        w_s[c, kk * PEER_NKEYS:(kk + 1) * PEER_NKEYS, :] = w

    @pl.when(j == 0)
    def _():
        o_ref[...] = jnp.zeros(o_ref.shape, F32)
        for c in range(PEER_CHUNKS):
            for kk in range(n_i1):
                build_gates(0, w_even, c, kk)

    def step(w_cur, w_next):
        acts = [jnp.dot(u_ref[c * ec:(c + 1) * ec, :], hnt_ref[...], preferred_element_type=F32)
                for c in range(PEER_CHUNKS)]
        rows = D_MODEL // n_i1
        p = w_cur[0] * jax.nn.gelu(acts[0]).astype(BF16)
        for c in range(PEER_CHUNKS):
            p_next = []
            for kk in range(n_i1):
                rsl = slice(kk * rows, (kk + 1) * rows)
                o_ref[rsl, :] += jnp.dot(vt_ref[c, rsl, :], p, preferred_element_type=F32)
                if c + 1 < PEER_CHUNKS:
                    esl = slice(kk * PEER_NKEYS, (kk + 1) * PEER_NKEYS)
                    p_next.append(w_cur[c + 1, esl, :] * jax.nn.gelu(acts[c + 1][esl, :]).astype(BF16))
                build_gates(j + 1, w_next, c, kk)
            if p_next:
                p = jnp.concatenate(p_next, axis=0)

    @pl.when(j % 2 == 0)
    def _():
        step(w_even, w_odd)

    @pl.when(j % 2 == 1)
    def _():
        step(w_odd, w_even)


def _peer(hnt, u, vt, r2, e2, jb, e1n):
    S = hnt.shape[1]
    tm, ec, nc = PEER_TM, PEER_EC, PEER_CHUNKS
    tokt = pl.BlockSpec((D_MODEL, tm), lambda i, j: (0, i))
    sel = pl.BlockSpec((PEER_HEADS, PEER_NKEYS, tm), lambda i, j: (0, 0, i))
    return pl.pallas_call(
        _peer_body,
        grid=(S // tm, PEER_EXPERTS // (ec * nc)),
        in_specs=[tokt,
                  pl.BlockSpec((ec * nc, D_MODEL), lambda i, j: (j, 0)),
                  pl.BlockSpec((nc, D_MODEL, ec), lambda i, j: (j, 0, 0)),
                  sel, sel, sel, sel],
        out_specs=tokt,
        out_shape=jax.ShapeDtypeStruct((D_MODEL, S), F32),
        scratch_shapes=[pltpu.VMEM((nc, ec, tm), BF16), pltpu.VMEM((nc, ec, tm), BF16)],
        compiler_params=_params(("parallel", "arbitrary")),
    )(hnt, u, vt, r2, e2, jb, e1n)


def _ple_body(h_ref, ft_ref, p_ref, gp_ref, wg_ref, bg_ref, wp_ref, gf_ref, o_ref):
    h = h_ref[...] + ft_ref[...].T
    hn = _rms(h, gp_ref[...]).astype(BF16)
    g = jax.nn.sigmoid(jnp.dot(hn, wg_ref[...], preferred_element_type=F32) + bg_ref[...])
    pp = jnp.dot(p_ref[...].astype(BF16), wp_ref[...], preferred_element_type=F32)
    o_ref[...] = _rms(h + g * pp, gf_ref[...])


def _ple(h, ft, p, gp, wg, bg, wp, gf):
    S = h.shape[0]
    tm = PLE_TM
    row = lambda n: pl.BlockSpec((tm, n), lambda i: (i, 0))
    vec = _resident((1, D_MODEL))
    return pl.pallas_call(
        _ple_body,
        grid=(S // tm,),
        in_specs=[row(D_MODEL), pl.BlockSpec((D_MODEL, tm), lambda i: (0, i)), row(PLE_DIM), vec,
                  _resident(wg.shape), vec, _resident(wp.shape), vec],
        out_specs=row(D_MODEL),
        out_shape=jax.ShapeDtypeStruct((S, D_MODEL), F32),
        compiler_params=_params(("parallel",)),
    )(h, ft, p, gp, wg, bg, wp, gf)


def _pair_blocks(w):
    w = w.reshape(8, 2, 64, 64)
    z = jnp.zeros((8, 64, 64), w.dtype)
    top = jnp.concatenate([w[:, 0], z], axis=-1)
    bot = jnp.concatenate([z, w[:, 1]], axis=-1)
    return jnp.concatenate([top, bot], axis=-2).astype(BF16)


def kernel(x, p, positions, mix_norm, w_in, conv_w, conv_b, rg_wa, rg_ba, rg_wx, rg_bx, rg_lambda,
           attn_out_norm, rnn_out_norm, w_out, ffn_norm, peer_wq, peer_subkeys, peer_u, peer_v,
           ple_norm, ple_w_gate, ple_b_gate, ple_proj, final_norm):
    B, S, D = x.shape
    assert (B, S, D) == (1, SEQ, D_MODEL) and S % SPAN == 0 and w_in.shape[0] == 1
    vec = lambda a: a.reshape(1, -1).astype(F32)

    half = ROPE_DIM // 2
    inv_freq = ROPE_THETA ** (-jnp.arange(half, dtype=F32) * 2.0 / ROPE_DIM)
    invf = jnp.zeros((1, HEAD_DIM), F32).at[0, :ROPE_DIM].set(jnp.tile(inv_freq, 2))

    q, k, v, xr, gate = _inproj(x[0], positions.reshape(S, 1), invf, vec(mix_norm[0]),
                                w_in[0].astype(BF16))
    attn = _attention(q, k, v)
    rnn = _rnn(xr, gate, conv_w[0], vec(conv_b[0]), _pair_blocks(rg_wa[0]), vec(rg_ba[0]),
               _pair_blocks(rg_wx[0]), vec(rg_bx[0]), vec(rg_lambda[0]), vec(rnn_out_norm[0]))
    sk = peer_subkeys[0].reshape(2 * PEER_HEADS, PEER_NKEYS, -1).astype(BF16)
    h1, hnt, st = _outproj(attn, rnn, x[0], vec(attn_out_norm[0]), w_out[0].astype(BF16),
                           vec(ffn_norm[0]), peer_wq[0].astype(BF16), sk)
    r2, e2, jb, e1n = _select(st)
    vt = peer_v[0].astype(BF16).reshape(PEER_EXPERTS // PEER_EC, PEER_EC, D).transpose(0, 2, 1)
    ft = _peer(hnt, peer_u[0].astype(BF16), vt, r2, e2, jb, e1n)
    out = _ple(h1, ft, p[0, 0], vec(ple_norm[0]), ple_w_gate[0].astype(BF16), vec(ple_b_gate[0]),
               ple_proj[0].astype(BF16), vec(final_norm))
    return out.reshape(B, S, D)
```

```python
import functools

import jax
import jax.numpy as jnp
from jax import lax
from jax.experimental import pallas as pl
from jax.experimental.pallas import tpu as pltpu

F32 = jnp.float32
BF16 = jnp.bfloat16

D_MODEL = 2048
SEQ = 8192
HEAD_DIM = 128
ATTN_WIDTH = 1024
N_HEADS = 8
ROPE_DIM = 32
ROPE_THETA = 500000.0
ATTN_BLOCK = 128
DILATIONS = (1, 4, 16)
SPAN = ATTN_BLOCK * 16
ATTN_UNROLL = 16
RNN_WIDTH = 1024
CONV_WIDTH = 4
RGLRU_C = 8.0
PEER_HEADS = 8
PEER_NKEYS = 128
PEER_EXPERTS = PEER_NKEYS * PEER_NKEYS
PEER_TOPK = 16
PLE_DIM = 256
EPS = 1e-6

VMEM_LIMIT = 56 * 1024 * 1024

IN_TM = 512
RNN_TM = 256
OUT_TM = 256
SEL_TB = 256
PEER_TM = 512
PEER_EC = 512
PEER_CHUNKS = 2
PLE_TM = 512


def _params(sem):
    return pltpu.CompilerParams(dimension_semantics=sem, vmem_limit_bytes=VMEM_LIMIT)


def _rms(x, g):
    return x * lax.rsqrt(jnp.mean(x * x, axis=-1, keepdims=True) + EPS) * g


def _resident(shape):
    nd = len(shape)
    return pl.BlockSpec(shape, lambda *_: (0,) * nd, pipeline_mode=pl.Buffered(1))


def _inproj_body(x_ref, pos_ref, invf_ref, g_ref, w_ref, q_ref, k_ref, v_ref, xr_ref, gate_ref):
    tm = x_ref.shape[0]
    xb = _rms(x_ref[...], g_ref[...]).astype(BF16)
    ang = pos_ref[...].astype(F32) * invf_ref[...]
    cosf = jnp.cos(ang)
    sinf = jnp.sin(ang)
    lane = lax.broadcasted_iota(jnp.int32, (tm, HEAD_DIM), 1)
    first_half = lane < ROPE_DIM // 2

    def rope(c):
        partner = jnp.where(first_half,
                            -pltpu.roll(c, HEAD_DIM - ROPE_DIM // 2, 1),
                            pltpu.roll(c, ROPE_DIM // 2, 1))
        return c * cosf + partner * sinf

    outs = (q_ref, k_ref, v_ref, xr_ref, gate_ref)
    for n, o_ref in enumerate(outs):
        y = jnp.dot(xb, w_ref[:, n * 1024:(n + 1) * 1024], preferred_element_type=F32)
        if n < 2:
            scale = HEAD_DIM ** -0.5 if n == 0 else 1.0
            for h in range(N_HEADS):
                sl = slice(h * HEAD_DIM, (h + 1) * HEAD_DIM)
                o_ref[:, sl] = rope(y[:, sl]) * scale
        else:
            o_ref[...] = y


def _inproj(x, pos, invf, g, w):
    S = x.shape[0]
    tm = IN_TM
    row = lambda n: pl.BlockSpec((tm, n), lambda i: (i, 0))
    out = jax.ShapeDtypeStruct((S, 1024), F32)
    return pl.pallas_call(
        _inproj_body,
        grid=(S // tm,),
        in_specs=[row(D_MODEL), row(1), _resident((1, HEAD_DIM)), _resident((1, D_MODEL)),
                  _resident(w.shape)],
        out_specs=[row(1024)] * 5,
        out_shape=[out] * 5,
        compiler_params=_params(("parallel",)),
    )(x, pos, invf, g, w)


def _attn_body(q_ref, k_ref, v_ref, o_ref, t4, q4, q16, k1, k4, k16, v1, v4, v16, op, ls):
    c = pl.program_id(1)
    kv = {1: (k1, v1), 4: (k4, v4), 16: (k16, v16)}

    @pl.when(c == 0)
    def _():
        for d in DILATIONS:
            for ref in kv[d]:
                ref[:, 0:ATTN_BLOCK, :] = jnp.zeros((d, ATTN_BLOCK, HEAD_DIM), BF16)

    @pl.when(c > 0)
    def _():
        for d in DILATIONS:
            rows = SPAN // d
            for ref in kv[d]:
                ref[:, 0:ATTN_BLOCK, :] = ref[:, rows:rows + ATTN_BLOCK, :]

    def deinterleave(x_ref, x1, x4, x16, off):
        if x1 is not None:
            x1[0, off:off + SPAN, :] = x_ref[...].astype(BF16)
        for r4 in range(4):
            t = x_ref[pl.ds(r4, SPAN // 4, stride=4), :]
            t4[r4] = t
            x4[r4, off:off + SPAN // 4, :] = t.astype(BF16)
        for r4 in range(4):
            for r2 in range(4):
                t = t4[r4, pl.ds(r2, SPAN // 16, stride=4), :]
                x16[4 * r2 + r4, off:off + SPAN // 16, :] = t.astype(BF16)

    deinterleave(q_ref, None, q4, q16, 0)
    deinterleave(k_ref, k1, k4, k16, ATTN_BLOCK)
    deinterleave(v_ref, v1, v4, v16, ATTN_BLOCK)

    qi = lax.broadcasted_iota(jnp.int32, (ATTN_BLOCK, 2 * ATTN_BLOCK), 0)
    kj = lax.broadcasted_iota(jnp.int32, (ATTN_BLOCK, 2 * ATTN_BLOCK), 1)
    dist = ATTN_BLOCK + qi - kj
    band = (dist >= 0) & (dist <= ATTN_BLOCK)

    def tile(qt, kk, vv, has_prev):
        s = lax.dot_general(qt, kk, (((1,), (1,)), ((), ())), preferred_element_type=F32)
        first_key = jnp.where(has_prev, 0, ATTN_BLOCK)
        mask = band & (kj >= first_key)
        s = jnp.where(mask, s, -jnp.inf)
        m = jnp.max(s, axis=-1, keepdims=True)
        e = jnp.exp(s - m)
        l = jnp.sum(e, axis=-1, keepdims=True)
        o = jnp.dot(e.astype(BF16), vv, preferred_element_type=F32)
        o = o / l
        lse = jnp.broadcast_to(m + jnp.log(l), (ATTN_BLOCK, HEAD_DIM))
        return o, lse

    prev_span = c > 0

    def run_pattern(p_idx, d, q_tile):
        k_s, v_s = kv[d]
        nblk = SPAN // d // ATTN_BLOCK

        def body(idx, carry):
            r = idx // nblk
            n = idx % nblk
            row0 = pl.multiple_of(n * ATTN_BLOCK, ATTN_BLOCK)
            kk = k_s[r, pl.ds(row0, 2 * ATTN_BLOCK), :]
            vv = v_s[r, pl.ds(row0, 2 * ATTN_BLOCK), :]
            o, lse = tile(q_tile(r, row0), kk, vv, (n > 0) | prev_span)
            start = n * (ATTN_BLOCK * d) + r
            if d == 1:
                dst = pl.ds(pl.multiple_of(start, ATTN_BLOCK), ATTN_BLOCK)
            else:
                dst = pl.ds(start, ATTN_BLOCK, stride=d)
            op[p_idx, dst, :] = o
            ls[p_idx, dst, :] = lse
            return carry

        lax.fori_loop(0, d * nblk, body, 0, unroll=ATTN_UNROLL)

    run_pattern(0, 1, lambda r, row0: q_ref[pl.ds(row0, ATTN_BLOCK), :].astype(BF16))
    run_pattern(1, 4, lambda r, row0: q4[r, pl.ds(row0, ATTN_BLOCK), :])
    run_pattern(2, 16, lambda r, row0: q16[r, pl.ds(row0, ATTN_BLOCK), :])

    l0, l1, l2 = ls[0], ls[1], ls[2]
    top = jnp.maximum(jnp.maximum(l0, l1), l2)
    w0 = jnp.exp(l0 - top)
    w1 = jnp.exp(l1 - top)
    w2 = jnp.exp(l2 - top)
    o_ref[...] = (w0 * op[0] + w1 * op[1] + w2 * op[2]) / (w0 + w1 + w2)


def _attention(q, k, v):
    S = q.shape[0]
    spec = pl.BlockSpec((SPAN, HEAD_DIM), lambda h, c: (c, h))
    slab = lambda d, off, dt: pltpu.VMEM((d, off + SPAN // d, HEAD_DIM), dt)
    scratch = [
        slab(4, 0, F32), slab(4, 0, BF16), slab(16, 0, BF16),
        slab(1, ATTN_BLOCK, BF16), slab(4, ATTN_BLOCK, BF16), slab(16, ATTN_BLOCK, BF16),
        slab(1, ATTN_BLOCK, BF16), slab(4, ATTN_BLOCK, BF16), slab(16, ATTN_BLOCK, BF16),
        pltpu.VMEM((3, SPAN, HEAD_DIM), F32), pltpu.VMEM((3, SPAN, HEAD_DIM), F32),
    ]
    return pl.pallas_call(
        _attn_body,
        grid=(N_HEADS, S // SPAN),
        in_specs=[spec] * 3,
        out_specs=spec,
        out_shape=jax.ShapeDtypeStruct((S, ATTN_WIDTH), F32),
        scratch_shapes=scratch,
        compiler_params=_params(("arbitrary", "arbitrary")),
    )(q, k, v)


def _rnn_body(xr_ref, gate_ref, cw_ref, cb_ref, wa_ref, ba_ref, wx_ref, bx_ref, lam_ref, g_ref,
              o_ref, xext, hcar, a_s, u_s):
    tm = xr_ref.shape[0]
    i = pl.program_id(0)

    @pl.when(i == 0)
    def _():
        xext[0:8, :] = jnp.zeros((8, RNN_WIDTH), F32)
        hcar[...] = jnp.zeros((8, RNN_WIDTH), F32)

    x = xr_ref[...]
    xext[8:8 + tm, :] = x
    xc = cb_ref[...]
    for tap in range(CONV_WIDTH):
        sh = CONV_WIDTH - 1 - tap
        xc = xc + xext[8 - sh:8 - sh + tm, :] * cw_ref[tap:tap + 1, :]
    xext[0:8, :] = x[tm - 8:tm, :]

    xcb = xc.astype(BF16)
    for j in range(RNN_WIDTH // 128):
        sl = slice(j * 128, (j + 1) * 128)
        blk = xcb[:, sl]
        r = jax.nn.sigmoid(jnp.dot(blk, wa_ref[j], preferred_element_type=F32) + ba_ref[:, sl])
        ig = jax.nn.sigmoid(jnp.dot(blk, wx_ref[j], preferred_element_type=F32) + bx_ref[:, sl])
        log_a = -RGLRU_C * r * jax.nn.softplus(-lam_ref[:, sl])
        a = jnp.exp(log_a)
        a_s[:, sl] = a
        u_s[:, sl] = jnp.sqrt(-jnp.tanh(log_a) * (a * a + 1.0)) * ig * xc[:, sl]

    row = lax.broadcasted_iota(jnp.int32, (8, RNN_WIDTH), 0)

    def scan(t, h):
        base = pl.multiple_of(t * 8, 8)
        A = a_s[pl.ds(base, 8), :]
        B = u_s[pl.ds(base, 8), :]
        for sft in (1, 2, 4):
            valid = row >= sft
            A_sh = pltpu.roll(A, sft, 0)
            B_sh = pltpu.roll(B, sft, 0)
            B = jnp.where(valid, A * B_sh + B, B)
            A = jnp.where(valid, A * A_sh, A)
        hs = A * h + B
        u_s[pl.ds(base, 8), :] = hs
        return jnp.broadcast_to(hs[7:8, :], (8, RNN_WIDTH))

    hcar[...] = lax.fori_loop(0, tm // 8, scan, hcar[...])

    rnn = u_s[...] * jax.nn.gelu(gate_ref[...])
    o_ref[...] = _rms(rnn, g_ref[...]).astype(BF16)


def _rnn(xr, gate, cw, cb, wa, ba, wx, bx, lam, g):
    S = xr.shape[0]
    tm = RNN_TM
    row = pl.BlockSpec((tm, RNN_WIDTH), lambda i: (i, 0))
    vec = _resident((1, RNN_WIDTH))
    return pl.pallas_call(
        _rnn_body,
        grid=(S // tm,),
        in_specs=[row, row, _resident(cw.shape), vec, _resident(wa.shape), vec,
                  _resident(wx.shape), vec, vec, vec],
        out_specs=row,
        out_shape=jax.ShapeDtypeStruct((S, RNN_WIDTH), BF16),
        scratch_shapes=[pltpu.VMEM((tm + 8, RNN_WIDTH), F32), pltpu.VMEM((8, RNN_WIDTH), F32),
                        pltpu.VMEM((tm, RNN_WIDTH), F32), pltpu.VMEM((tm, RNN_WIDTH), F32)],
        compiler_params=_params(("arbitrary",)),
    )(xr, gate, cw, cb, wa, ba, wx, bx, lam, g)


def _outproj_body(attn_ref, rnn_ref, x_ref, ga_ref, wo_ref, gf_ref, wq_ref, sk_ref,
                  h1_ref, hnt_ref, st_ref):
    an = _rms(attn_ref[...], ga_ref[...]).astype(BF16)
    mix = jnp.dot(an, wo_ref[0:ATTN_WIDTH, :], preferred_element_type=F32)
    mix = mix + jnp.dot(rnn_ref[...], wo_ref[ATTN_WIDTH:, :], preferred_element_type=F32)
    h1 = x_ref[...] + mix
    h1_ref[...] = h1
    hn32 = _rms(h1, gf_ref[...])
    hn = hn32.astype(BF16)
    hnt_ref[...] = hn32.T.astype(BF16)
    qp = jnp.dot(hn, wq_ref[...], preferred_element_type=F32).astype(BF16)
    for hc in range(2 * PEER_HEADS):
        st_ref[hc] = lax.dot_general(sk_ref[hc], qp[:, hc * 128:(hc + 1) * 128],
                                     (((1,), (1,)), ((), ())), preferred_element_type=F32)


def _outproj(attn, rnn, x, ga, wo, gf, wq, sk):
    S = x.shape[0]
    tm = OUT_TM
    row = lambda n: pl.BlockSpec((tm, n), lambda i: (i, 0))
    return pl.pallas_call(
        _outproj_body,
        grid=(S // tm,),
        in_specs=[row(ATTN_WIDTH), row(RNN_WIDTH), row(D_MODEL), _resident((1, ATTN_WIDTH)),
                  _resident(wo.shape), _resident((1, D_MODEL)), _resident(wq.shape),
                  _resident(sk.shape)],
        out_specs=[row(D_MODEL), pl.BlockSpec((D_MODEL, tm), lambda i: (0, i)),
                   pl.BlockSpec((2 * PEER_HEADS, PEER_NKEYS, tm), lambda i: (0, 0, i))],
        out_shape=[jax.ShapeDtypeStruct((S, D_MODEL), F32), jax.ShapeDtypeStruct((D_MODEL, S), BF16),
                   jax.ShapeDtypeStruct((2 * PEER_HEADS, PEER_NKEYS, S), F32)],
        compiler_params=_params(("parallel",)),
    )(attn, rnn, x, ga, wo, gf, wq, sk)


def _select_body(st_ref, r2_ref, e2_ref, jb_ref, e1n_ref, a_s, b_s):
    tb = st_ref.shape[-1]
    K = PEER_TOPK
    ninf = -jnp.inf
    row8 = lax.broadcasted_iota(jnp.int32, (8, tb), 0)

    def top_values(s, out_s):
        def rnd(k, s):
            mx = jnp.max(s, axis=0, keepdims=True)
            out_s[pl.ds(k, 1), :] = mx
            return jnp.where(s == mx, ninf, s)
        lax.fori_loop(0, K, rnd, s)

    def head(h, carry):
        s1 = st_ref[2 * h]
        s2 = st_ref[2 * h + 1]
        top_values(s1, a_s)
        top_values(s2, b_s)
        A = a_s[...]
        B = b_s[...]
        B8 = B[0:8, :]
        limits = (16, 8, 5, 4, 3, 2, 2, 2)
        cands = [A[0:1, :] + B, A[1:2, :] + B8]
        for i in range(2, 8):
            cands.append(jnp.where(row8 < limits[i], A[i:i + 1, :] + B8, ninf))
        cands.append(A[8:16, :] + B[0:1, :])
        cand = jnp.concatenate(cands, axis=0)

        def rnd(k, st):
            cnd, _ = st
            mx = jnp.max(cnd, axis=0, keepdims=True)
            return jnp.where(cnd == mx, ninf, cnd), mx
        _, tau = lax.fori_loop(0, K, rnd, (cand, jnp.zeros((1, tb), F32)))

        cmax = A[0:1, :] + B[0:1, :]
        z = jnp.sum(jnp.where(cand >= tau, jnp.exp(cand - cmax), 0.0), axis=0, keepdims=True)

        j_rows = [jnp.sum(jnp.where(cands[i] >= tau, 1.0, 0.0), axis=0, keepdims=True)
                  for i in range(8)]
        tail = jnp.where(cands[8] >= tau, 1.0, 0.0)
        for i in range(8):
            j_rows.append(tail[i:i + 1, :])

        jb = jnp.zeros((PEER_NKEYS, tb), F32)
        r2 = jnp.full((PEER_NKEYS, tb), float(K), F32)
        for i in range(K):
            jb = jnp.where(s1 == A[i:i + 1, :], j_rows[i], jb)
            r2 = jnp.where(s2 == B[i:i + 1, :], float(i), r2)
        jb_ref[h] = jb
        r2_ref[h] = r2.astype(BF16)
        e1n_ref[h] = jnp.exp(s1 - A[0:1, :]) / z
        e2_ref[h] = jnp.exp(s2 - B[0:1, :]).astype(BF16)
        return carry

    lax.fori_loop(0, PEER_HEADS, head, 0)


def _select(st):
    S = st.shape[-1]
    tb = SEL_TB
    out = lambda dt: jax.ShapeDtypeStruct((PEER_HEADS, PEER_NKEYS, S), dt)
    ospec = pl.BlockSpec((PEER_HEADS, PEER_NKEYS, tb), lambda i: (0, 0, i))
    return pl.pallas_call(
        _select_body,
        grid=(S // tb,),
        in_specs=[pl.BlockSpec((2 * PEER_HEADS, PEER_NKEYS, tb), lambda i: (0, 0, i))],
        out_specs=[ospec] * 4,
        out_shape=[out(BF16), out(BF16), out(F32), out(F32)],
        scratch_shapes=[pltpu.VMEM((PEER_TOPK, tb), F32), pltpu.VMEM((PEER_TOPK, tb), F32)],
        compiler_params=_params(("parallel",)),
    )(st)


def _peer_body(hnt_ref, u_ref, vt_ref, r2_ref, e2_ref, jb_ref, e1n_ref, o_ref, w_even, w_odd):
    j = pl.program_id(1)
    ec = PEER_EC
    n_i1 = ec // PEER_NKEYS

    zero = jnp.zeros((), BF16)

    def build_gates(step, w_s, c, kk):
        i1 = jnp.minimum((step * PEER_CHUNKS + c) * n_i1 + kk, PEER_NKEYS - 1)
        w = None
        for h in range(PEER_HEADS):
            jb_b = jb_ref[h, pl.ds(i1, 1), :].astype(BF16)
            e1_b = e1n_ref[h, pl.ds(i1, 1), :].astype(BF16)
            t = jnp.where(r2_ref[h] < jb_b, e2_ref[h], zero) * e1_b
            w = t if w is None else w + t
        w_s[c, kk * PEER_NKEYS:(kk + 1) * PEER_NKEYS, :] = w

    @pl.when(j == 0)
    def _():
        o_ref[...] = jnp.zeros(o_ref.shape, F32)
        for c in range(PEER_CHUNKS):
            for kk in range(n_i1):
                build_gates(0, w_even, c, kk)

    def step(w_cur, w_next):
        acts = [jnp.dot(u_ref[c * ec:(c + 1) * ec, :], hnt_ref[...], preferred_element_type=F32)
                for c in range(PEER_CHUNKS)]
        rows = D_MODEL // n_i1
        p = w_cur[0] * jax.nn.gelu(acts[0]).astype(BF16)
        for c in range(PEER_CHUNKS):
            p_next = []
            for kk in range(n_i1):
                rsl = slice(kk * rows, (kk + 1) * rows)
                o_ref[rsl, :] += jnp.dot(vt_ref[c, rsl, :], p, preferred_element_type=F32)
                if c + 1 < PEER_CHUNKS:
                    esl = slice(kk * PEER_NKEYS, (kk + 1) * PEER_NKEYS)
                    p_next.append(w_cur[c + 1, esl, :] * jax.nn.gelu(acts[c + 1][esl, :]).astype(BF16))
                build_gates(j + 1, w_next, c, kk)
            if p_next:
                p = jnp.concatenate(p_next, axis=0)

    @pl.when(j % 2 == 0)
    def _():
        step(w_even, w_odd)

    @pl.when(j % 2 == 1)
    def _():
        step(w_odd, w_even)


def _peer(hnt, u, vt, r2, e2, jb, e1n):
    S = hnt.shape[1]
    tm, ec, nc = PEER_TM, PEER_EC, PEER_CHUNKS
    tokt = pl.BlockSpec((D_MODEL, tm), lambda i, j: (0, i))
    sel = pl.BlockSpec((PEER_HEADS, PEER_NKEYS, tm), lambda i, j: (0, 0, i))
    return pl.pallas_call(
        _peer_body,
        grid=(S // tm, PEER_EXPERTS // (ec * nc)),
        in_specs=[tokt,
                  pl.BlockSpec((ec * nc, D_MODEL), lambda i, j: (j, 0)),
                  pl.BlockSpec((nc, D_MODEL, ec), lambda i, j: (j, 0, 0)),
                  sel, sel, sel, sel],
        out_specs=tokt,
        out_shape=jax.ShapeDtypeStruct((D_MODEL, S), F32),
        scratch_shapes=[pltpu.VMEM((nc, ec, tm), BF16), pltpu.VMEM((nc, ec, tm), BF16)],
        compiler_params=_params(("parallel", "arbitrary")),
    )(hnt, u, vt, r2, e2, jb, e1n)


def _ple_body(h_ref, ft_ref, p_ref, gp_ref, wg_ref, bg_ref, wp_ref, gf_ref, o_ref):
    h = h_ref[...] + ft_ref[...].T
    hn = _rms(h, gp_ref[...]).astype(BF16)
    g = jax.nn.sigmoid(jnp.dot(hn, wg_ref[...], preferred_element_type=F32) + bg_ref[...])
    pp = jnp.dot(p_ref[...].astype(BF16), wp_ref[...], preferred_element_type=F32)
    o_ref[...] = _rms(h + g * pp, gf_ref[...])


def _ple(h, ft, p, gp, wg, bg, wp, gf):
    S = h.shape[0]
    tm = PLE_TM
    row = lambda n: pl.BlockSpec((tm, n), lambda i: (i, 0))
    vec = _resident((1, D_MODEL))
    return pl.pallas_call(
        _ple_body,
        grid=(S // tm,),
        in_specs=[row(D_MODEL), pl.BlockSpec((D_MODEL, tm), lambda i: (0, i)), row(PLE_DIM), vec,
                  _resident(wg.shape), vec, _resident(wp.shape), vec],
        out_specs=row(D_MODEL),
        out_shape=jax.ShapeDtypeStruct((S, D_MODEL), F32),
        compiler_params=_params(("parallel",)),
    )(h, ft, p, gp, wg, bg, wp, gf)


def _pair_blocks(w):
    w = w.reshape(8, 2, 64, 64)
    z = jnp.zeros((8, 64, 64), w.dtype)
    top = jnp.concatenate([w[:, 0], z], axis=-1)
    bot = jnp.concatenate([z, w[:, 1]], axis=-1)
    return jnp.concatenate([top, bot], axis=-2).astype(BF16)


def kernel(x, p, positions, mix_norm, w_in, conv_w, conv_b, rg_wa, rg_ba, rg_wx, rg_bx, rg_lambda,
           attn_out_norm, rnn_out_norm, w_out, ffn_norm, peer_wq, peer_subkeys, peer_u, peer_v,
           ple_norm, ple_w_gate, ple_b_gate, ple_proj, final_norm):
    B, S, D = x.shape
    assert (B, S, D) == (1, SEQ, D_MODEL) and S % SPAN == 0 and w_in.shape[0] == 1
    vec = lambda a: a.reshape(1, -1).astype(F32)

    half = ROPE_DIM // 2
    inv_freq = ROPE_THETA ** (-jnp.arange(half, dtype=F32) * 2.0 / ROPE_DIM)
    invf = jnp.zeros((1, HEAD_DIM), F32).at[0, :ROPE_DIM].set(jnp.tile(inv_freq, 2))

    q, k, v, xr, gate = _inproj(x[0], positions.reshape(S, 1), invf, vec(mix_norm[0]),
                                w_in[0].astype(BF16))
    attn = _attention(q, k, v)
    rnn = _rnn(xr, gate, conv_w[0], vec(conv_b[0]), _pair_blocks(rg_wa[0]), vec(rg_ba[0]),
               _pair_blocks(rg_wx[0]), vec(rg_bx[0]), vec(rg_lambda[0]), vec(rnn_out_norm[0]))
    sk = peer_subkeys[0].reshape(2 * PEER_HEADS, PEER_NKEYS, -1).astype(BF16)
    h1, hnt, st = _outproj(attn, rnn, x[0], vec(attn_out_norm[0]), w_out[0].astype(BF16),
                           vec(ffn_norm[0]), peer_wq[0].astype(BF16), sk)
    r2, e2, jb, e1n = _select(st)
    vt = peer_v[0].astype(BF16).reshape(PEER_EXPERTS // PEER_EC, PEER_EC, D).transpose(0, 2, 1)
    ft = _peer(hnt, peer_u[0].astype(BF16), vt, r2, e2, jb, e1n)
    out = _ple(h1, ft, p[0, 0], vec(ple_norm[0]), ple_w_gate[0].astype(BF16), vec(ple_b_gate[0]),
               ple_proj[0].astype(BF16), vec(final_norm))
    return out.reshape(B, S, D)
```

```python
import functools

import jax
import jax.numpy as jnp
from jax import lax
from jax.experimental import pallas as pl
from jax.experimental.pallas import tpu as pltpu

F32 = jnp.float32
BF16 = jnp.bfloat16

D_MODEL = 2048
SEQ = 8192
HEAD_DIM = 128
ATTN_WIDTH = 1024
N_HEADS = 8
ROPE_DIM = 32
ROPE_THETA = 500000.0
ATTN_BLOCK = 128
DILATIONS = (1, 4, 16)
SPAN = ATTN_BLOCK * 16
ATTN_UNROLL = 16
RNN_WIDTH = 1024
CONV_WIDTH = 4
RGLRU_C = 8.0
PEER_HEADS = 8
PEER_NKEYS = 128
PEER_EXPERTS = PEER_NKEYS * PEER_NKEYS
PEER_TOPK = 16
PLE_DIM = 256
EPS = 1e-6

VMEM_LIMIT = 56 * 1024 * 1024

IN_TM = 512
RNN_TM = 256
OUT_TM = 256
SEL_TB = 256
PEER_TM = 512
PEER_EC = 512
PEER_CHUNKS = 2
PEER_GATES_EARLY = 0
PLE_TM = 512


def _params(sem):
    return pltpu.CompilerParams(dimension_semantics=sem, vmem_limit_bytes=VMEM_LIMIT)


def _rms(x, g):
    return x * lax.rsqrt(jnp.mean(x * x, axis=-1, keepdims=True) + EPS) * g


def _resident(shape):
    nd = len(shape)
    return pl.BlockSpec(shape, lambda *_: (0,) * nd, pipeline_mode=pl.Buffered(1))


def _inproj_body(x_ref, pos_ref, invf_ref, g_ref, w_ref, q_ref, k_ref, v_ref, xr_ref, gate_ref):
    tm = x_ref.shape[0]
    xb = _rms(x_ref[...], g_ref[...]).astype(BF16)
    ang = pos_ref[...].astype(F32) * invf_ref[...]
    cosf = jnp.cos(ang)
    sinf = jnp.sin(ang)
    lane = lax.broadcasted_iota(jnp.int32, (tm, HEAD_DIM), 1)
    first_half = lane < ROPE_DIM // 2

    def rope(c):
        partner = jnp.where(first_half,
                            -pltpu.roll(c, HEAD_DIM - ROPE_DIM // 2, 1),
                            pltpu.roll(c, ROPE_DIM // 2, 1))
        return c * cosf + partner * sinf

    outs = (q_ref, k_ref, v_ref, xr_ref, gate_ref)
    for n, o_ref in enumerate(outs):
        y = jnp.dot(xb, w_ref[:, n * 1024:(n + 1) * 1024], preferred_element_type=F32)
        if n < 2:
            scale = HEAD_DIM ** -0.5 if n == 0 else 1.0
            for h in range(N_HEADS):
                sl = slice(h * HEAD_DIM, (h + 1) * HEAD_DIM)
                o_ref[:, sl] = rope(y[:, sl]) * scale
        else:
            o_ref[...] = y


def _inproj(x, pos, invf, g, w):
    S = x.shape[0]
    tm = IN_TM
    row = lambda n: pl.BlockSpec((tm, n), lambda i: (i, 0))
    out = jax.ShapeDtypeStruct((S, 1024), F32)
    return pl.pallas_call(
        _inproj_body,
        grid=(S // tm,),
        in_specs=[row(D_MODEL), row(1), _resident((1, HEAD_DIM)), _resident((1, D_MODEL)),
                  _resident(w.shape)],
        out_specs=[row(1024)] * 5,
        out_shape=[out] * 5,
        compiler_params=_params(("parallel",)),
    )(x, pos, invf, g, w)


def _attn_body(q_ref, k_ref, v_ref, o_ref, t4, q4, q16, k1, k4, k16, v1, v4, v16, op, ls):
    c = pl.program_id(1)
    kv = {1: (k1, v1), 4: (k4, v4), 16: (k16, v16)}

    @pl.when(c == 0)
    def _():
        for d in DILATIONS:
            for ref in kv[d]:
                ref[:, 0:ATTN_BLOCK, :] = jnp.zeros((d, ATTN_BLOCK, HEAD_DIM), BF16)

    @pl.when(c > 0)
    def _():
        for d in DILATIONS:
            rows = SPAN // d
            for ref in kv[d]:
                ref[:, 0:ATTN_BLOCK, :] = ref[:, rows:rows + ATTN_BLOCK, :]

    def deinterleave(x_ref, x1, x4, x16, off):
        if x1 is not None:
            x1[0, off:off + SPAN, :] = x_ref[...].astype(BF16)
        for r4 in range(4):
            t = x_ref[pl.ds(r4, SPAN // 4, stride=4), :]
            t4[r4] = t
            x4[r4, off:off + SPAN // 4, :] = t.astype(BF16)
        for r4 in range(4):
            for r2 in range(4):
                t = t4[r4, pl.ds(r2, SPAN // 16, stride=4), :]
                x16[4 * r2 + r4, off:off + SPAN // 16, :] = t.astype(BF16)

    deinterleave(q_ref, None, q4, q16, 0)
    deinterleave(k_ref, k1, k4, k16, ATTN_BLOCK)
    deinterleave(v_ref, v1, v4, v16, ATTN_BLOCK)

    qi = lax.broadcasted_iota(jnp.int32, (ATTN_BLOCK, 2 * ATTN_BLOCK), 0)
    kj = lax.broadcasted_iota(jnp.int32, (ATTN_BLOCK, 2 * ATTN_BLOCK), 1)
    dist = ATTN_BLOCK + qi - kj
    band = (dist >= 0) & (dist <= ATTN_BLOCK)

    def tile(qt, kk, vv, has_prev):
        s = lax.dot_general(qt, kk, (((1,), (1,)), ((), ())), preferred_element_type=F32)
        first_key = jnp.where(has_prev, 0, ATTN_BLOCK)
        mask = band & (kj >= first_key)
        s = jnp.where(mask, s, -jnp.inf)
        m = jnp.max(s, axis=-1, keepdims=True)
        e = jnp.exp(s - m)
        l = jnp.sum(e, axis=-1, keepdims=True)
        o = jnp.dot(e.astype(BF16), vv, preferred_element_type=F32)
        o = o / l
        lse = jnp.broadcast_to(m + jnp.log(l), (ATTN_BLOCK, HEAD_DIM))
        return o, lse

    prev_span = c > 0

    def run_pattern(p_idx, d, q_tile):
        k_s, v_s = kv[d]
        nblk = SPAN // d // ATTN_BLOCK

        def body(idx, carry):
            r = idx // nblk
            n = idx % nblk
            row0 = pl.multiple_of(n * ATTN_BLOCK, ATTN_BLOCK)
            kk = k_s[r, pl.ds(row0, 2 * ATTN_BLOCK), :]
            vv = v_s[r, pl.ds(row0, 2 * ATTN_BLOCK), :]
            o, lse = tile(q_tile(r, row0), kk, vv, (n > 0) | prev_span)
            start = n * (ATTN_BLOCK * d) + r
            if d == 1:
                dst = pl.ds(pl.multiple_of(start, ATTN_BLOCK), ATTN_BLOCK)
            else:
                dst = pl.ds(start, ATTN_BLOCK, stride=d)
            op[p_idx, dst, :] = o
            ls[p_idx, dst, :] = lse
            return carry

        lax.fori_loop(0, d * nblk, body, 0, unroll=ATTN_UNROLL)

    run_pattern(0, 1, lambda r, row0: q_ref[pl.ds(row0, ATTN_BLOCK), :].astype(BF16))
    run_pattern(1, 4, lambda r, row0: q4[r, pl.ds(row0, ATTN_BLOCK), :])
    run_pattern(2, 16, lambda r, row0: q16[r, pl.ds(row0, ATTN_BLOCK), :])

    l0, l1, l2 = ls[0], ls[1], ls[2]
    top = jnp.maximum(jnp.maximum(l0, l1), l2)
    w0 = jnp.exp(l0 - top)
    w1 = jnp.exp(l1 - top)
    w2 = jnp.exp(l2 - top)
    o_ref[...] = (w0 * op[0] + w1 * op[1] + w2 * op[2]) / (w0 + w1 + w2)


def _attention(q, k, v):
    S = q.shape[0]
    spec = pl.BlockSpec((SPAN, HEAD_DIM), lambda h, c: (c, h))
    slab = lambda d, off, dt: pltpu.VMEM((d, off + SPAN // d, HEAD_DIM), dt)
    scratch = [
        slab(4, 0, F32), slab(4, 0, BF16), slab(16, 0, BF16),
        slab(1, ATTN_BLOCK, BF16), slab(4, ATTN_BLOCK, BF16), slab(16, ATTN_BLOCK, BF16),
        slab(1, ATTN_BLOCK, BF16), slab(4, ATTN_BLOCK, BF16), slab(16, ATTN_BLOCK, BF16),
        pltpu.VMEM((3, SPAN, HEAD_DIM), F32), pltpu.VMEM((3, SPAN, HEAD_DIM), F32),
    ]
    return pl.pallas_call(
        _attn_body,
        grid=(N_HEADS, S // SPAN),
        in_specs=[spec] * 3,
        out_specs=spec,
        out_shape=jax.ShapeDtypeStruct((S, ATTN_WIDTH), F32),
        scratch_shapes=scratch,
        compiler_params=_params(("arbitrary", "arbitrary")),
    )(q, k, v)


def _rnn_body(xr_ref, gate_ref, cw_ref, cb_ref, wa_ref, ba_ref, wx_ref, bx_ref, lam_ref, g_ref,
              o_ref, xext, hcar, a_s, u_s):
    tm = xr_ref.shape[0]
    i = pl.program_id(0)

    @pl.when(i == 0)
    def _():
        xext[0:8, :] = jnp.zeros((8, RNN_WIDTH), F32)
        hcar[...] = jnp.zeros((8, RNN_WIDTH), F32)

    x = xr_ref[...]
    xext[8:8 + tm, :] = x
    xc = cb_ref[...]
    for tap in range(CONV_WIDTH):
        sh = CONV_WIDTH - 1 - tap
        xc = xc + xext[8 - sh:8 - sh + tm, :] * cw_ref[tap:tap + 1, :]
    xext[0:8, :] = x[tm - 8:tm, :]

    xcb = xc.astype(BF16)
    for j in range(RNN_WIDTH // 128):
        sl = slice(j * 128, (j + 1) * 128)
        blk = xcb[:, sl]
        r = jax.nn.sigmoid(jnp.dot(blk, wa_ref[j], preferred_element_type=F32) + ba_ref[:, sl])
        ig = jax.nn.sigmoid(jnp.dot(blk, wx_ref[j], preferred_element_type=F32) + bx_ref[:, sl])
        log_a = -RGLRU_C * r * jax.nn.softplus(-lam_ref[:, sl])
        a = jnp.exp(log_a)
        a_s[:, sl] = a
        u_s[:, sl] = jnp.sqrt(-jnp.tanh(log_a) * (a * a + 1.0)) * ig * xc[:, sl]

    row = lax.broadcasted_iota(jnp.int32, (8, RNN_WIDTH), 0)

    def scan(t, h):
        base = pl.multiple_of(t * 8, 8)
        A = a_s[pl.ds(base, 8), :]
        B = u_s[pl.ds(base, 8), :]
        for sft in (1, 2, 4):
            valid = row >= sft
            A_sh = pltpu.roll(A, sft, 0)
            B_sh = pltpu.roll(B, sft, 0)
            B = jnp.where(valid, A * B_sh + B, B)
            A = jnp.where(valid, A * A_sh, A)
        hs = A * h + B
        u_s[pl.ds(base, 8), :] = hs
        return jnp.broadcast_to(hs[7:8, :], (8, RNN_WIDTH))

    hcar[...] = lax.fori_loop(0, tm // 8, scan, hcar[...])

    rnn = u_s[...] * jax.nn.gelu(gate_ref[...])
    o_ref[...] = _rms(rnn, g_ref[...]).astype(BF16)


def _rnn(xr, gate, cw, cb, wa, ba, wx, bx, lam, g):
    S = xr.shape[0]
    tm = RNN_TM
    row = pl.BlockSpec((tm, RNN_WIDTH), lambda i: (i, 0))
    vec = _resident((1, RNN_WIDTH))
    return pl.pallas_call(
        _rnn_body,
        grid=(S // tm,),
        in_specs=[row, row, _resident(cw.shape), vec, _resident(wa.shape), vec,
                  _resident(wx.shape), vec, vec, vec],
        out_specs=row,
        out_shape=jax.ShapeDtypeStruct((S, RNN_WIDTH), BF16),
        scratch_shapes=[pltpu.VMEM((tm + 8, RNN_WIDTH), F32), pltpu.VMEM((8, RNN_WIDTH), F32),
                        pltpu.VMEM((tm, RNN_WIDTH), F32), pltpu.VMEM((tm, RNN_WIDTH), F32)],
        compiler_params=_params(("arbitrary",)),
    )(xr, gate, cw, cb, wa, ba, wx, bx, lam, g)


def _outproj_body(attn_ref, rnn_ref, x_ref, ga_ref, wo_ref, gf_ref, wq_ref, sk_ref,
                  h1_ref, hnt_ref, st_ref):
    an = _rms(attn_ref[...], ga_ref[...]).astype(BF16)
    mix = jnp.dot(an, wo_ref[0:ATTN_WIDTH, :], preferred_element_type=F32)
    mix = mix + jnp.dot(rnn_ref[...], wo_ref[ATTN_WIDTH:, :], preferred_element_type=F32)
    h1 = x_ref[...] + mix
    h1_ref[...] = h1
    hn32 = _rms(h1, gf_ref[...])
    hn = hn32.astype(BF16)
    hnt_ref[...] = hn32.T.astype(BF16)
    qp = jnp.dot(hn, wq_ref[...], preferred_element_type=F32).astype(BF16)
    for hc in range(2 * PEER_HEADS):
        st_ref[hc] = lax.dot_general(sk_ref[hc], qp[:, hc * 128:(hc + 1) * 128],
                                     (((1,), (1,)), ((), ())), preferred_element_type=F32)


def _outproj(attn, rnn, x, ga, wo, gf, wq, sk):
    S = x.shape[0]
    tm = OUT_TM
    row = lambda n: pl.BlockSpec((tm, n), lambda i: (i, 0))
    return pl.pallas_call(
        _outproj_body,
        grid=(S // tm,),
        in_specs=[row(ATTN_WIDTH), row(RNN_WIDTH), row(D_MODEL), _resident((1, ATTN_WIDTH)),
                  _resident(wo.shape), _resident((1, D_MODEL)), _resident(wq.shape),
                  _resident(sk.shape)],
        out_specs=[row(D_MODEL), pl.BlockSpec((D_MODEL, tm), lambda i: (0, i)),
                   pl.BlockSpec((2 * PEER_HEADS, PEER_NKEYS, tm), lambda i: (0, 0, i))],
        out_shape=[jax.ShapeDtypeStruct((S, D_MODEL), F32), jax.ShapeDtypeStruct((D_MODEL, S), BF16),
                   jax.ShapeDtypeStruct((2 * PEER_HEADS, PEER_NKEYS, S), F32)],
        compiler_params=_params(("parallel",)),
    )(attn, rnn, x, ga, wo, gf, wq, sk)


def _select_body(st_ref, r2_ref, e2_ref, jb_ref, e1n_ref, work, rank, tops, cnd_s, tau_s):
    tb = st_ref.shape[-1]
    K = PEER_TOPK
    NA = 2 * PEER_HEADS
    ninf = -jnp.inf
    row8 = lax.broadcasted_iota(jnp.int32, (8, tb), 0)

    work[...] = st_ref[...]
    for h in range(PEER_HEADS):
        rank[h] = jnp.full((PEER_NKEYS, tb), float(K), F32)

    def extract(k, carry):
        kf = jnp.asarray(k, dtype=F32)
        for a in range(NA):
            s = work[a]
            mx = jnp.max(s, axis=0, keepdims=True)
            tops[k, a:a + 1, :] = mx
            hit = s == mx
            work[a] = jnp.where(hit, ninf, s)
            if a % 2 == 1:
                rank[a // 2] = jnp.where(hit, kf, rank[a // 2])
        return carry

    lax.fori_loop(0, K, extract, 0)

    def candidates(h):
        A = tops[:, 2 * h, :]
        B = tops[:, 2 * h + 1, :]
        B8 = B[0:8, :]
        limits = (16, 8, 5, 4, 3, 2, 2, 2)
        cands = [A[0:1, :] + B, A[1:2, :] + B8]
        for i in range(2, 8):
            cands.append(jnp.where(row8 < limits[i], A[i:i + 1, :] + B8, ninf))
        cands.append(A[8:16, :] + B[0:1, :])
        return A, B, cands

    for h in range(PEER_HEADS):
        cnd_s[h] = jnp.concatenate(candidates(h)[2], axis=0)

    def kth(k, carry):
        for h in range(PEER_HEADS):
            c = cnd_s[h]
            mx = jnp.max(c, axis=0, keepdims=True)
            tau_s[h:h + 1, :] = mx
            cnd_s[h] = jnp.where(c == mx, ninf, c)
        return carry

    lax.fori_loop(0, K, kth, 0)

    for h in range(PEER_HEADS):
        A, B, cands = candidates(h)
        tau = tau_s[h:h + 1, :]
        cmax = A[0:1, :] + B[0:1, :]
        z = None
        for c in cands:
            zc = jnp.sum(jnp.where(c >= tau, jnp.exp(c - cmax), 0.0), axis=0, keepdims=True)
            z = zc if z is None else z + zc

        j_rows = [jnp.sum(jnp.where(cands[i] >= tau, 1.0, 0.0), axis=0, keepdims=True)
                  for i in range(8)]
        tail = jnp.where(cands[8] >= tau, 1.0, 0.0)
        for i in range(8):
            j_rows.append(tail[i:i + 1, :])

        s1 = st_ref[2 * h]
        jb = jnp.zeros((PEER_NKEYS, tb), F32)
        for i in range(K):
            jb = jnp.where(s1 == A[i:i + 1, :], j_rows[i], jb)
        jb_ref[h] = jb
        r2_ref[h] = rank[h].astype(BF16)
        e1n_ref[h] = jnp.exp(s1 - A[0:1, :]) / z
        e2_ref[h] = jnp.exp(st_ref[2 * h + 1] - B[0:1, :]).astype(BF16)


def _select(st):
    S = st.shape[-1]
    tb = SEL_TB
    out = lambda dt: jax.ShapeDtypeStruct((PEER_HEADS, PEER_NKEYS, S), dt)
    ospec = pl.BlockSpec((PEER_HEADS, PEER_NKEYS, tb), lambda i: (0, 0, i))
    return pl.pallas_call(
        _select_body,
        grid=(S // tb,),
        in_specs=[pl.BlockSpec((2 * PEER_HEADS, PEER_NKEYS, tb), lambda i: (0, 0, i))],
        out_specs=[ospec] * 4,
        out_shape=[out(BF16), out(BF16), out(F32), out(F32)],
        scratch_shapes=[pltpu.VMEM((2 * PEER_HEADS, PEER_NKEYS, tb), F32),
                        pltpu.VMEM((PEER_HEADS, PEER_NKEYS, tb), F32),
                        pltpu.VMEM((PEER_TOPK, 2 * PEER_HEADS, tb), F32),
                        pltpu.VMEM((PEER_HEADS, 80, tb), F32),
                        pltpu.VMEM((PEER_HEADS, tb), F32)],
        compiler_params=_params(("parallel",)),
    )(st)


def _peer_body(hnt_ref, u_ref, vt_ref, r2_ref, e2_ref, jb_ref, e1n_ref, o_ref, w_even, w_odd):
    j = pl.program_id(1)
    ec = PEER_EC
    n_i1 = ec // PEER_NKEYS

    zero = jnp.zeros((), BF16)

    def build_gates(step, w_s, c, kk):
        i1 = jnp.minimum((step * PEER_CHUNKS + c) * n_i1 + kk, PEER_NKEYS - 1)
        w = None
        for h in range(PEER_HEADS):
            jb_b = jb_ref[h, pl.ds(i1, 1), :].astype(BF16)
            e1_b = e1n_ref[h, pl.ds(i1, 1), :].astype(BF16)
            t = jnp.where(r2_ref[h] < jb_b, e2_ref[h], zero) * e1_b
            w = t if w is None else w + t
        w_s[c, kk * PEER_NKEYS:(kk + 1) * PEER_NKEYS, :] = w

    @pl.when(j == 0)
    def _():
        o_ref[...] = jnp.zeros(o_ref.shape, F32)
        for c in range(PEER_CHUNKS):
            for kk in range(n_i1):
                build_gates(0, w_even, c, kk)

    def step(w_cur, w_next):
        pieces = [(c, kk) for c in range(PEER_CHUNKS) for kk in range(n_i1)]

        def gates(n):
            for _ in range(n):
                if pieces:
                    build_gates(j + 1, w_next, *pieces.pop(0))

        acts = []
        for c in range(PEER_CHUNKS):
            acts.append(jnp.dot(u_ref[c * ec:(c + 1) * ec, :], hnt_ref[...],
                                preferred_element_type=F32))
            gates(PEER_GATES_EARLY)
        rows = D_MODEL // n_i1
        p = w_cur[0] * jax.nn.gelu(acts[0]).astype(BF16)
        for c in range(PEER_CHUNKS):
            p_next = []
            for kk in range(n_i1):
                rsl = slice(kk * rows, (kk + 1) * rows)
                o_ref[rsl, :] += jnp.dot(vt_ref[c, rsl, :], p, preferred_element_type=F32)
                if c + 1 < PEER_CHUNKS:
                    esl = slice(kk * PEER_NKEYS, (kk + 1) * PEER_NKEYS)
                    p_next.append(w_cur[c + 1, esl, :] * jax.nn.gelu(acts[c + 1][esl, :]).astype(BF16))
                gates(1)
            if p_next:
                p = jnp.concatenate(p_next, axis=0)
        gates(len(pieces))

    @pl.when(j % 2 == 0)
    def _():
        step(w_even, w_odd)

    @pl.when(j % 2 == 1)
    def _():
        step(w_odd, w_even)


def _peer(hnt, u, vt, r2, e2, jb, e1n):
    S = hnt.shape[1]
    tm, ec, nc = PEER_TM, PEER_EC, PEER_CHUNKS
    tokt = pl.BlockSpec((D_MODEL, tm), lambda i, j: (0, i))
    sel = pl.BlockSpec((PEER_HEADS, PEER_NKEYS, tm), lambda i, j: (0, 0, i))
    return pl.pallas_call(
        _peer_body,
        grid=(S // tm, PEER_EXPERTS // (ec * nc)),
        in_specs=[tokt,
                  pl.BlockSpec((ec * nc, D_MODEL), lambda i, j: (j, 0)),
                  pl.BlockSpec((nc, D_MODEL, ec), lambda i, j: (j, 0, 0)),
                  sel, sel, sel, sel],
        out_specs=tokt,
        out_shape=jax.ShapeDtypeStruct((D_MODEL, S), F32),
        scratch_shapes=[pltpu.VMEM((nc, ec, tm), BF16), pltpu.VMEM((nc, ec, tm), BF16)],
        compiler_params=_params(("parallel", "arbitrary")),
    )(hnt, u, vt, r2, e2, jb, e1n)


def _ple_body(h_ref, ft_ref, p_ref, gp_ref, wg_ref, bg_ref, wp_ref, gf_ref, o_ref):
    h = h_ref[...] + ft_ref[...].T
    hn = _rms(h, gp_ref[...]).astype(BF16)
    g = jax.nn.sigmoid(jnp.dot(hn, wg_ref[...], preferred_element_type=F32) + bg_ref[...])
    pp = jnp.dot(p_ref[...].astype(BF16), wp_ref[...], preferred_element_type=F32)
    o_ref[...] = _rms(h + g * pp, gf_ref[...])


def _ple(h, ft, p, gp, wg, bg, wp, gf):
    S = h.shape[0]
    tm = PLE_TM
    row = lambda n: pl.BlockSpec((tm, n), lambda i: (i, 0))
    vec = _resident((1, D_MODEL))
    return pl.pallas_call(
        _ple_body,
        grid=(S // tm,),
        in_specs=[row(D_MODEL), pl.BlockSpec((D_MODEL, tm), lambda i: (0, i)), row(PLE_DIM), vec,
                  _resident(wg.shape), vec, _resident(wp.shape), vec],
        out_specs=row(D_MODEL),
        out_shape=jax.ShapeDtypeStruct((S, D_MODEL), F32),
        compiler_params=_params(("parallel",)),
    )(h, ft, p, gp, wg, bg, wp, gf)


def _pair_blocks(w):
    w = w.reshape(8, 2, 64, 64)
    z = jnp.zeros((8, 64, 64), w.dtype)
    top = jnp.concatenate([w[:, 0], z], axis=-1)
    bot = jnp.concatenate([z, w[:, 1]], axis=-1)
    return jnp.concatenate([top, bot], axis=-2).astype(BF16)


def kernel(x, p, positions, mix_norm, w_in, conv_w, conv_b, rg_wa, rg_ba, rg_wx, rg_bx, rg_lambda,
           attn_out_norm, rnn_out_norm, w_out, ffn_norm, peer_wq, peer_subkeys, peer_u, peer_v,
           ple_norm, ple_w_gate, ple_b_gate, ple_proj, final_norm):
    B, S, D = x.shape
    assert (B, S, D) == (1, SEQ, D_MODEL) and S % SPAN == 0 and w_in.shape[0] == 1
    vec = lambda a: a.reshape(1, -1).astype(F32)

    half = ROPE_DIM // 2
    inv_freq = ROPE_THETA ** (-jnp.arange(half, dtype=F32) * 2.0 / ROPE_DIM)
    invf = jnp.zeros((1, HEAD_DIM), F32).at[0, :ROPE_DIM].set(jnp.tile(inv_freq, 2))

    q, k, v, xr, gate = _inproj(x[0], positions.reshape(S, 1), invf, vec(mix_norm[0]),
                                w_in[0].astype(BF16))
    attn = _attention(q, k, v)
    rnn = _rnn(xr, gate, conv_w[0], vec(conv_b[0]), _pair_blocks(rg_wa[0]), vec(rg_ba[0]),
               _pair_blocks(rg_wx[0]), vec(rg_bx[0]), vec(rg_lambda[0]), vec(rnn_out_norm[0]))
    sk = peer_subkeys[0].reshape(2 * PEER_HEADS, PEER_NKEYS, -1).astype(BF16)
    h1, hnt, st = _outproj(attn, rnn, x[0], vec(attn_out_norm[0]), w_out[0].astype(BF16),
                           vec(ffn_norm[0]), peer_wq[0].astype(BF16), sk)
    r2, e2, jb, e1n = _select(st)
    vt = peer_v[0].astype(BF16).reshape(PEER_EXPERTS // PEER_EC, PEER_EC, D).transpose(0, 2, 1)
    ft = _peer(hnt, peer_u[0].astype(BF16), vt, r2, e2, jb, e1n)
    out = _ple(h1, ft, p[0, 0], vec(ple_norm[0]), ple_w_gate[0].astype(BF16), vec(ple_b_gate[0]),
               ple_proj[0].astype(BF16), vec(final_norm))
    return out.reshape(B, S, D)
```

```python
import functools

import jax
import jax.numpy as jnp
from jax import lax
from jax.experimental import pallas as pl
from jax.experimental.pallas import tpu as pltpu

F32 = jnp.float32
BF16 = jnp.bfloat16

D_MODEL = 2048
SEQ = 8192
HEAD_DIM = 128
ATTN_WIDTH = 1024
N_HEADS = 8
ROPE_DIM = 32
ROPE_THETA = 500000.0
ATTN_BLOCK = 128
DILATIONS = (1, 4, 16)
SPAN = ATTN_BLOCK * 16
RNN_WIDTH = 1024
CONV_WIDTH = 4
RGLRU_C = 8.0
PEER_HEADS = 8
PEER_NKEYS = 128
PEER_EXPERTS = PEER_NKEYS * PEER_NKEYS
PEER_TOPK = 16
PLE_DIM = 256
EPS = 1e-6

VMEM_LIMIT = 56 * 1024 * 1024

IN_TM = 512
RNN_TM = 256
OUT_TM = 256
SEL_TB = 256
PEER_TM = 512
PEER_EC = 512
PEER_CHUNKS = 2
PEER_GATES_EARLY = 0
PEER_ROW_BLOCKS = 4
PLE_TM = 512


def _params(sem):
    return pltpu.CompilerParams(dimension_semantics=sem, vmem_limit_bytes=VMEM_LIMIT)


def _rms(x, g):
    return x * lax.rsqrt(jnp.mean(x * x, axis=-1, keepdims=True) + EPS) * g


def _resident(shape):
    nd = len(shape)
    return pl.BlockSpec(shape, lambda *_: (0,) * nd, pipeline_mode=pl.Buffered(1))


def _cast_body(x_ref, o_ref):
    o_ref[...] = x_ref[...].astype(BF16)


def _to_bf16(w):
    rows, cols = w.shape
    tr = 256
    spec = pl.BlockSpec((tr, cols), lambda i: (i, 0))
    return pl.pallas_call(
        _cast_body,
        grid=(rows // tr,),
        in_specs=[spec],
        out_specs=spec,
        out_shape=jax.ShapeDtypeStruct((rows, cols), BF16),
        compiler_params=_params(("parallel",)),
    )(w)


def _inproj_body(x_ref, pos_ref, invf_ref, g_ref, w_ref, q_ref, k_ref, v_ref, xr_ref, gate_ref):
    tm = x_ref.shape[0]
    xb = _rms(x_ref[...], g_ref[...]).astype(BF16)
    ang = pos_ref[...].astype(F32) * invf_ref[...]
    cosf = jnp.cos(ang)
    sinf = jnp.sin(ang)
    lane = lax.broadcasted_iota(jnp.int32, (tm, HEAD_DIM), 1)
    first_half = lane < ROPE_DIM // 2

    def rope(c):
        partner = jnp.where(first_half,
                            -pltpu.roll(c, HEAD_DIM - ROPE_DIM // 2, 1),
                            pltpu.roll(c, ROPE_DIM // 2, 1))
        return c * cosf + partner * sinf

    outs = (q_ref, k_ref, v_ref, xr_ref, gate_ref)
    for n, o_ref in enumerate(outs):
        y = jnp.dot(xb, w_ref[:, n * 1024:(n + 1) * 1024], preferred_element_type=F32)
        if n < 2:
            scale = HEAD_DIM ** -0.5 if n == 0 else 1.0
            for h in range(N_HEADS):
                sl = slice(h * HEAD_DIM, (h + 1) * HEAD_DIM)
                o_ref[:, sl] = rope(y[:, sl]) * scale
        else:
            o_ref[...] = y


def _inproj(x, pos, invf, g, w):
    S = x.shape[0]
    tm = IN_TM
    row = lambda n: pl.BlockSpec((tm, n), lambda i: (i, 0))
    out = jax.ShapeDtypeStruct((S, 1024), F32)
    return pl.pallas_call(
        _inproj_body,
        grid=(S // tm,),
        in_specs=[row(D_MODEL), row(1), _resident((1, HEAD_DIM)), _resident((1, D_MODEL)),
                  _resident(w.shape)],
        out_specs=[row(1024)] * 5,
        out_shape=[out] * 5,
        compiler_params=_params(("parallel",)),
    )(x, pos, invf, g, w)


def _attn_body(q_ref, k_ref, v_ref, o_ref, t4, q4, q16, k1, k4, k16, v1, v4, v16, op, ls):
    c = pl.program_id(1)
    kv = {1: (k1, v1), 4: (k4, v4), 16: (k16, v16)}

    @pl.when(c == 0)
    def _():
        for d in DILATIONS:
            for ref in kv[d]:
                ref[:, 0:ATTN_BLOCK, :] = jnp.zeros((d, ATTN_BLOCK, HEAD_DIM), BF16)

    @pl.when(c > 0)
    def _():
        for d in DILATIONS:
            rows = SPAN // d
            for ref in kv[d]:
                ref[:, 0:ATTN_BLOCK, :] = ref[:, rows:rows + ATTN_BLOCK, :]

    def deinterleave(x_ref, x1, x4, x16, off):
        if x1 is not None:
            x1[0, off:off + SPAN, :] = x_ref[...].astype(BF16)
        for r4 in range(4):
            t = x_ref[pl.ds(r4, SPAN // 4, stride=4), :]
            t4[r4] = t
            x4[r4, off:off + SPAN // 4, :] = t.astype(BF16)
        for r4 in range(4):
            for r2 in range(4):
                t = t4[r4, pl.ds(r2, SPAN // 16, stride=4), :]
                x16[4 * r2 + r4, off:off + SPAN // 16, :] = t.astype(BF16)

    deinterleave(q_ref, None, q4, q16, 0)
    deinterleave(k_ref, k1, k4, k16, ATTN_BLOCK)
    deinterleave(v_ref, v1, v4, v16, ATTN_BLOCK)

    qi = lax.broadcasted_iota(jnp.int32, (ATTN_BLOCK, 2 * ATTN_BLOCK), 0)
    kj = lax.broadcasted_iota(jnp.int32, (ATTN_BLOCK, 2 * ATTN_BLOCK), 1)
    dist = ATTN_BLOCK + qi - kj
    band = (dist >= 0) & (dist <= ATTN_BLOCK)
    bias_band = jnp.where(band, 0.0, -jnp.inf)
    first_key = jnp.where(c > 0, 0, ATTN_BLOCK)
    bias_first = jnp.where(band & (kj >= first_key), 0.0, -jnp.inf)

    def tile(qt, kk, vv, first_block):
        s = lax.dot_general(qt, kk, (((1,), (1,)), ((), ())), preferred_element_type=F32)
        s = s + (bias_first if first_block else bias_band)
        m = jnp.max(s, axis=-1, keepdims=True)
        e = jnp.exp(s - m)
        l = jnp.sum(e, axis=-1, keepdims=True)
        o = jnp.dot(e.astype(BF16), vv, preferred_element_type=F32)
        o = o / l
        lse = jnp.broadcast_to(m + jnp.log(l), (ATTN_BLOCK, HEAD_DIM))
        return o, lse

    def run_pattern(p_idx, d, q_tile):
        k_s, v_s = kv[d]
        nblk = SPAN // d // ATTN_BLOCK
        for r in range(d):
            for n in range(nblk):
                row0 = n * ATTN_BLOCK
                kk = k_s[r, row0:row0 + 2 * ATTN_BLOCK, :]
                vv = v_s[r, row0:row0 + 2 * ATTN_BLOCK, :]
                o, lse = tile(q_tile(r, row0), kk, vv, n == 0)
                start = n * (ATTN_BLOCK * d) + r
                dst = pl.ds(start, ATTN_BLOCK) if d == 1 else pl.ds(start, ATTN_BLOCK, stride=d)
                op[p_idx, dst, :] = o
                ls[p_idx, dst, :] = lse

    run_pattern(0, 1, lambda r, row0: q_ref[row0:row0 + ATTN_BLOCK, :].astype(BF16))
    run_pattern(1, 4, lambda r, row0: q4[r, row0:row0 + ATTN_BLOCK, :])
    run_pattern(2, 16, lambda r, row0: q16[r, row0:row0 + ATTN_BLOCK, :])

    l0, l1, l2 = ls[0], ls[1], ls[2]
    top = jnp.maximum(jnp.maximum(l0, l1), l2)
    w0 = jnp.exp(l0 - top)
    w1 = jnp.exp(l1 - top)
    w2 = jnp.exp(l2 - top)
    o_ref[...] = (w0 * op[0] + w1 * op[1] + w2 * op[2]) / (w0 + w1 + w2)


def _attention(q, k, v):
    S = q.shape[0]
    spec = pl.BlockSpec((SPAN, HEAD_DIM), lambda h, c: (c, h))
    slab = lambda d, off, dt: pltpu.VMEM((d, off + SPAN // d, HEAD_DIM), dt)
    scratch = [
        slab(4, 0, F32), slab(4, 0, BF16), slab(16, 0, BF16),
        slab(1, ATTN_BLOCK, BF16), slab(4, ATTN_BLOCK, BF16), slab(16, ATTN_BLOCK, BF16),
        slab(1, ATTN_BLOCK, BF16), slab(4, ATTN_BLOCK, BF16), slab(16, ATTN_BLOCK, BF16),
        pltpu.VMEM((3, SPAN, HEAD_DIM), F32), pltpu.VMEM((3, SPAN, HEAD_DIM), F32),
    ]
    return pl.pallas_call(
        _attn_body,
        grid=(N_HEADS, S // SPAN),
        in_specs=[spec] * 3,
        out_specs=spec,
        out_shape=jax.ShapeDtypeStruct((S, ATTN_WIDTH), F32),
        scratch_shapes=scratch,
        compiler_params=_params(("arbitrary", "arbitrary")),
    )(q, k, v)


def _rnn_body(xr_ref, gate_ref, cw_ref, cb_ref, wa_ref, ba_ref, wx_ref, bx_ref, lam_ref, g_ref,
              o_ref, xext, hcar, a_s, u_s):
    tm = xr_ref.shape[0]
    i = pl.program_id(0)

    @pl.when(i == 0)
    def _():
        xext[0:8, :] = jnp.zeros((8, RNN_WIDTH), F32)
        hcar[...] = jnp.zeros((8, RNN_WIDTH), F32)

    x = xr_ref[...]
    xext[8:8 + tm, :] = x
    xc = cb_ref[...]
    for tap in range(CONV_WIDTH):
        sh = CONV_WIDTH - 1 - tap
        xc = xc + xext[8 - sh:8 - sh + tm, :] * cw_ref[tap:tap + 1, :]
    xext[0:8, :] = x[tm - 8:tm, :]

    xcb = xc.astype(BF16)
    for j in range(RNN_WIDTH // 128):
        sl = slice(j * 128, (j + 1) * 128)
        blk = xcb[:, sl]
        r = jax.nn.sigmoid(jnp.dot(blk, wa_ref[j], preferred_element_type=F32) + ba_ref[:, sl])
        ig = jax.nn.sigmoid(jnp.dot(blk, wx_ref[j], preferred_element_type=F32) + bx_ref[:, sl])
        log_a = -RGLRU_C * r * jax.nn.softplus(-lam_ref[:, sl])
        a = jnp.exp(log_a)
        a_s[:, sl] = a
        u_s[:, sl] = jnp.sqrt(-jnp.tanh(log_a) * (a * a + 1.0)) * ig * xc[:, sl]

    row = lax.broadcasted_iota(jnp.int32, (8, RNN_WIDTH), 0)

    def scan(t, h):
        base = pl.multiple_of(t * 8, 8)
        A = a_s[pl.ds(base, 8), :]
        B = u_s[pl.ds(base, 8), :]
        for sft in (1, 2, 4):
            valid = row >= sft
            A_sh = pltpu.roll(A, sft, 0)
            B_sh = pltpu.roll(B, sft, 0)
            B = jnp.where(valid, A * B_sh + B, B)
            A = jnp.where(valid, A * A_sh, A)
        hs = A * h + B
        u_s[pl.ds(base, 8), :] = hs
        return jnp.broadcast_to(hs[7:8, :], (8, RNN_WIDTH))

    hcar[...] = lax.fori_loop(0, tm // 8, scan, hcar[...])

    rnn = u_s[...] * jax.nn.gelu(gate_ref[...])
    o_ref[...] = _rms(rnn, g_ref[...]).astype(BF16)


def _rnn(xr, gate, cw, cb, wa, ba, wx, bx, lam, g):
    S = xr.shape[0]
    tm = RNN_TM
    row = pl.BlockSpec((tm, RNN_WIDTH), lambda i: (i, 0))
    vec = _resident((1, RNN_WIDTH))
    return pl.pallas_call(
        _rnn_body,
        grid=(S // tm,),
        in_specs=[row, row, _resident(cw.shape), vec, _resident(wa.shape), vec,
                  _resident(wx.shape), vec, vec, vec],
        out_specs=row,
        out_shape=jax.ShapeDtypeStruct((S, RNN_WIDTH), BF16),
        scratch_shapes=[pltpu.VMEM((tm + 8, RNN_WIDTH), F32), pltpu.VMEM((8, RNN_WIDTH), F32),
                        pltpu.VMEM((tm, RNN_WIDTH), F32), pltpu.VMEM((tm, RNN_WIDTH), F32)],
        compiler_params=_params(("arbitrary",)),
    )(xr, gate, cw, cb, wa, ba, wx, bx, lam, g)


def _outproj_body(attn_ref, rnn_ref, x_ref, ga_ref, wo_ref, gf_ref, wq_ref, sk_ref,
                  h1_ref, hnt_ref, st_ref):
    an = _rms(attn_ref[...], ga_ref[...]).astype(BF16)
    mix = jnp.dot(an, wo_ref[0:ATTN_WIDTH, :], preferred_element_type=F32)
    mix = mix + jnp.dot(rnn_ref[...], wo_ref[ATTN_WIDTH:, :], preferred_element_type=F32)
    h1 = x_ref[...] + mix
    h1_ref[...] = h1
    hn32 = _rms(h1, gf_ref[...])
    hn = hn32.astype(BF16)
    hnt_ref[...] = hn32.T.astype(BF16)
    qp = jnp.dot(hn, wq_ref[...], preferred_element_type=F32).astype(BF16)
    for hc in range(2 * PEER_HEADS):
        st_ref[hc] = lax.dot_general(sk_ref[hc], qp[:, hc * 128:(hc + 1) * 128],
                                     (((1,), (1,)), ((), ())), preferred_element_type=F32)


def _outproj(attn, rnn, x, ga, wo, gf, wq, sk):
    S = x.shape[0]
    tm = OUT_TM
    row = lambda n: pl.BlockSpec((tm, n), lambda i: (i, 0))
    return pl.pallas_call(
        _outproj_body,
        grid=(S // tm,),
        in_specs=[row(ATTN_WIDTH), row(RNN_WIDTH), row(D_MODEL), _resident((1, ATTN_WIDTH)),
                  _resident(wo.shape), _resident((1, D_MODEL)), _resident(wq.shape),
                  _resident(sk.shape)],
        out_specs=[row(D_MODEL), pl.BlockSpec((D_MODEL, tm), lambda i: (0, i)),
                   pl.BlockSpec((2 * PEER_HEADS, PEER_NKEYS, tm), lambda i: (0, 0, i))],
        out_shape=[jax.ShapeDtypeStruct((S, D_MODEL), F32), jax.ShapeDtypeStruct((D_MODEL, S), BF16),
                   jax.ShapeDtypeStruct((2 * PEER_HEADS, PEER_NKEYS, S), F32)],
        compiler_params=_params(("parallel",)),
    )(attn, rnn, x, ga, wo, gf, wq, sk)


def _select_body(st_ref, r2_ref, e2_ref, jb_ref, e1n_ref, prev, rank, tops, cnd_s, tau_s):
    tb = st_ref.shape[-1]
    K = PEER_TOPK
    NA = 2 * PEER_HEADS
    ninf = -jnp.inf
    row8 = lax.broadcasted_iota(jnp.int32, (8, tb), 0)

    prev[...] = jnp.full((NA, tb), jnp.inf, F32)
    for h in range(PEER_HEADS):
        rank[h] = jnp.full((PEER_NKEYS, tb), -1.0, F32)

    def extract(k, carry):
        for a in range(NA):
            s = st_ref[a]
            below = s < prev[a:a + 1, :]
            mx = jnp.max(jnp.where(below, s, ninf), axis=0, keepdims=True)
            tops[k, a:a + 1, :] = mx
            prev[a:a + 1, :] = mx
            if a % 2 == 1:
                rank[a // 2] = rank[a // 2] + jnp.where(below, 1.0, 0.0)
        return carry

    lax.fori_loop(0, K, extract, 0)
    for h in range(PEER_HEADS):
        rank[h] = rank[h] + jnp.where(st_ref[2 * h + 1] < prev[2 * h + 1:2 * h + 2, :], 1.0, 0.0)

    def candidates(h):
        A = tops[:, 2 * h, :]
        B = tops[:, 2 * h + 1, :]
        B8 = B[0:8, :]
        limits = (16, 8, 5, 4, 3, 2, 2, 2)
        cands = [A[0:1, :] + B, A[1:2, :] + B8]
        for i in range(2, 8):
            cands.append(jnp.where(row8 < limits[i], A[i:i + 1, :] + B8, ninf))
        cands.append(A[8:16, :] + B[0:1, :])
        return A, B, cands

    for h in range(PEER_HEADS):
        cnd_s[h] = jnp.concatenate(candidates(h)[2], axis=0)

    def kth(k, carry):
        for h in range(PEER_HEADS):
            c = cnd_s[h]
            mx = jnp.max(c, axis=0, keepdims=True)
            tau_s[h:h + 1, :] = mx
            cnd_s[h] = jnp.where(c == mx, ninf, c)
        return carry

    lax.fori_loop(0, K, kth, 0)

    for h in range(PEER_HEADS):
        A, B, cands = candidates(h)
        tau = tau_s[h:h + 1, :]
        cmax = A[0:1, :] + B[0:1, :]
        z = None
        for c in cands:
            zc = jnp.sum(jnp.where(c >= tau, jnp.exp(c - cmax), 0.0), axis=0, keepdims=True)
            z = zc if z is None else z + zc

        j_rows = [jnp.sum(jnp.where(cands[i] >= tau, 1.0, 0.0), axis=0, keepdims=True)
                  for i in range(8)]
        tail = jnp.where(cands[8] >= tau, 1.0, 0.0)
        for i in range(8):
            j_rows.append(tail[i:i + 1, :])

        s1 = st_ref[2 * h]
        jb = jnp.zeros((PEER_NKEYS, tb), F32)
        for i in range(K):
            jb = jnp.where(s1 == A[i:i + 1, :], j_rows[i], jb)
        jb_ref[h] = jb
        r2_ref[h] = rank[h].astype(BF16)
        e1n_ref[h] = jnp.exp(s1 - A[0:1, :]) / z
        e2_ref[h] = jnp.exp(st_ref[2 * h + 1] - B[0:1, :]).astype(BF16)


def _select(st):
    S = st.shape[-1]
    tb = SEL_TB
    out = lambda dt: jax.ShapeDtypeStruct((PEER_HEADS, PEER_NKEYS, S), dt)
    ospec = pl.BlockSpec((PEER_HEADS, PEER_NKEYS, tb), lambda i: (0, 0, i))
    return pl.pallas_call(
        _select_body,
        grid=(S // tb,),
        in_specs=[pl.BlockSpec((2 * PEER_HEADS, PEER_NKEYS, tb), lambda i: (0, 0, i))],
        out_specs=[ospec] * 4,
        out_shape=[out(BF16), out(BF16), out(F32), out(F32)],
        scratch_shapes=[pltpu.VMEM((2 * PEER_HEADS, tb), F32),
                        pltpu.VMEM((PEER_HEADS, PEER_NKEYS, tb), F32),
                        pltpu.VMEM((PEER_TOPK, 2 * PEER_HEADS, tb), F32),
                        pltpu.VMEM((PEER_HEADS, 80, tb), F32),
                        pltpu.VMEM((PEER_HEADS, tb), F32)],
        compiler_params=_params(("parallel",)),
    )(st)


def _peer_body(hnt_ref, u_ref, vt_ref, r2_ref, e2_ref, jb_ref, e1n_ref, o_ref, w_even, w_odd):
    j = pl.program_id(1)
    ec = PEER_EC
    n_i1 = ec // PEER_NKEYS

    zero = jnp.zeros((), BF16)

    def build_gates(step, w_s, c, kk):
        i1 = jnp.minimum((step * PEER_CHUNKS + c) * n_i1 + kk, PEER_NKEYS - 1)
        w = None
        for h in range(PEER_HEADS):
            jb_b = jb_ref[h, pl.ds(i1, 1), :].astype(BF16)
            e1_b = e1n_ref[h, pl.ds(i1, 1), :].astype(BF16)
            t = jnp.where(r2_ref[h] < jb_b, e2_ref[h], zero) * e1_b
            w = t if w is None else w + t
        w_s[c, kk * PEER_NKEYS:(kk + 1) * PEER_NKEYS, :] = w

    @pl.when(j == 0)
    def _():
        o_ref[...] = jnp.zeros(o_ref.shape, F32)
        for c in range(PEER_CHUNKS):
            for kk in range(n_i1):
                build_gates(0, w_even, c, kk)

    def step(w_cur, w_next):
        pieces = [(c, kk) for c in range(PEER_CHUNKS) for kk in range(n_i1)]
        n_pieces = len(pieces)

        def gates(n):
            for _ in range(n):
                if pieces:
                    build_gates(j + 1, w_next, *pieces.pop(0))

        acts = []
        for c in range(PEER_CHUNKS):
            acts.append(jnp.dot(u_ref[c * ec:(c + 1) * ec, :], hnt_ref[...],
                                preferred_element_type=F32))
            gates(PEER_GATES_EARLY)
        nrb = PEER_ROW_BLOCKS
        rows = D_MODEL // nrb
        erows = ec // nrb
        p = w_cur[0] * jax.nn.gelu(acts[0]).astype(BF16)
        for c in range(PEER_CHUNKS):
            p_next = []
            for rb in range(nrb):
                rsl = slice(rb * rows, (rb + 1) * rows)
                o_ref[rsl, :] += jnp.dot(vt_ref[c, rsl, :], p, preferred_element_type=F32)
                if c + 1 < PEER_CHUNKS:
                    esl = slice(rb * erows, (rb + 1) * erows)
                    p_next.append(w_cur[c + 1, esl, :] * jax.nn.gelu(acts[c + 1][esl, :]).astype(BF16))
                done = c * nrb + rb + 1
                gates(done * n_pieces // (PEER_CHUNKS * nrb) - (n_pieces - len(pieces)))
            if p_next:
                p = jnp.concatenate(p_next, axis=0)
        gates(len(pieces))

    @pl.when(j % 2 == 0)
    def _():
        step(w_even, w_odd)

    @pl.when(j % 2 == 1)
    def _():
        step(w_odd, w_even)


def _peer(hnt, u, vt, r2, e2, jb, e1n):
    S = hnt.shape[1]
    tm, ec, nc = PEER_TM, PEER_EC, PEER_CHUNKS
    tokt = pl.BlockSpec((D_MODEL, tm), lambda i, j: (0, i))
    sel = pl.BlockSpec((PEER_HEADS, PEER_NKEYS, tm), lambda i, j: (0, 0, i))
    return pl.pallas_call(
        _peer_body,
        grid=(S // tm, PEER_EXPERTS // (ec * nc)),
        in_specs=[tokt,
                  pl.BlockSpec((ec * nc, D_MODEL), lambda i, j: (j, 0)),
                  pl.BlockSpec((nc, D_MODEL, ec), lambda i, j: (j, 0, 0)),
                  sel, sel, sel, sel],
        out_specs=tokt,
        out_shape=jax.ShapeDtypeStruct((D_MODEL, S), F32),
        scratch_shapes=[pltpu.VMEM((nc, ec, tm), BF16), pltpu.VMEM((nc, ec, tm), BF16)],
        compiler_params=_params(("parallel", "arbitrary")),
    )(hnt, u, vt, r2, e2, jb, e1n)


def _ple_body(h_ref, ft_ref, p_ref, gp_ref, wg_ref, bg_ref, wp_ref, gf_ref, o_ref):
    h = h_ref[...] + ft_ref[...].T
    hn = _rms(h, gp_ref[...]).astype(BF16)
    g = jax.nn.sigmoid(jnp.dot(hn, wg_ref[...], preferred_element_type=F32) + bg_ref[...])
    pp = jnp.dot(p_ref[...].astype(BF16), wp_ref[...], preferred_element_type=F32)
    o_ref[...] = _rms(h + g * pp, gf_ref[...])


def _ple(h, ft, p, gp, wg, bg, wp, gf):
    S = h.shape[0]
    tm = PLE_TM
    row = lambda n: pl.BlockSpec((tm, n), lambda i: (i, 0))
    vec = _resident((1, D_MODEL))
    return pl.pallas_call(
        _ple_body,
        grid=(S // tm,),
        in_specs=[row(D_MODEL), pl.BlockSpec((D_MODEL, tm), lambda i: (0, i)), row(PLE_DIM), vec,
                  _resident(wg.shape), vec, _resident(wp.shape), vec],
        out_specs=row(D_MODEL),
        out_shape=jax.ShapeDtypeStruct((S, D_MODEL), F32),
        compiler_params=_params(("parallel",)),
    )(h, ft, p, gp, wg, bg, wp, gf)


def _pair_blocks(w):
    w = w.reshape(8, 2, 64, 64)
    z = jnp.zeros((8, 64, 64), w.dtype)
    top = jnp.concatenate([w[:, 0], z], axis=-1)
    bot = jnp.concatenate([z, w[:, 1]], axis=-1)
    return jnp.concatenate([top, bot], axis=-2).astype(BF16)


def kernel(x, p, positions, mix_norm, w_in, conv_w, conv_b, rg_wa, rg_ba, rg_wx, rg_bx, rg_lambda,
           attn_out_norm, rnn_out_norm, w_out, ffn_norm, peer_wq, peer_subkeys, peer_u, peer_v,
           ple_norm, ple_w_gate, ple_b_gate, ple_proj, final_norm):
    B, S, D = x.shape
    assert (B, S, D) == (1, SEQ, D_MODEL) and S % SPAN == 0 and w_in.shape[0] == 1
    vec = lambda a: a.reshape(1, -1).astype(F32)

    half = ROPE_DIM // 2
    inv_freq = ROPE_THETA ** (-jnp.arange(half, dtype=F32) * 2.0 / ROPE_DIM)
    invf = jnp.zeros((1, HEAD_DIM), F32).at[0, :ROPE_DIM].set(jnp.tile(inv_freq, 2))

    q, k, v, xr, gate = _inproj(x[0], positions.reshape(S, 1), invf, vec(mix_norm[0]),
                                _to_bf16(w_in[0]))
    attn = _attention(q, k, v)
    rnn = _rnn(xr, gate, conv_w[0], vec(conv_b[0]), _pair_blocks(rg_wa[0]), vec(rg_ba[0]),
               _pair_blocks(rg_wx[0]), vec(rg_bx[0]), vec(rg_lambda[0]), vec(rnn_out_norm[0]))
    sk = peer_subkeys[0].reshape(2 * PEER_HEADS, PEER_NKEYS, -1).astype(BF16)
    h1, hnt, st = _outproj(attn, rnn, x[0], vec(attn_out_norm[0]), w_out[0].astype(BF16),
                           vec(ffn_norm[0]), peer_wq[0].astype(BF16), sk)
    r2, e2, jb, e1n = _select(st)
    vt = peer_v[0].astype(BF16).reshape(PEER_EXPERTS // PEER_EC, PEER_EC, D).transpose(0, 2, 1)
    ft = _peer(hnt, peer_u[0].astype(BF16), vt, r2, e2, jb, e1n)
    out = _ple(h1, ft, p[0, 0], vec(ple_norm[0]), ple_w_gate[0].astype(BF16), vec(ple_b_gate[0]),
               ple_proj[0].astype(BF16), vec(final_norm))
    return out.reshape(B, S, D)
```

```python
import functools

import jax
import jax.numpy as jnp
from jax import lax
from jax.experimental import pallas as pl
from jax.experimental.pallas import tpu as pltpu

F32 = jnp.float32
BF16 = jnp.bfloat16

D_MODEL = 2048
SEQ = 8192
HEAD_DIM = 128
ATTN_WIDTH = 1024
N_HEADS = 8
ROPE_DIM = 32
ROPE_THETA = 500000.0
ATTN_BLOCK = 128
DILATIONS = (1, 4, 16)
SPAN = ATTN_BLOCK * 16
RNN_WIDTH = 1024
CONV_WIDTH = 4
RGLRU_C = 8.0
PEER_HEADS = 8
PEER_NKEYS = 128
PEER_EXPERTS = PEER_NKEYS * PEER_NKEYS
PEER_TOPK = 16
PLE_DIM = 256
EPS = 1e-6

VMEM_LIMIT = 56 * 1024 * 1024

IN_TM = 512
RNN_TM = 256
OUT_TM = 256
SEL_TB = 256
PEER_TM = 512
PEER_EC = 512
PEER_CHUNKS = 2
PEER_GATES_EARLY = 0
PEER_ROW_BLOCKS = 4
PLE_TM = 512


def _params(sem):
    return pltpu.CompilerParams(dimension_semantics=sem, vmem_limit_bytes=VMEM_LIMIT)


def _rms(x, g):
    return x * lax.rsqrt(jnp.mean(x * x, axis=-1, keepdims=True) + EPS) * g


def _resident(shape):
    nd = len(shape)
    return pl.BlockSpec(shape, lambda *_: (0,) * nd, pipeline_mode=pl.Buffered(1))


def _cast_body(x_ref, o_ref):
    o_ref[...] = x_ref[...].astype(BF16)


def _to_bf16(w):
    rows, cols = w.shape
    tr = 256
    spec = pl.BlockSpec((tr, cols), lambda i: (i, 0))
    return pl.pallas_call(
        _cast_body,
        grid=(rows // tr,),
        in_specs=[spec],
        out_specs=spec,
        out_shape=jax.ShapeDtypeStruct((rows, cols), BF16),
        compiler_params=_params(("parallel",)),
    )(w)


def _inproj_body(x_ref, pos_ref, invf_ref, g_ref, w_ref, q_ref, k_ref, v_ref, xr_ref, gate_ref):
    tm = x_ref.shape[0]
    xb = _rms(x_ref[...], g_ref[...]).astype(BF16)
    ang = pos_ref[...].astype(F32) * invf_ref[...]
    cosf = jnp.cos(ang)
    sinf = jnp.sin(ang)
    lane = lax.broadcasted_iota(jnp.int32, (tm, HEAD_DIM), 1)
    first_half = lane < ROPE_DIM // 2

    def rope(c):
        partner = jnp.where(first_half,
                            -pltpu.roll(c, HEAD_DIM - ROPE_DIM // 2, 1),
                            pltpu.roll(c, ROPE_DIM // 2, 1))
        return c * cosf + partner * sinf

    outs = (q_ref, k_ref, v_ref, xr_ref, gate_ref)
    for n, o_ref in enumerate(outs):
        y = jnp.dot(xb, w_ref[:, n * 1024:(n + 1) * 1024], preferred_element_type=F32)
        if n < 2:
            scale = HEAD_DIM ** -0.5 if n == 0 else 1.0
            for h in range(N_HEADS):
                sl = slice(h * HEAD_DIM, (h + 1) * HEAD_DIM)
                o_ref[:, sl] = rope(y[:, sl]) * scale
        else:
            o_ref[...] = y


def _inproj(x, pos, invf, g, w):
    S = x.shape[0]
    tm = IN_TM
    row = lambda n: pl.BlockSpec((tm, n), lambda i: (i, 0))
    out = jax.ShapeDtypeStruct((S, 1024), F32)
    return pl.pallas_call(
        _inproj_body,
        grid=(S // tm,),
        in_specs=[row(D_MODEL), row(1), _resident((1, HEAD_DIM)), _resident((1, D_MODEL)),
                  _resident(w.shape)],
        out_specs=[row(1024)] * 5,
        out_shape=[out] * 5,
        compiler_params=_params(("parallel",)),
    )(x, pos, invf, g, w)


def _attn_body(q_ref, k_ref, v_ref, o_ref, t4, q4, q16, k1, k4, k16, v1, v4, v16, op, ls):
    c = pl.program_id(1)
    kv = {1: (k1, v1), 4: (k4, v4), 16: (k16, v16)}

    @pl.when(c == 0)
    def _():
        for d in DILATIONS:
            for ref in kv[d]:
                ref[:, 0:ATTN_BLOCK, :] = jnp.zeros((d, ATTN_BLOCK, HEAD_DIM), BF16)

    @pl.when(c > 0)
    def _():
        for d in DILATIONS:
            rows = SPAN // d
            for ref in kv[d]:
                ref[:, 0:ATTN_BLOCK, :] = ref[:, rows:rows + ATTN_BLOCK, :]

    def deinterleave(x_ref, x1, x4, x16, off):
        if x1 is not None:
            x1[0, off:off + SPAN, :] = x_ref[...].astype(BF16)
        for r4 in range(4):
            t = x_ref[pl.ds(r4, SPAN // 4, stride=4), :]
            t4[r4] = t
            x4[r4, off:off + SPAN // 4, :] = t.astype(BF16)
        for r4 in range(4):
            for r2 in range(4):
                t = t4[r4, pl.ds(r2, SPAN // 16, stride=4), :]
                x16[4 * r2 + r4, off:off + SPAN // 16, :] = t.astype(BF16)

    deinterleave(q_ref, None, q4, q16, 0)
    deinterleave(k_ref, k1, k4, k16, ATTN_BLOCK)
    deinterleave(v_ref, v1, v4, v16, ATTN_BLOCK)

    qi = lax.broadcasted_iota(jnp.int32, (ATTN_BLOCK, 2 * ATTN_BLOCK), 0)
    kj = lax.broadcasted_iota(jnp.int32, (ATTN_BLOCK, 2 * ATTN_BLOCK), 1)
    dist = ATTN_BLOCK + qi - kj
    band = (dist >= 0) & (dist <= ATTN_BLOCK)
    bias_band = jnp.where(band, 0.0, -jnp.inf)
    first_key = jnp.where(c > 0, 0, ATTN_BLOCK)
    bias_first = jnp.where(band & (kj >= first_key), 0.0, -jnp.inf)

    def tile(qt, kk, vv, first_block):
        s = lax.dot_general(qt, kk, (((1,), (1,)), ((), ())), preferred_element_type=F32)
        s = s + (bias_first if first_block else bias_band)
        m = jnp.max(s, axis=-1, keepdims=True)
        e = jnp.exp(s - m)
        l = jnp.sum(e, axis=-1, keepdims=True)
        o = jnp.dot(e.astype(BF16), vv, preferred_element_type=F32)
        o = o / l
        lse = jnp.broadcast_to(m + jnp.log(l), (ATTN_BLOCK, HEAD_DIM))
        return o, lse

    def run_pattern(p_idx, d, q_tile):
        k_s, v_s = kv[d]
        nblk = SPAN // d // ATTN_BLOCK
        for r in range(d):
            for n in range(nblk):
                row0 = n * ATTN_BLOCK
                kk = k_s[r, row0:row0 + 2 * ATTN_BLOCK, :]
                vv = v_s[r, row0:row0 + 2 * ATTN_BLOCK, :]
                o, lse = tile(q_tile(r, row0), kk, vv, n == 0)
                start = n * (ATTN_BLOCK * d) + r
                dst = pl.ds(start, ATTN_BLOCK) if d == 1 else pl.ds(start, ATTN_BLOCK, stride=d)
                op[p_idx, dst, :] = o
                ls[p_idx, dst, :] = lse

    run_pattern(0, 1, lambda r, row0: q_ref[row0:row0 + ATTN_BLOCK, :].astype(BF16))
    run_pattern(1, 4, lambda r, row0: q4[r, row0:row0 + ATTN_BLOCK, :])
    run_pattern(2, 16, lambda r, row0: q16[r, row0:row0 + ATTN_BLOCK, :])

    l0, l1, l2 = ls[0], ls[1], ls[2]
    top = jnp.maximum(jnp.maximum(l0, l1), l2)
    w0 = jnp.exp(l0 - top)
    w1 = jnp.exp(l1 - top)
    w2 = jnp.exp(l2 - top)
    o_ref[...] = (w0 * op[0] + w1 * op[1] + w2 * op[2]) / (w0 + w1 + w2)


def _attention(q, k, v):
    S = q.shape[0]
    spec = pl.BlockSpec((SPAN, HEAD_DIM), lambda h, c: (c, h))
    slab = lambda d, off, dt: pltpu.VMEM((d, off + SPAN // d, HEAD_DIM), dt)
    scratch = [
        slab(4, 0, F32), slab(4, 0, BF16), slab(16, 0, BF16),
        slab(1, ATTN_BLOCK, BF16), slab(4, ATTN_BLOCK, BF16), slab(16, ATTN_BLOCK, BF16),
        slab(1, ATTN_BLOCK, BF16), slab(4, ATTN_BLOCK, BF16), slab(16, ATTN_BLOCK, BF16),
        pltpu.VMEM((3, SPAN, HEAD_DIM), F32), pltpu.VMEM((3, SPAN, HEAD_DIM), F32),
    ]
    return pl.pallas_call(
        _attn_body,
        grid=(N_HEADS, S // SPAN),
        in_specs=[spec] * 3,
        out_specs=spec,
        out_shape=jax.ShapeDtypeStruct((S, ATTN_WIDTH), F32),
        scratch_shapes=scratch,
        compiler_params=_params(("arbitrary", "arbitrary")),
    )(q, k, v)


def _rnn_body(xr_ref, gate_ref, cw_ref, cb_ref, wa_ref, ba_ref, wx_ref, bx_ref, lam_ref, g_ref,
              o_ref, xext, hcar, a_s, u_s):
    tm = xr_ref.shape[0]
    i = pl.program_id(0)

    @pl.when(i == 0)
    def _():
        xext[0:8, :] = jnp.zeros((8, RNN_WIDTH), F32)
        hcar[...] = jnp.zeros((8, RNN_WIDTH), F32)

    x = xr_ref[...]
    xext[8:8 + tm, :] = x
    xc = cb_ref[...]
    for tap in range(CONV_WIDTH):
        sh = CONV_WIDTH - 1 - tap
        xc = xc + xext[8 - sh:8 - sh + tm, :] * cw_ref[tap:tap + 1, :]
    xext[0:8, :] = x[tm - 8:tm, :]

    xcb = xc.astype(BF16)
    for j in range(RNN_WIDTH // 128):
        sl = slice(j * 128, (j + 1) * 128)
        blk = xcb[:, sl]
        r = jax.nn.sigmoid(jnp.dot(blk, wa_ref[j], preferred_element_type=F32) + ba_ref[:, sl])
        ig = jax.nn.sigmoid(jnp.dot(blk, wx_ref[j], preferred_element_type=F32) + bx_ref[:, sl])
        log_a = -RGLRU_C * r * jax.nn.softplus(-lam_ref[:, sl])
        a = jnp.exp(log_a)
        a_s[:, sl] = a
        u_s[:, sl] = jnp.sqrt(-jnp.tanh(log_a) * (a * a + 1.0)) * ig * xc[:, sl]

    row = lax.broadcasted_iota(jnp.int32, (8, RNN_WIDTH), 0)

    def scan(t, h):
        base = pl.multiple_of(t * 8, 8)
        A = a_s[pl.ds(base, 8), :]
        B = u_s[pl.ds(base, 8), :]
        for sft in (1, 2, 4):
            valid = row >= sft
            A_sh = pltpu.roll(A, sft, 0)
            B_sh = pltpu.roll(B, sft, 0)
            B = jnp.where(valid, A * B_sh + B, B)
            A = jnp.where(valid, A * A_sh, A)
        hs = A * h + B
        u_s[pl.ds(base, 8), :] = hs
        return jnp.broadcast_to(hs[7:8, :], (8, RNN_WIDTH))

    hcar[...] = lax.fori_loop(0, tm // 8, scan, hcar[...])

    rnn = u_s[...] * jax.nn.gelu(gate_ref[...])
    o_ref[...] = _rms(rnn, g_ref[...]).astype(BF16)


def _rnn(xr, gate, cw, cb, wa, ba, wx, bx, lam, g):
    S = xr.shape[0]
    tm = RNN_TM
    row = pl.BlockSpec((tm, RNN_WIDTH), lambda i: (i, 0))
    vec = _resident((1, RNN_WIDTH))
    return pl.pallas_call(
        _rnn_body,
        grid=(S // tm,),
        in_specs=[row, row, _resident(cw.shape), vec, _resident(wa.shape), vec,
                  _resident(wx.shape), vec, vec, vec],
        out_specs=row,
        out_shape=jax.ShapeDtypeStruct((S, RNN_WIDTH), BF16),
        scratch_shapes=[pltpu.VMEM((tm + 8, RNN_WIDTH), F32), pltpu.VMEM((8, RNN_WIDTH), F32),
                        pltpu.VMEM((tm, RNN_WIDTH), F32), pltpu.VMEM((tm, RNN_WIDTH), F32)],
        compiler_params=_params(("arbitrary",)),
    )(xr, gate, cw, cb, wa, ba, wx, bx, lam, g)


def _outproj_body(attn_ref, rnn_ref, x_ref, ga_ref, wo_ref, gf_ref, wq_ref, sk_ref,
                  h1_ref, hnt_ref, st_ref):
    an = _rms(attn_ref[...], ga_ref[...]).astype(BF16)
    mix = jnp.dot(an, wo_ref[0:ATTN_WIDTH, :], preferred_element_type=F32)
    mix = mix + jnp.dot(rnn_ref[...], wo_ref[ATTN_WIDTH:, :], preferred_element_type=F32)
    h1 = x_ref[...] + mix
    h1_ref[...] = h1
    hn32 = _rms(h1, gf_ref[...])
    hn = hn32.astype(BF16)
    hnt_ref[...] = hn32.T.astype(BF16)
    qp = jnp.dot(hn, wq_ref[...], preferred_element_type=F32).astype(BF16)
    for hc in range(2 * PEER_HEADS):
        st_ref[hc] = lax.dot_general(sk_ref[hc], qp[:, hc * 128:(hc + 1) * 128],
                                     (((1,), (1,)), ((), ())), preferred_element_type=F32)


def _outproj(attn, rnn, x, ga, wo, gf, wq, sk):
    S = x.shape[0]
    tm = OUT_TM
    row = lambda n: pl.BlockSpec((tm, n), lambda i: (i, 0))
    return pl.pallas_call(
        _outproj_body,
        grid=(S // tm,),
        in_specs=[row(ATTN_WIDTH), row(RNN_WIDTH), row(D_MODEL), _resident((1, ATTN_WIDTH)),
                  _resident(wo.shape), _resident((1, D_MODEL)), _resident(wq.shape),
                  _resident(sk.shape)],
        out_specs=[row(D_MODEL), pl.BlockSpec((D_MODEL, tm), lambda i: (0, i)),
                   pl.BlockSpec((2 * PEER_HEADS, PEER_NKEYS, tm), lambda i: (0, 0, i))],
        out_shape=[jax.ShapeDtypeStruct((S, D_MODEL), F32), jax.ShapeDtypeStruct((D_MODEL, S), BF16),
                   jax.ShapeDtypeStruct((2 * PEER_HEADS, PEER_NKEYS, S), F32)],
        compiler_params=_params(("parallel",)),
    )(attn, rnn, x, ga, wo, gf, wq, sk)


def _sort16_network():
    pairs = []

    def merge(lo, n, r):
        step = 2 * r
        if step < n:
            merge(lo, n, step)
            merge(lo + r, n, step)
            pairs.extend((i, i + r) for i in range(lo + r, lo + n - r, step))
        else:
            pairs.append((lo, lo + r))

    def sort(lo, n):
        if n > 1:
            sort(lo, n // 2)
            sort(lo + n // 2, n // 2)
            merge(lo, n, 1)

    sort(0, 16)
    return tuple(pairs)


_SORT16 = _sort16_network()


def _select_body(st_ref, r2_ref, e2_ref, jb_ref, e1n_ref, tops, cnd_s, tau_s):
    tb = st_ref.shape[-1]
    K = PEER_TOPK
    NA = 2 * PEER_HEADS
    ninf = -jnp.inf
    row8 = lax.broadcasted_iota(jnp.int32, (8, tb), 0)

    def extract(a, carry):
        v = [st_ref[a, 8 * g:8 * g + 8, :] for g in range(PEER_NKEYS // 8)]
        for i, j in _SORT16:
            v[i], v[j] = jnp.maximum(v[i], v[j]), jnp.minimum(v[i], v[j])
        for k in range(K):
            mx = jnp.max(v[0], axis=0, keepdims=True)
            tops[a, k:k + 1, :] = mx
            hit = v[0] == mx
            for g in range(K - 1 - k):
                v[g] = jnp.where(hit, v[g + 1], v[g])
        return carry

    lax.fori_loop(0, NA, extract, 0, unroll=4)

    def candidates(h):
        A = tops[2 * h]
        B = tops[2 * h + 1]
        B8 = B[0:8, :]
        limits = (16, 8, 5, 4, 3, 2, 2, 2)
        cands = [A[0:1, :] + B, A[1:2, :] + B8]
        for i in range(2, 8):
            cands.append(jnp.where(row8 < limits[i], A[i:i + 1, :] + B8, ninf))
        cands.append(A[8:16, :] + B[0:1, :])
        return A, B, cands

    for h in range(PEER_HEADS):
        cnd_s[h] = jnp.concatenate(candidates(h)[2], axis=0)

    def kth(k, carry):
        for h in range(PEER_HEADS):
            c = cnd_s[h]
            mx = jnp.max(c, axis=0, keepdims=True)
            tau_s[h:h + 1, :] = mx
            cnd_s[h] = jnp.where(c == mx, ninf, c)
        return carry

    lax.fori_loop(0, K, kth, 0)

    for h in range(PEER_HEADS):
        A, B, cands = candidates(h)
        tau = tau_s[h:h + 1, :]
        cmax = A[0:1, :] + B[0:1, :]
        z = None
        for c in cands:
            zc = jnp.sum(jnp.where(c >= tau, jnp.exp(c - cmax), 0.0), axis=0, keepdims=True)
            z = zc if z is None else z + zc

        j_rows = [jnp.sum(jnp.where(cands[i] >= tau, 1.0, 0.0), axis=0, keepdims=True)
                  for i in range(8)]
        tail = jnp.where(cands[8] >= tau, 1.0, 0.0)
        for i in range(8):
            j_rows.append(tail[i:i + 1, :])

        s1 = st_ref[2 * h]
        s2 = st_ref[2 * h + 1]
        jb = jnp.zeros((PEER_NKEYS, tb), F32)
        r2 = jnp.full((PEER_NKEYS, tb), float(K), F32)
        for i in range(K):
            jb = jnp.where(s1 == A[i:i + 1, :], j_rows[i], jb)
            r2 = jnp.where(s2 == B[i:i + 1, :], float(i), r2)
        jb_ref[h] = jb
        r2_ref[h] = r2.astype(BF16)
        e1n_ref[h] = jnp.exp(s1 - A[0:1, :]) / z
        e2_ref[h] = jnp.exp(s2 - B[0:1, :]).astype(BF16)


def _select(st):
    S = st.shape[-1]
    tb = SEL_TB
    out = lambda dt: jax.ShapeDtypeStruct((PEER_HEADS, PEER_NKEYS, S), dt)
    ospec = pl.BlockSpec((PEER_HEADS, PEER_NKEYS, tb), lambda i: (0, 0, i))
    return pl.pallas_call(
        _select_body,
        grid=(S // tb,),
        in_specs=[pl.BlockSpec((2 * PEER_HEADS, PEER_NKEYS, tb), lambda i: (0, 0, i))],
        out_specs=[ospec] * 4,
        out_shape=[out(BF16), out(BF16), out(F32), out(F32)],
        scratch_shapes=[pltpu.VMEM((2 * PEER_HEADS, PEER_TOPK, tb), F32),
                        pltpu.VMEM((PEER_HEADS, 80, tb), F32),
                        pltpu.VMEM((PEER_HEADS, tb), F32)],
        compiler_params=_params(("parallel",)),
    )(st)


def _peer_body(hnt_ref, u_ref, vt_ref, r2_ref, e2_ref, jb_ref, e1n_ref, o_ref, w_even, w_odd):
    j = pl.program_id(1)
    ec = PEER_EC
    n_i1 = ec // PEER_NKEYS

    zero = jnp.zeros((), BF16)

    def build_gates(step, w_s, c, kk):
        i1 = jnp.minimum((step * PEER_CHUNKS + c) * n_i1 + kk, PEER_NKEYS - 1)
        w = None
        for h in range(PEER_HEADS):
            jb_b = jb_ref[h, pl.ds(i1, 1), :].astype(BF16)
            e1_b = e1n_ref[h, pl.ds(i1, 1), :].astype(BF16)
            t = jnp.where(r2_ref[h] < jb_b, e2_ref[h], zero) * e1_b
            w = t if w is None else w + t
        w_s[c, kk * PEER_NKEYS:(kk + 1) * PEER_NKEYS, :] = w

    @pl.when(j == 0)
    def _():
        o_ref[...] = jnp.zeros(o_ref.shape, F32)
        for c in range(PEER_CHUNKS):
            for kk in range(n_i1):
                build_gates(0, w_even, c, kk)

    def step(w_cur, w_next):
        pieces = [(c, kk) for c in range(PEER_CHUNKS) for kk in range(n_i1)]
        n_pieces = len(pieces)

        def gates(n):
            for _ in range(n):
                if pieces:
                    build_gates(j + 1, w_next, *pieces.pop(0))

        acts = []
        for c in range(PEER_CHUNKS):
            acts.append(jnp.dot(u_ref[c * ec:(c + 1) * ec, :], hnt_ref[...],
                                preferred_element_type=F32))
            gates(PEER_GATES_EARLY)
        nrb = PEER_ROW_BLOCKS
        rows = D_MODEL // nrb
        erows = ec // nrb
        p = w_cur[0] * jax.nn.gelu(acts[0]).astype(BF16)
        for c in range(PEER_CHUNKS):
            p_next = []
            for rb in range(nrb):
                rsl = slice(rb * rows, (rb + 1) * rows)
                o_ref[rsl, :] += jnp.dot(vt_ref[c, rsl, :], p, preferred_element_type=F32)
                if c + 1 < PEER_CHUNKS:
                    esl = slice(rb * erows, (rb + 1) * erows)
                    p_next.append(w_cur[c + 1, esl, :] * jax.nn.gelu(acts[c + 1][esl, :]).astype(BF16))
                done = c * nrb + rb + 1
                gates(done * n_pieces // (PEER_CHUNKS * nrb) - (n_pieces - len(pieces)))
            if p_next:
                p = jnp.concatenate(p_next, axis=0)
        gates(len(pieces))

    @pl.when(j % 2 == 0)
    def _():
        step(w_even, w_odd)

    @pl.when(j % 2 == 1)
    def _():
        step(w_odd, w_even)


def _peer(hnt, u, vt, r2, e2, jb, e1n):
    S = hnt.shape[1]
    tm, ec, nc = PEER_TM, PEER_EC, PEER_CHUNKS
    tokt = pl.BlockSpec((D_MODEL, tm), lambda i, j: (0, i))
    sel = pl.BlockSpec((PEER_HEADS, PEER_NKEYS, tm), lambda i, j: (0, 0, i))
    return pl.pallas_call(
        _peer_body,
        grid=(S // tm, PEER_EXPERTS // (ec * nc)),
        in_specs=[tokt,
                  pl.BlockSpec((ec * nc, D_MODEL), lambda i, j: (j, 0)),
                  pl.BlockSpec((nc, D_MODEL, ec), lambda i, j: (j, 0, 0)),
                  sel, sel, sel, sel],
        out_specs=tokt,
        out_shape=jax.ShapeDtypeStruct((D_MODEL, S), F32),
        scratch_shapes=[pltpu.VMEM((nc, ec, tm), BF16), pltpu.VMEM((nc, ec, tm), BF16)],
        compiler_params=_params(("parallel", "arbitrary")),
    )(hnt, u, vt, r2, e2, jb, e1n)


def _ple_body(h_ref, ft_ref, p_ref, gp_ref, wg_ref, bg_ref, wp_ref, gf_ref, o_ref):
    h = h_ref[...] + ft_ref[...].T
    hn = _rms(h, gp_ref[...]).astype(BF16)
    g = jax.nn.sigmoid(jnp.dot(hn, wg_ref[...], preferred_element_type=F32) + bg_ref[...])
    pp = jnp.dot(p_ref[...].astype(BF16), wp_ref[...], preferred_element_type=F32)
    o_ref[...] = _rms(h + g * pp, gf_ref[...])


def _ple(h, ft, p, gp, wg, bg, wp, gf):
    S = h.shape[0]
    tm = PLE_TM
    row = lambda n: pl.BlockSpec((tm, n), lambda i: (i, 0))
    vec = _resident((1, D_MODEL))
    return pl.pallas_call(
        _ple_body,
        grid=(S // tm,),
        in_specs=[row(D_MODEL), pl.BlockSpec((D_MODEL, tm), lambda i: (0, i)), row(PLE_DIM), vec,
                  _resident(wg.shape), vec, _resident(wp.shape), vec],
        out_specs=row(D_MODEL),
        out_shape=jax.ShapeDtypeStruct((S, D_MODEL), F32),
        compiler_params=_params(("parallel",)),
    )(h, ft, p, gp, wg, bg, wp, gf)


def _pair_blocks(w):
    w = w.reshape(8, 2, 64, 64)
    z = jnp.zeros((8, 64, 64), w.dtype)
    top = jnp.concatenate([w[:, 0], z], axis=-1)
    bot = jnp.concatenate([z, w[:, 1]], axis=-1)
    return jnp.concatenate([top, bot], axis=-2).astype(BF16)


def kernel(x, p, positions, mix_norm, w_in, conv_w, conv_b, rg_wa, rg_ba, rg_wx, rg_bx, rg_lambda,
           attn_out_norm, rnn_out_norm, w_out, ffn_norm, peer_wq, peer_subkeys, peer_u, peer_v,
           ple_norm, ple_w_gate, ple_b_gate, ple_proj, final_norm):
    B, S, D = x.shape
    assert (B, S, D) == (1, SEQ, D_MODEL) and S % SPAN == 0 and w_in.shape[0] == 1
    vec = lambda a: a.reshape(1, -1).astype(F32)

    half = ROPE_DIM // 2
    inv_freq = ROPE_THETA ** (-jnp.arange(half, dtype=F32) * 2.0 / ROPE_DIM)
    invf = jnp.zeros((1, HEAD_DIM), F32).at[0, :ROPE_DIM].set(jnp.tile(inv_freq, 2))

    q, k, v, xr, gate = _inproj(x[0], positions.reshape(S, 1), invf, vec(mix_norm[0]),
                                _to_bf16(w_in[0]))
    attn = _attention(q, k, v)
    rnn = _rnn(xr, gate, conv_w[0], vec(conv_b[0]), _pair_blocks(rg_wa[0]), vec(rg_ba[0]),
               _pair_blocks(rg_wx[0]), vec(rg_bx[0]), vec(rg_lambda[0]), vec(rnn_out_norm[0]))
    sk = peer_subkeys[0].reshape(2 * PEER_HEADS, PEER_NKEYS, -1).astype(BF16)
    h1, hnt, st = _outproj(attn, rnn, x[0], vec(attn_out_norm[0]), w_out[0].astype(BF16),
                           vec(ffn_norm[0]), peer_wq[0].astype(BF16), sk)
    r2, e2, jb, e1n = _select(st)
    vt = peer_v[0].astype(BF16).reshape(PEER_EXPERTS // PEER_EC, PEER_EC, D).transpose(0, 2, 1)
    ft = _peer(hnt, peer_u[0].astype(BF16), vt, r2, e2, jb, e1n)
    out = _ple(h1, ft, p[0, 0], vec(ple_norm[0]), ple_w_gate[0].astype(BF16), vec(ple_b_gate[0]),
               ple_proj[0].astype(BF16), vec(final_norm))
    return out.reshape(B, S, D)
```

```python
import functools

import jax
import jax.numpy as jnp
from jax import lax
from jax.experimental import pallas as pl
from jax.experimental.pallas import tpu as pltpu

F32 = jnp.float32
BF16 = jnp.bfloat16

D_MODEL = 2048
SEQ = 8192
HEAD_DIM = 128
ATTN_WIDTH = 1024
N_HEADS = 8
ROPE_DIM = 32
ROPE_THETA = 500000.0
ATTN_BLOCK = 128
DILATIONS = (1, 4, 16)
SPAN = ATTN_BLOCK * 16
RNN_WIDTH = 1024
CONV_WIDTH = 4
RGLRU_C = 8.0
PEER_HEADS = 8
PEER_NKEYS = 128
PEER_EXPERTS = PEER_NKEYS * PEER_NKEYS
PEER_TOPK = 16
PLE_DIM = 256
EPS = 1e-6

VMEM_LIMIT = 56 * 1024 * 1024

IN_TM = 512
RNN_TM = 256
OUT_TM = 256
SEL_TB = 256
PEER_TM = 512
PEER_EC = 512
PEER_CHUNKS = 2
PEER_GATES_EARLY = 0
PEER_ROW_BLOCKS = 4
PLE_TM = 512


def _params(sem):
    return pltpu.CompilerParams(dimension_semantics=sem, vmem_limit_bytes=VMEM_LIMIT)


def _rms(x, g):
    return x * lax.rsqrt(jnp.mean(x * x, axis=-1, keepdims=True) + EPS) * g


def _resident(shape):
    nd = len(shape)
    return pl.BlockSpec(shape, lambda *_: (0,) * nd, pipeline_mode=pl.Buffered(1))


def _cast_body(x_ref, o_ref):
    o_ref[...] = x_ref[...].astype(BF16)


def _to_bf16(w):
    rows, cols = w.shape
    tr = 256
    spec = pl.BlockSpec((tr, cols), lambda i: (i, 0))
    return pl.pallas_call(
        _cast_body,
        grid=(rows // tr,),
        in_specs=[spec],
        out_specs=spec,
        out_shape=jax.ShapeDtypeStruct((rows, cols), BF16),
        compiler_params=_params(("parallel",)),
    )(w)


def _inproj_body(x_ref, pos_ref, invf_ref, g_ref, w_ref, q_ref, k_ref, v_ref, xr_ref, gate_ref):
    tm = x_ref.shape[0]
    xb = _rms(x_ref[...], g_ref[...]).astype(BF16)
    ang = pos_ref[...].astype(F32) * invf_ref[...]
    cosf = jnp.cos(ang)
    sinf = jnp.sin(ang)
    lane = lax.broadcasted_iota(jnp.int32, (tm, HEAD_DIM), 1)
    first_half = lane < ROPE_DIM // 2

    def rope(c):
        partner = jnp.where(first_half,
                            -pltpu.roll(c, HEAD_DIM - ROPE_DIM // 2, 1),
                            pltpu.roll(c, ROPE_DIM // 2, 1))
        return c * cosf + partner * sinf

    outs = (q_ref, k_ref, v_ref, xr_ref, gate_ref)
    for n, o_ref in enumerate(outs):
        y = jnp.dot(xb, w_ref[:, n * 1024:(n + 1) * 1024], preferred_element_type=F32)
        if n < 2:
            scale = HEAD_DIM ** -0.5 if n == 0 else 1.0
            for h in range(N_HEADS):
                sl = slice(h * HEAD_DIM, (h + 1) * HEAD_DIM)
                o_ref[:, sl] = rope(y[:, sl]) * scale
        else:
            o_ref[...] = y


def _inproj(x, pos, invf, g, w):
    S = x.shape[0]
    tm = IN_TM
    row = lambda n: pl.BlockSpec((tm, n), lambda i: (i, 0))
    out = jax.ShapeDtypeStruct((S, 1024), F32)
    return pl.pallas_call(
        _inproj_body,
        grid=(S // tm,),
        in_specs=[row(D_MODEL), row(1), _resident((1, HEAD_DIM)), _resident((1, D_MODEL)),
                  _resident(w.shape)],
        out_specs=[row(1024)] * 5,
        out_shape=[out] * 5,
        compiler_params=_params(("parallel",)),
    )(x, pos, invf, g, w)


def _attn_body(q_ref, k_ref, v_ref, o_ref, t4, q4, q16, k1, k4, k16, v1, v4, v16, op, ls):
    c = pl.program_id(1)
    kv = {1: (k1, v1), 4: (k4, v4), 16: (k16, v16)}

    @pl.when(c == 0)
    def _():
        for d in DILATIONS:
            for ref in kv[d]:
                ref[:, 0:ATTN_BLOCK, :] = jnp.zeros((d, ATTN_BLOCK, HEAD_DIM), BF16)

    @pl.when(c > 0)
    def _():
        for d in DILATIONS:
            rows = SPAN // d
            for ref in kv[d]:
                ref[:, 0:ATTN_BLOCK, :] = ref[:, rows:rows + ATTN_BLOCK, :]

    def deinterleave(x_ref, x1, x4, x16, off):
        if x1 is not None:
            x1[0, off:off + SPAN, :] = x_ref[...].astype(BF16)
        for r4 in range(4):
            t = x_ref[pl.ds(r4, SPAN // 4, stride=4), :]
            t4[r4] = t
            x4[r4, off:off + SPAN // 4, :] = t.astype(BF16)
        for r4 in range(4):
            for r2 in range(4):
                t = t4[r4, pl.ds(r2, SPAN // 16, stride=4), :]
                x16[4 * r2 + r4, off:off + SPAN // 16, :] = t.astype(BF16)

    deinterleave(q_ref, None, q4, q16, 0)
    deinterleave(k_ref, k1, k4, k16, ATTN_BLOCK)
    deinterleave(v_ref, v1, v4, v16, ATTN_BLOCK)

    qi = lax.broadcasted_iota(jnp.int32, (ATTN_BLOCK, 2 * ATTN_BLOCK), 0)
    kj = lax.broadcasted_iota(jnp.int32, (ATTN_BLOCK, 2 * ATTN_BLOCK), 1)
    dist = ATTN_BLOCK + qi - kj
    band = (dist >= 0) & (dist <= ATTN_BLOCK)
    bias_band = jnp.where(band, 0.0, -jnp.inf)
    first_key = jnp.where(c > 0, 0, ATTN_BLOCK)
    bias_first = jnp.where(band & (kj >= first_key), 0.0, -jnp.inf)

    def tile(qt, kk, vv, first_block):
        s = lax.dot_general(qt, kk, (((1,), (1,)), ((), ())), preferred_element_type=F32)
        s = s + (bias_first if first_block else bias_band)
        m = jnp.max(s, axis=-1, keepdims=True)
        e = jnp.exp(s - m)
        l = jnp.sum(e, axis=-1, keepdims=True)
        o = jnp.dot(e.astype(BF16), vv, preferred_element_type=F32)
        o = o / l
        lse = jnp.broadcast_to(m + jnp.log(l), (ATTN_BLOCK, HEAD_DIM))
        return o, lse

    def run_pattern(p_idx, d, q_tile):
        k_s, v_s = kv[d]
        nblk = SPAN // d // ATTN_BLOCK
        for r in range(d):
            for n in range(nblk):
                row0 = n * ATTN_BLOCK
                kk = k_s[r, row0:row0 + 2 * ATTN_BLOCK, :]
                vv = v_s[r, row0:row0 + 2 * ATTN_BLOCK, :]
                o, lse = tile(q_tile(r, row0), kk, vv, n == 0)
                start = n * (ATTN_BLOCK * d) + r
                dst = pl.ds(start, ATTN_BLOCK) if d == 1 else pl.ds(start, ATTN_BLOCK, stride=d)
                op[p_idx, dst, :] = o
                ls[p_idx, dst, :] = lse

    run_pattern(0, 1, lambda r, row0: q_ref[row0:row0 + ATTN_BLOCK, :].astype(BF16))
    run_pattern(1, 4, lambda r, row0: q4[r, row0:row0 + ATTN_BLOCK, :])
    run_pattern(2, 16, lambda r, row0: q16[r, row0:row0 + ATTN_BLOCK, :])

    l0, l1, l2 = ls[0], ls[1], ls[2]
    top = jnp.maximum(jnp.maximum(l0, l1), l2)
    w0 = jnp.exp(l0 - top)
    w1 = jnp.exp(l1 - top)
    w2 = jnp.exp(l2 - top)
    o_ref[...] = (w0 * op[0] + w1 * op[1] + w2 * op[2]) / (w0 + w1 + w2)


def _attention(q, k, v):
    S = q.shape[0]
    spec = pl.BlockSpec((SPAN, HEAD_DIM), lambda h, c: (c, h))
    slab = lambda d, off, dt: pltpu.VMEM((d, off + SPAN // d, HEAD_DIM), dt)
    scratch = [
        slab(4, 0, F32), slab(4, 0, BF16), slab(16, 0, BF16),
        slab(1, ATTN_BLOCK, BF16), slab(4, ATTN_BLOCK, BF16), slab(16, ATTN_BLOCK, BF16),
        slab(1, ATTN_BLOCK, BF16), slab(4, ATTN_BLOCK, BF16), slab(16, ATTN_BLOCK, BF16),
        pltpu.VMEM((3, SPAN, HEAD_DIM), F32), pltpu.VMEM((3, SPAN, HEAD_DIM), F32),
    ]
    return pl.pallas_call(
        _attn_body,
        grid=(N_HEADS, S // SPAN),
        in_specs=[spec] * 3,
        out_specs=spec,
        out_shape=jax.ShapeDtypeStruct((S, ATTN_WIDTH), F32),
        scratch_shapes=scratch,
        compiler_params=_params(("arbitrary", "arbitrary")),
    )(q, k, v)


def _rnn_body(xr_ref, gate_ref, cw_ref, cb_ref, wa_ref, ba_ref, wx_ref, bx_ref, lam_ref, g_ref,
              o_ref, xext, hcar, a_s, u_s):
    tm = xr_ref.shape[0]
    i = pl.program_id(0)

    @pl.when(i == 0)
    def _():
        xext[0:8, :] = jnp.zeros((8, RNN_WIDTH), F32)
        hcar[...] = jnp.zeros((8, RNN_WIDTH), F32)

    x = xr_ref[...]
    xext[8:8 + tm, :] = x
    xc = cb_ref[...]
    for tap in range(CONV_WIDTH):
        sh = CONV_WIDTH - 1 - tap
        xc = xc + xext[8 - sh:8 - sh + tm, :] * cw_ref[tap:tap + 1, :]
    xext[0:8, :] = x[tm - 8:tm, :]

    xcb = xc.astype(BF16)
    for j in range(RNN_WIDTH // 128):
        sl = slice(j * 128, (j + 1) * 128)
        blk = xcb[:, sl]
        r = jax.nn.sigmoid(jnp.dot(blk, wa_ref[j], preferred_element_type=F32) + ba_ref[:, sl])
        ig = jax.nn.sigmoid(jnp.dot(blk, wx_ref[j], preferred_element_type=F32) + bx_ref[:, sl])
        log_a = -RGLRU_C * r * jax.nn.softplus(-lam_ref[:, sl])
        a = jnp.exp(log_a)
        a_s[:, sl] = a
        u_s[:, sl] = jnp.sqrt(-jnp.tanh(log_a) * (a * a + 1.0)) * ig * xc[:, sl]

    row = lax.broadcasted_iota(jnp.int32, (8, RNN_WIDTH), 0)

    def scan(t, h):
        base = pl.multiple_of(t * 8, 8)
        A = a_s[pl.ds(base, 8), :]
        B = u_s[pl.ds(base, 8), :]
        for sft in (1, 2, 4):
            valid = row >= sft
            A_sh = pltpu.roll(A, sft, 0)
            B_sh = pltpu.roll(B, sft, 0)
            B = jnp.where(valid, A * B_sh + B, B)
            A = jnp.where(valid, A * A_sh, A)
        hs = A * h + B
        u_s[pl.ds(base, 8), :] = hs
        return jnp.broadcast_to(hs[7:8, :], (8, RNN_WIDTH))

    hcar[...] = lax.fori_loop(0, tm // 8, scan, hcar[...])

    rnn = u_s[...] * jax.nn.gelu(gate_ref[...])
    o_ref[...] = _rms(rnn, g_ref[...]).astype(BF16)


def _rnn(xr, gate, cw, cb, wa, ba, wx, bx, lam, g):
    S = xr.shape[0]
    tm = RNN_TM
    row = pl.BlockSpec((tm, RNN_WIDTH), lambda i: (i, 0))
    vec = _resident((1, RNN_WIDTH))
    return pl.pallas_call(
        _rnn_body,
        grid=(S // tm,),
        in_specs=[row, row, _resident(cw.shape), vec, _resident(wa.shape), vec,
                  _resident(wx.shape), vec, vec, vec],
        out_specs=row,
        out_shape=jax.ShapeDtypeStruct((S, RNN_WIDTH), BF16),
        scratch_shapes=[pltpu.VMEM((tm + 8, RNN_WIDTH), F32), pltpu.VMEM((8, RNN_WIDTH), F32),
                        pltpu.VMEM((tm, RNN_WIDTH), F32), pltpu.VMEM((tm, RNN_WIDTH), F32)],
        compiler_params=_params(("arbitrary",)),
    )(xr, gate, cw, cb, wa, ba, wx, bx, lam, g)


def _outproj_body(attn_ref, rnn_ref, x_ref, ga_ref, wo_ref, gf_ref, wq_ref, sk_ref,
                  h1_ref, hnt_ref, st_ref):
    an = _rms(attn_ref[...], ga_ref[...]).astype(BF16)
    mix = jnp.dot(an, wo_ref[0:ATTN_WIDTH, :], preferred_element_type=F32)
    mix = mix + jnp.dot(rnn_ref[...], wo_ref[ATTN_WIDTH:, :], preferred_element_type=F32)
    h1 = x_ref[...] + mix
    h1_ref[...] = h1
    hn32 = _rms(h1, gf_ref[...])
    hn = hn32.astype(BF16)
    hnt_ref[...] = hn32.T.astype(BF16)
    qp = jnp.dot(hn, wq_ref[...], preferred_element_type=F32).astype(BF16)
    for hc in range(2 * PEER_HEADS):
        st_ref[hc] = lax.dot_general(sk_ref[hc], qp[:, hc * 128:(hc + 1) * 128],
                                     (((1,), (1,)), ((), ())), preferred_element_type=F32)


def _outproj(attn, rnn, x, ga, wo, gf, wq, sk):
    S = x.shape[0]
    tm = OUT_TM
    row = lambda n: pl.BlockSpec((tm, n), lambda i: (i, 0))
    return pl.pallas_call(
        _outproj_body,
        grid=(S // tm,),
        in_specs=[row(ATTN_WIDTH), row(RNN_WIDTH), row(D_MODEL), _resident((1, ATTN_WIDTH)),
                  _resident(wo.shape), _resident((1, D_MODEL)), _resident(wq.shape),
                  _resident(sk.shape)],
        out_specs=[row(D_MODEL), pl.BlockSpec((D_MODEL, tm), lambda i: (0, i)),
                   pl.BlockSpec((2 * PEER_HEADS, PEER_NKEYS, tm), lambda i: (0, 0, i))],
        out_shape=[jax.ShapeDtypeStruct((S, D_MODEL), F32), jax.ShapeDtypeStruct((D_MODEL, S), BF16),
                   jax.ShapeDtypeStruct((2 * PEER_HEADS, PEER_NKEYS, S), F32)],
        compiler_params=_params(("parallel",)),
    )(attn, rnn, x, ga, wo, gf, wq, sk)


def _sort16_network():
    pairs = []

    def merge(lo, n, r):
        step = 2 * r
        if step < n:
            merge(lo, n, step)
            merge(lo + r, n, step)
            pairs.extend((i, i + r) for i in range(lo + r, lo + n - r, step))
        else:
            pairs.append((lo, lo + r))

    def sort(lo, n):
        if n > 1:
            sort(lo, n // 2)
            sort(lo + n // 2, n // 2)
            merge(lo, n, 1)

    sort(0, 16)
    return tuple(pairs)


_SORT16 = _sort16_network()
_SORT10 = tuple(p for p in _SORT16 if p[1] < 10)


def _select_body(st_ref, r2_ref, e2_ref, jb_ref, e1n_ref, tops, tau_s):
    tb = st_ref.shape[-1]
    K = PEER_TOPK
    NA = 2 * PEER_HEADS
    ninf = -jnp.inf
    row8 = lax.broadcasted_iota(jnp.int32, (8, tb), 0)

    def extract(a, carry):
        v = [st_ref[a, 8 * g:8 * g + 8, :] for g in range(PEER_NKEYS // 8)]
        for i, j in _SORT16:
            v[i], v[j] = jnp.maximum(v[i], v[j]), jnp.minimum(v[i], v[j])
        for k in range(K):
            mx = jnp.max(v[0], axis=0, keepdims=True)
            tops[a, k:k + 1, :] = mx
            hit = v[0] == mx
            for g in range(K - 1 - k):
                v[g] = jnp.where(hit, v[g + 1], v[g])
        return carry

    lax.fori_loop(0, NA, extract, 0, unroll=4)

    def candidates(h):
        A = tops[2 * h]
        B = tops[2 * h + 1]
        B8 = B[0:8, :]
        limits = (16, 8, 5, 4, 3, 2, 2, 2)
        cands = [A[0:1, :] + B, A[1:2, :] + B8]
        for i in range(2, 8):
            cands.append(jnp.where(row8 < limits[i], A[i:i + 1, :] + B8, ninf))
        cands.append(A[8:16, :] + B[0:1, :])
        return A, B, cands

    def kth(h, carry):
        cands = candidates(h)[2]
        v = [cands[0][0:8, :], cands[0][8:16, :]] + cands[1:]
        for i, j in _SORT10:
            v[i], v[j] = jnp.maximum(v[i], v[j]), jnp.minimum(v[i], v[j])
        for k in range(K):
            mx = jnp.max(v[0], axis=0, keepdims=True)
            hit = v[0] == mx
            for g in range(min(len(v), K - 1 - k)):
                v[g] = jnp.where(hit, v[g + 1] if g + 1 < len(v) else ninf, v[g])
        tau_s[pl.ds(h, 1), :] = mx
        return carry

    lax.fori_loop(0, PEER_HEADS, kth, 0, unroll=4)

    for h in range(PEER_HEADS):
        A, B, cands = candidates(h)
        tau = tau_s[h:h + 1, :]
        cmax = A[0:1, :] + B[0:1, :]
        z = None
        for c in cands:
            zc = jnp.sum(jnp.where(c >= tau, jnp.exp(c - cmax), 0.0), axis=0, keepdims=True)
            z = zc if z is None else z + zc

        j_rows = [jnp.sum(jnp.where(cands[i] >= tau, 1.0, 0.0), axis=0, keepdims=True)
                  for i in range(8)]
        tail = jnp.where(cands[8] >= tau, 1.0, 0.0)
        for i in range(8):
            j_rows.append(tail[i:i + 1, :])

        s1 = st_ref[2 * h]
        s2 = st_ref[2 * h + 1]
        jb = jnp.zeros((PEER_NKEYS, tb), F32)
        r2 = jnp.full((PEER_NKEYS, tb), float(K), F32)
        for i in range(K):
            jb = jnp.where(s1 == A[i:i + 1, :], j_rows[i], jb)
            r2 = jnp.where(s2 == B[i:i + 1, :], float(i), r2)
        jb_ref[h] = jb
        r2_ref[h] = r2.astype(BF16)
        e1n_ref[h] = jnp.exp(s1 - A[0:1, :]) / z
        e2_ref[h] = jnp.exp(s2 - B[0:1, :]).astype(BF16)


def _select(st):
    S = st.shape[-1]
    tb = SEL_TB
    out = lambda dt: jax.ShapeDtypeStruct((PEER_HEADS, PEER_NKEYS, S), dt)
    ospec = pl.BlockSpec((PEER_HEADS, PEER_NKEYS, tb), lambda i: (0, 0, i))
    return pl.pallas_call(
        _select_body,
        grid=(S // tb,),
        in_specs=[pl.BlockSpec((2 * PEER_HEADS, PEER_NKEYS, tb), lambda i: (0, 0, i))],
        out_specs=[ospec] * 4,
        out_shape=[out(BF16), out(BF16), out(F32), out(F32)],
        scratch_shapes=[pltpu.VMEM((2 * PEER_HEADS, PEER_TOPK, tb), F32),
                        pltpu.VMEM((PEER_HEADS, tb), F32)],
        compiler_params=_params(("parallel",)),
    )(st)


def _peer_body(hnt_ref, u_ref, vt_ref, r2_ref, e2_ref, jb_ref, e1n_ref, o_ref, w_even, w_odd):
    j = pl.program_id(1)
    ec = PEER_EC
    n_i1 = ec // PEER_NKEYS

    zero = jnp.zeros((), BF16)

    def build_gates(step, w_s, c, kk):
        i1 = jnp.minimum((step * PEER_CHUNKS + c) * n_i1 + kk, PEER_NKEYS - 1)
        w = None
        for h in range(PEER_HEADS):
            jb_b = jb_ref[h, pl.ds(i1, 1), :].astype(BF16)
            e1_b = e1n_ref[h, pl.ds(i1, 1), :].astype(BF16)
            t = jnp.where(r2_ref[h] < jb_b, e2_ref[h], zero) * e1_b
            w = t if w is None else w + t
        w_s[c, kk * PEER_NKEYS:(kk + 1) * PEER_NKEYS, :] = w

    @pl.when(j == 0)
    def _():
        o_ref[...] = jnp.zeros(o_ref.shape, F32)
        for c in range(PEER_CHUNKS):
            for kk in range(n_i1):
                build_gates(0, w_even, c, kk)

    def step(w_cur, w_next):
        pieces = [(c, kk) for c in range(PEER_CHUNKS) for kk in range(n_i1)]
        n_pieces = len(pieces)

        def gates(n):
            for _ in range(n):
                if pieces:
                    build_gates(j + 1, w_next, *pieces.pop(0))

        acts = []
        for c in range(PEER_CHUNKS):
            acts.append(jnp.dot(u_ref[c * ec:(c + 1) * ec, :], hnt_ref[...],
                                preferred_element_type=F32))
            gates(PEER_GATES_EARLY)
        nrb = PEER_ROW_BLOCKS
        rows = D_MODEL // nrb
        erows = ec // nrb
        p = w_cur[0] * jax.nn.gelu(acts[0]).astype(BF16)
        for c in range(PEER_CHUNKS):
            p_next = []
            for rb in range(nrb):
                rsl = slice(rb * rows, (rb + 1) * rows)
                o_ref[rsl, :] += jnp.dot(vt_ref[c, rsl, :], p, preferred_element_type=F32)
                if c + 1 < PEER_CHUNKS:
                    esl = slice(rb * erows, (rb + 1) * erows)
                    p_next.append(w_cur[c + 1, esl, :] * jax.nn.gelu(acts[c + 1][esl, :]).astype(BF16))
                done = c * nrb + rb + 1
                gates(done * n_pieces // (PEER_CHUNKS * nrb) - (n_pieces - len(pieces)))
            if p_next:
                p = jnp.concatenate(p_next, axis=0)
        gates(len(pieces))

    @pl.when(j % 2 == 0)
    def _():
        step(w_even, w_odd)

    @pl.when(j % 2 == 1)
    def _():
        step(w_odd, w_even)


def _peer(hnt, u, vt, r2, e2, jb, e1n):
    S = hnt.shape[1]
    tm, ec, nc = PEER_TM, PEER_EC, PEER_CHUNKS
    tokt = pl.BlockSpec((D_MODEL, tm), lambda i, j: (0, i))
    sel = pl.BlockSpec((PEER_HEADS, PEER_NKEYS, tm), lambda i, j: (0, 0, i))
    return pl.pallas_call(
        _peer_body,
        grid=(S // tm, PEER_EXPERTS // (ec * nc)),
        in_specs=[tokt,
                  pl.BlockSpec((ec * nc, D_MODEL), lambda i, j: (j, 0)),
                  pl.BlockSpec((nc, D_MODEL, ec), lambda i, j: (j, 0, 0)),
                  sel, sel, sel, sel],
        out_specs=tokt,
        out_shape=jax.ShapeDtypeStruct((D_MODEL, S), F32),
        scratch_shapes=[pltpu.VMEM((nc, ec, tm), BF16), pltpu.VMEM((nc, ec, tm), BF16)],
        compiler_params=_params(("parallel", "arbitrary")),
    )(hnt, u, vt, r2, e2, jb, e1n)


def _ple_body(h_ref, ft_ref, p_ref, gp_ref, wg_ref, bg_ref, wp_ref, gf_ref, o_ref):
    half = h_ref.shape[0] // 2
    for r in range(2):
        rsl = slice(r * half, (r + 1) * half)
        h = h_ref[rsl, :] + ft_ref[:, rsl].T
        hn = _rms(h, gp_ref[...]).astype(BF16)
        g = jax.nn.sigmoid(jnp.dot(hn, wg_ref[...], preferred_element_type=F32) + bg_ref[...])
        pp = jnp.dot(p_ref[rsl, :].astype(BF16), wp_ref[...], preferred_element_type=F32)
        o_ref[rsl, :] = _rms(h + g * pp, gf_ref[...])


def _ple(h, ft, p, gp, wg, bg, wp, gf):
    S = h.shape[0]
    tm = PLE_TM
    row = lambda n: pl.BlockSpec((tm, n), lambda i: (i, 0))
    vec = _resident((1, D_MODEL))
    return pl.pallas_call(
        _ple_body,
        grid=(S // tm,),
        in_specs=[row(D_MODEL), pl.BlockSpec((D_MODEL, tm), lambda i: (0, i)), row(PLE_DIM), vec,
                  _resident(wg.shape), vec, _resident(wp.shape), vec],
        out_specs=row(D_MODEL),
        out_shape=jax.ShapeDtypeStruct((S, D_MODEL), F32),
        compiler_params=_params(("parallel",)),
    )(h, ft, p, gp, wg, bg, wp, gf)


def _pair_blocks(w):
    w = w.reshape(8, 2, 64, 64)
    z = jnp.zeros((8, 64, 64), w.dtype)
    top = jnp.concatenate([w[:, 0], z], axis=-1)
    bot = jnp.concatenate([z, w[:, 1]], axis=-1)
    return jnp.concatenate([top, bot], axis=-2).astype(BF16)


def kernel(x, p, positions, mix_norm, w_in, conv_w, conv_b, rg_wa, rg_ba, rg_wx, rg_bx, rg_lambda,
           attn_out_norm, rnn_out_norm, w_out, ffn_norm, peer_wq, peer_subkeys, peer_u, peer_v,
           ple_norm, ple_w_gate, ple_b_gate, ple_proj, final_norm):
    B, S, D = x.shape
    assert (B, S, D) == (1, SEQ, D_MODEL) and S % SPAN == 0 and w_in.shape[0] == 1
    vec = lambda a: a.reshape(1, -1).astype(F32)

    half = ROPE_DIM // 2
    inv_freq = ROPE_THETA ** (-jnp.arange(half, dtype=F32) * 2.0 / ROPE_DIM)
    invf = jnp.zeros((1, HEAD_DIM), F32).at[0, :ROPE_DIM].set(jnp.tile(inv_freq, 2))

    q, k, v, xr, gate = _inproj(x[0], positions.reshape(S, 1), invf, vec(mix_norm[0]),
                                _to_bf16(w_in[0]))
    attn = _attention(q, k, v)
    rnn = _rnn(xr, gate, conv_w[0], vec(conv_b[0]), _pair_blocks(rg_wa[0]), vec(rg_ba[0]),
               _pair_blocks(rg_wx[0]), vec(rg_bx[0]), vec(rg_lambda[0]), vec(rnn_out_norm[0]))
    sk = peer_subkeys[0].reshape(2 * PEER_HEADS, PEER_NKEYS, -1).astype(BF16)
    h1, hnt, st = _outproj(attn, rnn, x[0], vec(attn_out_norm[0]), w_out[0].astype(BF16),
                           vec(ffn_norm[0]), peer_wq[0].astype(BF16), sk)
    r2, e2, jb, e1n = _select(st)
    vt = peer_v[0].astype(BF16).reshape(PEER_EXPERTS // PEER_EC, PEER_EC, D).transpose(0, 2, 1)
    ft = _peer(hnt, peer_u[0].astype(BF16), vt, r2, e2, jb, e1n)
    out = _ple(h1, ft, p[0, 0], vec(ple_norm[0]), ple_w_gate[0].astype(BF16), vec(ple_b_gate[0]),
               ple_proj[0].astype(BF16), vec(final_norm))
    return out.reshape(B, S, D)
```

```python
import jax
import jax.numpy as jnp
from jax import lax
from jax.experimental import pallas as pl
from jax.experimental.pallas import tpu as pltpu

F32 = jnp.float32
BF16 = jnp.bfloat16

D_MODEL = 2048
SEQ = 8192
HEAD_DIM = 128
ATTN_WIDTH = 1024
N_HEADS = 8
ROPE_DIM = 32
ROPE_THETA = 500000.0
ATTN_BLOCK = 128
DILATIONS = (1, 4, 16)
SPAN = ATTN_BLOCK * 16
RNN_WIDTH = 1024
CONV_WIDTH = 4
RGLRU_C = 8.0
PEER_HEADS = 8
PEER_NKEYS = 128
PEER_EXPERTS = PEER_NKEYS * PEER_NKEYS
PEER_TOPK = 16
PLE_DIM = 256
EPS = 1e-6

VMEM_LIMIT = 56 * 1024 * 1024

IN_TM = 512
RNN_TM = 256
OUT_TM = 256
SEL_TB = 256
PEER_TM = 512
PEER_EC = 512
PEER_CHUNKS = 2
PEER_ROW_BLOCKS = 4
PLE_TM = 512


def _params(sem):
    return pltpu.CompilerParams(dimension_semantics=sem, vmem_limit_bytes=VMEM_LIMIT)


def _rms(x, g):
    return x * lax.rsqrt(jnp.mean(x * x, axis=-1, keepdims=True) + EPS) * g


def _resident(shape):
    nd = len(shape)
    return pl.BlockSpec(shape, lambda *_: (0,) * nd, pipeline_mode=pl.Buffered(1))


def _inproj_body(x_ref, pos_ref, invf_ref, g_ref, w_ref, q_ref, k_ref, v_ref, xr_ref, gate_ref):
    tm = x_ref.shape[0]
    xb = _rms(x_ref[...], g_ref[...]).astype(BF16)
    ang = pos_ref[...].astype(F32) * invf_ref[...]
    cosf = jnp.cos(ang)
    sinf = jnp.sin(ang)
    lane = lax.broadcasted_iota(jnp.int32, (tm, HEAD_DIM), 1)
    first_half = lane < ROPE_DIM // 2

    def rope(c):
        partner = jnp.where(first_half,
                            -pltpu.roll(c, HEAD_DIM - ROPE_DIM // 2, 1),
                            pltpu.roll(c, ROPE_DIM // 2, 1))
        return c * cosf + partner * sinf

    outs = (q_ref, k_ref, v_ref, xr_ref, gate_ref)
    for n, o_ref in enumerate(outs):
        y = jnp.dot(xb, w_ref[:, n * 1024:(n + 1) * 1024], preferred_element_type=F32)
        if n < 2:
            scale = HEAD_DIM ** -0.5 if n == 0 else 1.0
            for h in range(N_HEADS):
                sl = slice(h * HEAD_DIM, (h + 1) * HEAD_DIM)
                o_ref[:, sl] = rope(y[:, sl]) * scale
        else:
            o_ref[...] = y


def _inproj(x, pos, invf, g, w):
    S = x.shape[0]
    tm = IN_TM
    row = lambda n: pl.BlockSpec((tm, n), lambda i: (i, 0))
    out = jax.ShapeDtypeStruct((S, 1024), F32)
    return pl.pallas_call(
        _inproj_body,
        grid=(S // tm,),
        in_specs=[row(D_MODEL), row(1), _resident((1, HEAD_DIM)), _resident((1, D_MODEL)),
                  _resident(w.shape)],
        out_specs=[row(1024)] * 5,
        out_shape=[out] * 5,
        compiler_params=_params(("parallel",)),
    )(x, pos, invf, g, w)


def _attn_body(q_ref, k_ref, v_ref, o_ref, t4, q4, q16, k1, k4, k16, v1, v4, v16, op, ls):
    c = pl.program_id(1)
    kv = {1: (k1, v1), 4: (k4, v4), 16: (k16, v16)}

    @pl.when(c == 0)
    def _():
        for d in DILATIONS:
            for ref in kv[d]:
                ref[:, 0:ATTN_BLOCK, :] = jnp.zeros((d, ATTN_BLOCK, HEAD_DIM), BF16)

    @pl.when(c > 0)
    def _():
        for d in DILATIONS:
            rows = SPAN // d
            for ref in kv[d]:
                ref[:, 0:ATTN_BLOCK, :] = ref[:, rows:rows + ATTN_BLOCK, :]

    def deinterleave(x_ref, x1, x4, x16, off):
        if x1 is not None:
            x1[0, off:off + SPAN, :] = x_ref[...].astype(BF16)
        for r4 in range(4):
            t = x_ref[pl.ds(r4, SPAN // 4, stride=4), :]
            t4[r4] = t
            x4[r4, off:off + SPAN // 4, :] = t.astype(BF16)
        for r4 in range(4):
            for r2 in range(4):
                t = t4[r4, pl.ds(r2, SPAN // 16, stride=4), :]
                x16[4 * r2 + r4, off:off + SPAN // 16, :] = t.astype(BF16)

    deinterleave(q_ref, None, q4, q16, 0)
    deinterleave(k_ref, k1, k4, k16, ATTN_BLOCK)
    deinterleave(v_ref, v1, v4, v16, ATTN_BLOCK)

    qi = lax.broadcasted_iota(jnp.int32, (ATTN_BLOCK, 2 * ATTN_BLOCK), 0)
    kj = lax.broadcasted_iota(jnp.int32, (ATTN_BLOCK, 2 * ATTN_BLOCK), 1)
    dist = ATTN_BLOCK + qi - kj
    band = (dist >= 0) & (dist <= ATTN_BLOCK)
    bias_band = jnp.where(band, 0.0, -jnp.inf)
    first_key = jnp.where(c > 0, 0, ATTN_BLOCK)
    bias_first = jnp.where(band & (kj >= first_key), 0.0, -jnp.inf)

    def tile(qt, kk, vv, first_block):
        s = lax.dot_general(qt, kk, (((1,), (1,)), ((), ())), preferred_element_type=F32)
        s = s + (bias_first if first_block else bias_band)
        m = jnp.max(s, axis=-1, keepdims=True)
        e = jnp.exp(s - m)
        l = jnp.sum(e, axis=-1, keepdims=True)
        o = jnp.dot(e.astype(BF16), vv, preferred_element_type=F32)
        o = o / l
        lse = jnp.broadcast_to(m + jnp.log(l), (ATTN_BLOCK, HEAD_DIM))
        return o, lse

    def run_pattern(p_idx, d, q_tile):
        k_s, v_s = kv[d]
        nblk = SPAN // d // ATTN_BLOCK
        for r in range(d):
            for n in range(nblk):
                row0 = n * ATTN_BLOCK
                kk = k_s[r, row0:row0 + 2 * ATTN_BLOCK, :]
                vv = v_s[r, row0:row0 + 2 * ATTN_BLOCK, :]
                o, lse = tile(q_tile(r, row0), kk, vv, n == 0)
                start = n * (ATTN_BLOCK * d) + r
                dst = pl.ds(start, ATTN_BLOCK) if d == 1 else pl.ds(start, ATTN_BLOCK, stride=d)
                op[p_idx, dst, :] = o
                ls[p_idx, dst, :] = lse

    run_pattern(0, 1, lambda r, row0: q_ref[row0:row0 + ATTN_BLOCK, :].astype(BF16))
    run_pattern(1, 4, lambda r, row0: q4[r, row0:row0 + ATTN_BLOCK, :])
    run_pattern(2, 16, lambda r, row0: q16[r, row0:row0 + ATTN_BLOCK, :])

    l0, l1, l2 = ls[0], ls[1], ls[2]
    top = jnp.maximum(jnp.maximum(l0, l1), l2)
    w0 = jnp.exp(l0 - top)
    w1 = jnp.exp(l1 - top)
    w2 = jnp.exp(l2 - top)
    o_ref[...] = (w0 * op[0] + w1 * op[1] + w2 * op[2]) / (w0 + w1 + w2)


def _attention(q, k, v):
    S = q.shape[0]
    spec = pl.BlockSpec((SPAN, HEAD_DIM), lambda h, c: (c, h))
    slab = lambda d, off, dt: pltpu.VMEM((d, off + SPAN // d, HEAD_DIM), dt)
    scratch = [
        slab(4, 0, F32), slab(4, 0, BF16), slab(16, 0, BF16),
        slab(1, ATTN_BLOCK, BF16), slab(4, ATTN_BLOCK, BF16), slab(16, ATTN_BLOCK, BF16),
        slab(1, ATTN_BLOCK, BF16), slab(4, ATTN_BLOCK, BF16), slab(16, ATTN_BLOCK, BF16),
        pltpu.VMEM((3, SPAN, HEAD_DIM), F32), pltpu.VMEM((3, SPAN, HEAD_DIM), F32),
    ]
    return pl.pallas_call(
        _attn_body,
        grid=(N_HEADS, S // SPAN),
        in_specs=[spec] * 3,
        out_specs=spec,
        out_shape=jax.ShapeDtypeStruct((S, ATTN_WIDTH), F32),
        scratch_shapes=scratch,
        compiler_params=_params(("arbitrary", "arbitrary")),
    )(q, k, v)


def _rnn_body(xr_ref, gate_ref, cw_ref, cb_ref, wa_ref, ba_ref, wx_ref, bx_ref, lam_ref, g_ref,
              o_ref, xext, hcar, a_s, u_s):
    tm = xr_ref.shape[0]
    i = pl.program_id(0)

    @pl.when(i == 0)
    def _():
        xext[0:8, :] = jnp.zeros((8, RNN_WIDTH), F32)
        hcar[...] = jnp.zeros((8, RNN_WIDTH), F32)

    x = xr_ref[...]
    xext[8:8 + tm, :] = x
    xc = cb_ref[...]
    for tap in range(CONV_WIDTH):
        sh = CONV_WIDTH - 1 - tap
        xc = xc + xext[8 - sh:8 - sh + tm, :] * cw_ref[tap:tap + 1, :]
    xext[0:8, :] = x[tm - 8:tm, :]

    xcb = xc.astype(BF16)
    for j in range(RNN_WIDTH // 128):
        sl = slice(j * 128, (j + 1) * 128)
        blk = xcb[:, sl]
        r = jax.nn.sigmoid(jnp.dot(blk, wa_ref[j], preferred_element_type=F32) + ba_ref[:, sl])
        ig = jax.nn.sigmoid(jnp.dot(blk, wx_ref[j], preferred_element_type=F32) + bx_ref[:, sl])
        log_a = -RGLRU_C * r * jax.nn.softplus(-lam_ref[:, sl])
        a = jnp.exp(log_a)
        a_s[:, sl] = a
        u_s[:, sl] = jnp.sqrt(-jnp.tanh(log_a) * (a * a + 1.0)) * ig * xc[:, sl]

    row = lax.broadcasted_iota(jnp.int32, (8, RNN_WIDTH), 0)

    def scan(t, h):
        base = pl.multiple_of(t * 8, 8)
        A = a_s[pl.ds(base, 8), :]
        B = u_s[pl.ds(base, 8), :]
        for sft in (1, 2, 4):
            valid = row >= sft
            A_sh = pltpu.roll(A, sft, 0)
            B_sh = pltpu.roll(B, sft, 0)
            B = jnp.where(valid, A * B_sh + B, B)
            A = jnp.where(valid, A * A_sh, A)
        hs = A * h + B
        u_s[pl.ds(base, 8), :] = hs
        return jnp.broadcast_to(hs[7:8, :], (8, RNN_WIDTH))

    hcar[...] = lax.fori_loop(0, tm // 8, scan, hcar[...])

    rnn = u_s[...] * jax.nn.gelu(gate_ref[...])
    o_ref[...] = _rms(rnn, g_ref[...]).astype(BF16)


def _rnn(xr, gate, cw, cb, wa, ba, wx, bx, lam, g):
    S = xr.shape[0]
    tm = RNN_TM
    row = pl.BlockSpec((tm, RNN_WIDTH), lambda i: (i, 0))
    vec = _resident((1, RNN_WIDTH))
    return pl.pallas_call(
        _rnn_body,
        grid=(S // tm,),
        in_specs=[row, row, _resident(cw.shape), vec, _resident(wa.shape), vec,
                  _resident(wx.shape), vec, vec, vec],
        out_specs=row,
        out_shape=jax.ShapeDtypeStruct((S, RNN_WIDTH), BF16),
        scratch_shapes=[pltpu.VMEM((tm + 8, RNN_WIDTH), F32), pltpu.VMEM((8, RNN_WIDTH), F32),
                        pltpu.VMEM((tm, RNN_WIDTH), F32), pltpu.VMEM((tm, RNN_WIDTH), F32)],
        compiler_params=_params(("arbitrary",)),
    )(xr, gate, cw, cb, wa, ba, wx, bx, lam, g)


def _outproj_body(attn_ref, rnn_ref, x_ref, ga_ref, wo_ref, gf_ref, wq_ref, sk_ref,
                  h1_ref, hnt_ref, st_ref):
    an = _rms(attn_ref[...], ga_ref[...]).astype(BF16)
    mix = jnp.dot(an, wo_ref[0:ATTN_WIDTH, :], preferred_element_type=F32)
    mix = mix + jnp.dot(rnn_ref[...], wo_ref[ATTN_WIDTH:, :], preferred_element_type=F32)
    h1 = x_ref[...] + mix
    h1_ref[...] = h1
    hn32 = _rms(h1, gf_ref[...])
    hn = hn32.astype(BF16)
    hnt_ref[...] = hn32.T.astype(BF16)
    qp = jnp.dot(hn, wq_ref[...], preferred_element_type=F32).astype(BF16)
    for hc in range(2 * PEER_HEADS):
        st_ref[hc] = lax.dot_general(sk_ref[hc], qp[:, hc * 128:(hc + 1) * 128],
                                     (((1,), (1,)), ((), ())), preferred_element_type=F32)


def _outproj(attn, rnn, x, ga, wo, gf, wq, sk):
    S = x.shape[0]
    tm = OUT_TM
    row = lambda n: pl.BlockSpec((tm, n), lambda i: (i, 0))
    return pl.pallas_call(
        _outproj_body,
        grid=(S // tm,),
        in_specs=[row(ATTN_WIDTH), row(RNN_WIDTH), row(D_MODEL), _resident((1, ATTN_WIDTH)),
                  _resident(wo.shape), _resident((1, D_MODEL)), _resident(wq.shape),
                  _resident(sk.shape)],
        out_specs=[row(D_MODEL), pl.BlockSpec((D_MODEL, tm), lambda i: (0, i)),
                   pl.BlockSpec((2 * PEER_HEADS, PEER_NKEYS, tm), lambda i: (0, 0, i))],
        out_shape=[jax.ShapeDtypeStruct((S, D_MODEL), F32), jax.ShapeDtypeStruct((D_MODEL, S), BF16),
                   jax.ShapeDtypeStruct((2 * PEER_HEADS, PEER_NKEYS, S), F32)],
        compiler_params=_params(("parallel",)),
    )(attn, rnn, x, ga, wo, gf, wq, sk)


def _sort16_network():
    pairs = []

    def merge(lo, n, r):
        step = 2 * r
        if step < n:
            merge(lo, n, step)
            merge(lo + r, n, step)
            pairs.extend((i, i + r) for i in range(lo + r, lo + n - r, step))
        else:
            pairs.append((lo, lo + r))

    def sort(lo, n):
        if n > 1:
            sort(lo, n // 2)
            sort(lo + n // 2, n // 2)
            merge(lo, n, 1)

    sort(0, 16)
    return tuple(pairs)


_SORT16 = _sort16_network()
_SORT10 = tuple(p for p in _SORT16 if p[1] < 10)


def _select_body(st_ref, r2_ref, e2_ref, jb_ref, e1n_ref, tops, tau_s):
    tb = st_ref.shape[-1]
    K = PEER_TOPK
    NA = 2 * PEER_HEADS
    ninf = -jnp.inf
    row8 = lax.broadcasted_iota(jnp.int32, (8, tb), 0)

    def pop_max(v, depth):
        mx = jnp.max(v[0], axis=0, keepdims=True)
        where_hit = jnp.where(v[0] == mx, row8, 8)
        hit = where_hit == jnp.min(where_hit, axis=0, keepdims=True)
        for g in range(min(len(v), depth)):
            v[g] = jnp.where(hit, v[g + 1] if g + 1 < len(v) else ninf, v[g])
        return mx

    def extract(a, carry):
        v = [st_ref[a, 8 * g:8 * g + 8, :] for g in range(PEER_NKEYS // 8)]
        for i, j in _SORT16:
            v[i], v[j] = jnp.maximum(v[i], v[j]), jnp.minimum(v[i], v[j])
        for k in range(K):
            tops[a, k:k + 1, :] = pop_max(v, K - 1 - k)
        return carry

    lax.fori_loop(0, NA, extract, 0, unroll=4)

    def candidates(h):
        A = tops[2 * h]
        B = tops[2 * h + 1]
        B8 = B[0:8, :]
        limits = (16, 8, 5, 4, 3, 2, 2, 2)
        cands = [A[0:1, :] + B, A[1:2, :] + B8]
        for i in range(2, 8):
            cands.append(jnp.where(row8 < limits[i], A[i:i + 1, :] + B8, ninf))
        cands.append(A[8:16, :] + B[0:1, :])
        return A, B, cands

    def kth(h, carry):
        cands = candidates(h)[2]
        v = [cands[0][0:8, :], cands[0][8:16, :]] + cands[1:]
        for i, j in _SORT10:
            v[i], v[j] = jnp.maximum(v[i], v[j]), jnp.minimum(v[i], v[j])
        for k in range(K):
            tau = pop_max(v, K - 1 - k)
        tau_s[pl.ds(h, 1), :] = tau
        return carry

    lax.fori_loop(0, PEER_HEADS, kth, 0, unroll=4)

    for h in range(PEER_HEADS):
        A, B, cands = candidates(h)
        tau = tau_s[h:h + 1, :]
        cmax = A[0:1, :] + B[0:1, :]
        z = None
        for c in cands:
            zc = jnp.sum(jnp.where(c >= tau, jnp.exp(c - cmax), 0.0), axis=0, keepdims=True)
            z = zc if z is None else z + zc

        j_rows = [jnp.sum(jnp.where(cands[i] >= tau, 1.0, 0.0), axis=0, keepdims=True)
                  for i in range(8)]
        tail = jnp.where(cands[8] >= tau, 1.0, 0.0)
        for i in range(8):
            j_rows.append(tail[i:i + 1, :])

        s1 = st_ref[2 * h]
        s2 = st_ref[2 * h + 1]
        jb = jnp.zeros((PEER_NKEYS, tb), F32)
        r2 = jnp.full((PEER_NKEYS, tb), float(K), F32)
        for i in reversed(range(K)):
            jb = jnp.where(s1 == A[i:i + 1, :], j_rows[i], jb)
            r2 = jnp.where(s2 == B[i:i + 1, :], float(i), r2)
        jb_ref[h] = jb
        r2_ref[h] = r2.astype(BF16)
        e1n_ref[h] = jnp.exp(s1 - A[0:1, :]) / z
        e2_ref[h] = jnp.exp(s2 - B[0:1, :]).astype(BF16)


def _select(st):
    S = st.shape[-1]
    tb = SEL_TB
    out = lambda dt: jax.ShapeDtypeStruct((PEER_HEADS, PEER_NKEYS, S), dt)
    ospec = pl.BlockSpec((PEER_HEADS, PEER_NKEYS, tb), lambda i: (0, 0, i))
    return pl.pallas_call(
        _select_body,
        grid=(S // tb,),
        in_specs=[pl.BlockSpec((2 * PEER_HEADS, PEER_NKEYS, tb), lambda i: (0, 0, i))],
        out_specs=[ospec] * 4,
        out_shape=[out(BF16), out(BF16), out(F32), out(F32)],
        scratch_shapes=[pltpu.VMEM((2 * PEER_HEADS, PEER_TOPK, tb), F32),
                        pltpu.VMEM((PEER_HEADS, tb), F32)],
        compiler_params=_params(("parallel",)),
    )(st)


def _peer_body(hnt_ref, u_ref, vt_ref, r2_ref, e2_ref, jb_ref, e1n_ref, o_ref, w_even, w_odd):
    j = pl.program_id(1)
    ec = PEER_EC
    n_i1 = ec // PEER_NKEYS

    zero = jnp.zeros((), BF16)

    def build_gates(step, w_s, c, kk):
        i1 = jnp.minimum((step * PEER_CHUNKS + c) * n_i1 + kk, PEER_NKEYS - 1)
        w = None
        for h in range(PEER_HEADS):
            jb_b = jb_ref[h, pl.ds(i1, 1), :].astype(BF16)
            e1_b = e1n_ref[h, pl.ds(i1, 1), :].astype(BF16)
            t = jnp.where(r2_ref[h] < jb_b, e2_ref[h], zero) * e1_b
            w = t if w is None else w + t
        w_s[c, kk * PEER_NKEYS:(kk + 1) * PEER_NKEYS, :] = w

    @pl.when(j == 0)
    def _():
        o_ref[...] = jnp.zeros(o_ref.shape, F32)
        for c in range(PEER_CHUNKS):
            for kk in range(n_i1):
                build_gates(0, w_even, c, kk)

    def step(w_cur, w_next):
        pieces = [(c, kk) for c in range(PEER_CHUNKS) for kk in range(n_i1)]
        n_subdots = PEER_CHUNKS * PEER_ROW_BLOCKS
        assert len(pieces) % n_subdots == 0
        per_subdot = len(pieces) // n_subdots

        acts = [jnp.dot(u_ref[c * ec:(c + 1) * ec, :], hnt_ref[...], preferred_element_type=F32)
                for c in range(PEER_CHUNKS)]
        ps = [w_cur[c] * jax.nn.gelu(acts[c]).astype(BF16) for c in range(PEER_CHUNKS)]
        rows = D_MODEL // PEER_ROW_BLOCKS
        for rb in range(PEER_ROW_BLOCKS):
            rsl = slice(rb * rows, (rb + 1) * rows)
            tot = None
            for c in range(PEER_CHUNKS):
                part = jnp.dot(vt_ref[c, rsl, :], ps[c], preferred_element_type=F32)
                tot = part if tot is None else tot + part
                for _ in range(per_subdot):
                    build_gates(j + 1, w_next, *pieces.pop(0))
            o_ref[rsl, :] += tot

    @pl.when(j % 2 == 0)
    def _():
        step(w_even, w_odd)

    @pl.when(j % 2 == 1)
    def _():
        step(w_odd, w_even)


def _peer(hnt, u, vt, r2, e2, jb, e1n):
    S = hnt.shape[1]
    tm, ec, nc = PEER_TM, PEER_EC, PEER_CHUNKS
    tokt = pl.BlockSpec((D_MODEL, tm), lambda i, j: (0, i))
    sel = pl.BlockSpec((PEER_HEADS, PEER_NKEYS, tm), lambda i, j: (0, 0, i))
    return pl.pallas_call(
        _peer_body,
        grid=(S // tm, PEER_EXPERTS // (ec * nc)),
        in_specs=[tokt,
                  pl.BlockSpec((ec * nc, D_MODEL), lambda i, j: (j, 0)),
                  pl.BlockSpec((nc, D_MODEL, ec), lambda i, j: (j, 0, 0)),
                  sel, sel, sel, sel],
        out_specs=tokt,
        out_shape=jax.ShapeDtypeStruct((D_MODEL, S), F32),
        scratch_shapes=[pltpu.VMEM((nc, ec, tm), BF16), pltpu.VMEM((nc, ec, tm), BF16)],
        compiler_params=_params(("parallel", "arbitrary")),
    )(hnt, u, vt, r2, e2, jb, e1n)


def _ple_body(h_ref, ft_ref, p_ref, gp_ref, wg_ref, bg_ref, wp_ref, gf_ref, o_ref):
    half = h_ref.shape[0] // 2
    for r in range(2):
        rsl = slice(r * half, (r + 1) * half)
        h = h_ref[rsl, :] + ft_ref[:, rsl].T
        hn = _rms(h, gp_ref[...]).astype(BF16)
        g = jax.nn.sigmoid(jnp.dot(hn, wg_ref[...], preferred_element_type=F32) + bg_ref[...])
        pp = jnp.dot(p_ref[rsl, :].astype(BF16), wp_ref[...], preferred_element_type=F32)
        o_ref[rsl, :] = _rms(h + g * pp, gf_ref[...])


def _ple(h, ft, p, gp, wg, bg, wp, gf):
    S = h.shape[0]
    tm = PLE_TM
    row = lambda n: pl.BlockSpec((tm, n), lambda i: (i, 0))
    vec = _resident((1, D_MODEL))
    return pl.pallas_call(
        _ple_body,
        grid=(S // tm,),
        in_specs=[row(D_MODEL), pl.BlockSpec((D_MODEL, tm), lambda i: (0, i)), row(PLE_DIM), vec,
                  _resident(wg.shape), vec, _resident(wp.shape), vec],
        out_specs=row(D_MODEL),
        out_shape=jax.ShapeDtypeStruct((S, D_MODEL), F32),
        compiler_params=_params(("parallel",)),
    )(h, ft, p, gp, wg, bg, wp, gf)


def _pair_blocks(w):
    w = w.reshape(8, 2, 64, 64)
    z = jnp.zeros((8, 64, 64), w.dtype)
    top = jnp.concatenate([w[:, 0], z], axis=-1)
    bot = jnp.concatenate([z, w[:, 1]], axis=-1)
    return jnp.concatenate([top, bot], axis=-2).astype(BF16)


def kernel(x, p, positions, mix_norm, w_in, conv_w, conv_b, rg_wa, rg_ba, rg_wx, rg_bx, rg_lambda,
           attn_out_norm, rnn_out_norm, w_out, ffn_norm, peer_wq, peer_subkeys, peer_u, peer_v,
           ple_norm, ple_w_gate, ple_b_gate, ple_proj, final_norm):
    B, S, D = x.shape
    assert (B, S, D) == (1, SEQ, D_MODEL) and S % SPAN == 0 and w_in.shape[0] == 1
    vec = lambda a: a.reshape(1, -1).astype(F32)

    half = ROPE_DIM // 2
    inv_freq = ROPE_THETA ** (-jnp.arange(half, dtype=F32) * 2.0 / ROPE_DIM)
    invf = jnp.zeros((1, HEAD_DIM), F32).at[0, :ROPE_DIM].set(jnp.tile(inv_freq, 2))

    q, k, v, xr, gate = _inproj(x[0], positions.reshape(S, 1), invf, vec(mix_norm[0]),
                                w_in[0].astype(BF16))
    attn = _attention(q, k, v)
    rnn = _rnn(xr, gate, conv_w[0], vec(conv_b[0]), _pair_blocks(rg_wa[0]), vec(rg_ba[0]),
               _pair_blocks(rg_wx[0]), vec(rg_bx[0]), vec(rg_lambda[0]), vec(rnn_out_norm[0]))
    sk = peer_subkeys[0].reshape(2 * PEER_HEADS, PEER_NKEYS, -1).astype(BF16)
    h1, hnt, st = _outproj(attn, rnn, x[0], vec(attn_out_norm[0]), w_out[0].astype(BF16),
                           vec(ffn_norm[0]), peer_wq[0].astype(BF16), sk)
    r2, e2, jb, e1n = _select(st)
    vt = peer_v[0].astype(BF16).reshape(PEER_EXPERTS // PEER_EC, PEER_EC, D).transpose(0, 2, 1)
    ft = _peer(hnt, peer_u[0].astype(BF16), vt, r2, e2, jb, e1n)
    out = _ple(h1, ft, p[0, 0], vec(ple_norm[0]), ple_w_gate[0].astype(BF16), vec(ple_b_gate[0]),
               ple_proj[0].astype(BF16), vec(final_norm))
    return out.reshape(B, S, D)
```

```python
import jax
import jax.numpy as jnp
from jax import lax
from jax.experimental import pallas as pl
from jax.experimental.pallas import tpu as pltpu

F32 = jnp.float32
BF16 = jnp.bfloat16

D_MODEL = 2048
SEQ = 8192
HEAD_DIM = 128
ATTN_WIDTH = 1024
N_HEADS = 8
ROPE_DIM = 32
ROPE_THETA = 500000.0
ATTN_BLOCK = 128
DILATIONS = (1, 4, 16)
SPAN = ATTN_BLOCK * 16
RNN_WIDTH = 1024
CONV_WIDTH = 4
RGLRU_C = 8.0
PEER_HEADS = 8
PEER_NKEYS = 128
PEER_EXPERTS = PEER_NKEYS * PEER_NKEYS
PEER_TOPK = 16
PLE_DIM = 256
EPS = 1e-6

VMEM_LIMIT = 56 * 1024 * 1024

IN_TM = 512
RNN_TM = 256
OUT_TM = 256
SEL_TB = 256
PEER_TM = 512
PEER_EC = 256
PEER_CHUNKS = 4
PEER_ROW_BLOCKS = 2
PLE_TM = 512


def _params(sem):
    return pltpu.CompilerParams(dimension_semantics=sem, vmem_limit_bytes=VMEM_LIMIT)


def _rms(x, g):
    return x * lax.rsqrt(jnp.mean(x * x, axis=-1, keepdims=True) + EPS) * g


def _resident(shape):
    nd = len(shape)
    return pl.BlockSpec(shape, lambda *_: (0,) * nd, pipeline_mode=pl.Buffered(1))


def _inproj_body(x_ref, pos_ref, invf_ref, g_ref, w_ref, q_ref, k_ref, v_ref, xr_ref, gate_ref):
    tm = x_ref.shape[0]
    xb = _rms(x_ref[...], g_ref[...]).astype(BF16)
    ang = pos_ref[...].astype(F32) * invf_ref[...]
    cosf = jnp.cos(ang)
    sinf = jnp.sin(ang)
    lane = lax.broadcasted_iota(jnp.int32, (tm, HEAD_DIM), 1)
    first_half = lane < ROPE_DIM // 2

    def rope(c):
        partner = jnp.where(first_half,
                            -pltpu.roll(c, HEAD_DIM - ROPE_DIM // 2, 1),
                            pltpu.roll(c, ROPE_DIM // 2, 1))
        return c * cosf + partner * sinf

    outs = (q_ref, k_ref, v_ref, xr_ref, gate_ref)
    for n, o_ref in enumerate(outs):
        y = jnp.dot(xb, w_ref[:, n * 1024:(n + 1) * 1024], preferred_element_type=F32)
        if n < 2:
            scale = HEAD_DIM ** -0.5 if n == 0 else 1.0
            for h in range(N_HEADS):
                sl = slice(h * HEAD_DIM, (h + 1) * HEAD_DIM)
                o_ref[:, sl] = rope(y[:, sl]) * scale
        else:
            o_ref[...] = y


def _inproj(x, pos, invf, g, w):
    S = x.shape[0]
    tm = IN_TM
    row = lambda n: pl.BlockSpec((tm, n), lambda i: (i, 0))
    out = jax.ShapeDtypeStruct((S, 1024), F32)
    return pl.pallas_call(
        _inproj_body,
        grid=(S // tm,),
        in_specs=[row(D_MODEL), row(1), _resident((1, HEAD_DIM)), _resident((1, D_MODEL)),
                  _resident(w.shape)],
        out_specs=[row(1024)] * 5,
        out_shape=[out] * 5,
        compiler_params=_params(("parallel",)),
    )(x, pos, invf, g, w)


def _attn_body(q_ref, k_ref, v_ref, o_ref, t4, q4, q16, k1, k4, k16, v1, v4, v16, op, ls):
    c = pl.program_id(1)
    kv = {1: (k1, v1), 4: (k4, v4), 16: (k16, v16)}

    @pl.when(c == 0)
    def _():
        for d in DILATIONS:
            for ref in kv[d]:
                ref[:, 0:ATTN_BLOCK, :] = jnp.zeros((d, ATTN_BLOCK, HEAD_DIM), BF16)

    @pl.when(c > 0)
    def _():
        for d in DILATIONS:
            rows = SPAN // d
            for ref in kv[d]:
                ref[:, 0:ATTN_BLOCK, :] = ref[:, rows:rows + ATTN_BLOCK, :]

    def deinterleave(x_ref, x1, x4, x16, off):
        if x1 is not None:
            x1[0, off:off + SPAN, :] = x_ref[...].astype(BF16)
        for r4 in range(4):
            t = x_ref[pl.ds(r4, SPAN // 4, stride=4), :]
            t4[r4] = t
            x4[r4, off:off + SPAN // 4, :] = t.astype(BF16)
        for r4 in range(4):
            for r2 in range(4):
                t = t4[r4, pl.ds(r2, SPAN // 16, stride=4), :]
                x16[4 * r2 + r4, off:off + SPAN // 16, :] = t.astype(BF16)

    deinterleave(q_ref, None, q4, q16, 0)
    deinterleave(k_ref, k1, k4, k16, ATTN_BLOCK)
    deinterleave(v_ref, v1, v4, v16, ATTN_BLOCK)

    qi = lax.broadcasted_iota(jnp.int32, (ATTN_BLOCK, 2 * ATTN_BLOCK), 0)
    kj = lax.broadcasted_iota(jnp.int32, (ATTN_BLOCK, 2 * ATTN_BLOCK), 1)
    dist = ATTN_BLOCK + qi - kj
    band = (dist >= 0) & (dist <= ATTN_BLOCK)
    bias_band = jnp.where(band, 0.0, -jnp.inf)
    first_key = jnp.where(c > 0, 0, ATTN_BLOCK)
    bias_first = jnp.where(band & (kj >= first_key), 0.0, -jnp.inf)

    def tile(qt, kk, vv, first_block):
        s = lax.dot_general(qt, kk, (((1,), (1,)), ((), ())), preferred_element_type=F32)
        s = s + (bias_first if first_block else bias_band)
        m = jnp.max(s, axis=-1, keepdims=True)
        e = jnp.exp(s - m)
        l = jnp.sum(e, axis=-1, keepdims=True)
        o = jnp.dot(e.astype(BF16), vv, preferred_element_type=F32)
        o = o / l
        lse = jnp.broadcast_to(m + jnp.log(l), (ATTN_BLOCK, HEAD_DIM))
        return o, lse

    def run_pattern(p_idx, d, q_tile):
        k_s, v_s = kv[d]
        nblk = SPAN // d // ATTN_BLOCK
        for r in range(d):
            for n in range(nblk):
                row0 = n * ATTN_BLOCK
                kk = k_s[r, row0:row0 + 2 * ATTN_BLOCK, :]
                vv = v_s[r, row0:row0 + 2 * ATTN_BLOCK, :]
                o, lse = tile(q_tile(r, row0), kk, vv, n == 0)
                start = n * (ATTN_BLOCK * d) + r
                dst = pl.ds(start, ATTN_BLOCK) if d == 1 else pl.ds(start, ATTN_BLOCK, stride=d)
                op[p_idx, dst, :] = o
                ls[p_idx, dst, :] = lse

    run_pattern(0, 1, lambda r, row0: q_ref[row0:row0 + ATTN_BLOCK, :].astype(BF16))
    run_pattern(1, 4, lambda r, row0: q4[r, row0:row0 + ATTN_BLOCK, :])
    run_pattern(2, 16, lambda r, row0: q16[r, row0:row0 + ATTN_BLOCK, :])

    l0, l1, l2 = ls[0], ls[1], ls[2]
    top = jnp.maximum(jnp.maximum(l0, l1), l2)
    w0 = jnp.exp(l0 - top)
    w1 = jnp.exp(l1 - top)
    w2 = jnp.exp(l2 - top)
    o_ref[...] = (w0 * op[0] + w1 * op[1] + w2 * op[2]) / (w0 + w1 + w2)


def _attention(q, k, v):
    S = q.shape[0]
    spec = pl.BlockSpec((SPAN, HEAD_DIM), lambda h, c: (c, h))
    slab = lambda d, off, dt: pltpu.VMEM((d, off + SPAN // d, HEAD_DIM), dt)
    scratch = [
        slab(4, 0, F32), slab(4, 0, BF16), slab(16, 0, BF16),
        slab(1, ATTN_BLOCK, BF16), slab(4, ATTN_BLOCK, BF16), slab(16, ATTN_BLOCK, BF16),
        slab(1, ATTN_BLOCK, BF16), slab(4, ATTN_BLOCK, BF16), slab(16, ATTN_BLOCK, BF16),
        pltpu.VMEM((3, SPAN, HEAD_DIM), F32), pltpu.VMEM((3, SPAN, HEAD_DIM), F32),
    ]
    return pl.pallas_call(
        _attn_body,
        grid=(N_HEADS, S // SPAN),
        in_specs=[spec] * 3,
        out_specs=spec,
        out_shape=jax.ShapeDtypeStruct((S, ATTN_WIDTH), F32),
        scratch_shapes=scratch,
        compiler_params=_params(("arbitrary", "arbitrary")),
    )(q, k, v)


def _rnn_body(xr_ref, gate_ref, cw_ref, cb_ref, wa_ref, ba_ref, wx_ref, bx_ref, lam_ref, g_ref,
              o_ref, xext, hcar, a_s, u_s):
    tm = xr_ref.shape[0]
    i = pl.program_id(0)

    @pl.when(i == 0)
    def _():
        xext[0:8, :] = jnp.zeros((8, RNN_WIDTH), F32)
        hcar[...] = jnp.zeros((8, RNN_WIDTH), F32)

    x = xr_ref[...]
    xext[8:8 + tm, :] = x
    xc = cb_ref[...]
    for tap in range(CONV_WIDTH):
        sh = CONV_WIDTH - 1 - tap
        xc = xc + xext[8 - sh:8 - sh + tm, :] * cw_ref[tap:tap + 1, :]
    xext[0:8, :] = x[tm - 8:tm, :]

    xcb = xc.astype(BF16)
    for j in range(RNN_WIDTH // 128):
        sl = slice(j * 128, (j + 1) * 128)
        blk = xcb[:, sl]
        r = jax.nn.sigmoid(jnp.dot(blk, wa_ref[j], preferred_element_type=F32) + ba_ref[:, sl])
        ig = jax.nn.sigmoid(jnp.dot(blk, wx_ref[j], preferred_element_type=F32) + bx_ref[:, sl])
        log_a = -RGLRU_C * r * jax.nn.softplus(-lam_ref[:, sl])
        a = jnp.exp(log_a)
        a_s[:, sl] = a
        u_s[:, sl] = jnp.sqrt(-jnp.tanh(log_a) * (a * a + 1.0)) * ig * xc[:, sl]

    row = lax.broadcasted_iota(jnp.int32, (8, RNN_WIDTH), 0)

    def scan(t, h):
        base = pl.multiple_of(t * 8, 8)
        A = a_s[pl.ds(base, 8), :]
        B = u_s[pl.ds(base, 8), :]
        for sft in (1, 2, 4):
            valid = row >= sft
            A_sh = pltpu.roll(A, sft, 0)
            B_sh = pltpu.roll(B, sft, 0)
            B = jnp.where(valid, A * B_sh + B, B)
            A = jnp.where(valid, A * A_sh, A)
        hs = A * h + B
        u_s[pl.ds(base, 8), :] = hs
        return jnp.broadcast_to(hs[7:8, :], (8, RNN_WIDTH))

    hcar[...] = lax.fori_loop(0, tm // 8, scan, hcar[...])

    rnn = u_s[...] * jax.nn.gelu(gate_ref[...])
    o_ref[...] = _rms(rnn, g_ref[...]).astype(BF16)


def _rnn(xr, gate, cw, cb, wa, ba, wx, bx, lam, g):
    S = xr.shape[0]
    tm = RNN_TM
    row = pl.BlockSpec((tm, RNN_WIDTH), lambda i: (i, 0))
    vec = _resident((1, RNN_WIDTH))
    return pl.pallas_call(
        _rnn_body,
        grid=(S // tm,),
        in_specs=[row, row, _resident(cw.shape), vec, _resident(wa.shape), vec,
                  _resident(wx.shape), vec, vec, vec],
        out_specs=row,
        out_shape=jax.ShapeDtypeStruct((S, RNN_WIDTH), BF16),
        scratch_shapes=[pltpu.VMEM((tm + 8, RNN_WIDTH), F32), pltpu.VMEM((8, RNN_WIDTH), F32),
                        pltpu.VMEM((tm, RNN_WIDTH), F32), pltpu.VMEM((tm, RNN_WIDTH), F32)],
        compiler_params=_params(("arbitrary",)),
    )(xr, gate, cw, cb, wa, ba, wx, bx, lam, g)


def _outproj_body(attn_ref, rnn_ref, x_ref, ga_ref, wo_ref, gf_ref, wq_ref, sk_ref,
                  h1_ref, hnt_ref, st_ref):
    an = _rms(attn_ref[...], ga_ref[...]).astype(BF16)
    mix = jnp.dot(an, wo_ref[0:ATTN_WIDTH, :], preferred_element_type=F32)
    mix = mix + jnp.dot(rnn_ref[...], wo_ref[ATTN_WIDTH:, :], preferred_element_type=F32)
    h1 = x_ref[...] + mix
    h1_ref[...] = h1
    hn32 = _rms(h1, gf_ref[...])
    hn = hn32.astype(BF16)
    hnt_ref[...] = hn32.T.astype(BF16)
    qp = jnp.dot(hn, wq_ref[...], preferred_element_type=F32).astype(BF16)
    for hc in range(2 * PEER_HEADS):
        st_ref[hc] = lax.dot_general(sk_ref[hc], qp[:, hc * 128:(hc + 1) * 128],
                                     (((1,), (1,)), ((), ())), preferred_element_type=F32)


def _outproj(attn, rnn, x, ga, wo, gf, wq, sk):
    S = x.shape[0]
    tm = OUT_TM
    row = lambda n: pl.BlockSpec((tm, n), lambda i: (i, 0))
    return pl.pallas_call(
        _outproj_body,
        grid=(S // tm,),
        in_specs=[row(ATTN_WIDTH), row(RNN_WIDTH), row(D_MODEL), _resident((1, ATTN_WIDTH)),
                  _resident(wo.shape), _resident((1, D_MODEL)), _resident(wq.shape),
                  _resident(sk.shape)],
        out_specs=[row(D_MODEL), pl.BlockSpec((D_MODEL, tm), lambda i: (0, i)),
                   pl.BlockSpec((2 * PEER_HEADS, PEER_NKEYS, tm), lambda i: (0, 0, i))],
        out_shape=[jax.ShapeDtypeStruct((S, D_MODEL), F32), jax.ShapeDtypeStruct((D_MODEL, S), BF16),
                   jax.ShapeDtypeStruct((2 * PEER_HEADS, PEER_NKEYS, S), F32)],
        compiler_params=_params(("parallel",)),
    )(attn, rnn, x, ga, wo, gf, wq, sk)


def _sort16_network():
    pairs = []

    def merge(lo, n, r):
        step = 2 * r
        if step < n:
            merge(lo, n, step)
            merge(lo + r, n, step)
            pairs.extend((i, i + r) for i in range(lo + r, lo + n - r, step))
        else:
            pairs.append((lo, lo + r))

    def sort(lo, n):
        if n > 1:
            sort(lo, n // 2)
            sort(lo + n // 2, n // 2)
            merge(lo, n, 1)

    sort(0, 16)
    return tuple(pairs)


_SORT16 = _sort16_network()
_SORT10 = tuple(p for p in _SORT16 if p[1] < 10)


def _select_body(st_ref, r2_ref, e2_ref, jb_ref, e1n_ref, tops, tau_s):
    tb = st_ref.shape[-1]
    K = PEER_TOPK
    NA = 2 * PEER_HEADS
    ninf = -jnp.inf
    row8 = lax.broadcasted_iota(jnp.int32, (8, tb), 0)

    def pop_max(v, depth):
        mx = jnp.max(v[0], axis=0, keepdims=True)
        where_hit = jnp.where(v[0] == mx, row8, 8)
        hit = where_hit == jnp.min(where_hit, axis=0, keepdims=True)
        for g in range(min(len(v), depth)):
            v[g] = jnp.where(hit, v[g + 1] if g + 1 < len(v) else ninf, v[g])
        return mx

    def extract(a, carry):
        v = [st_ref[a, 8 * g:8 * g + 8, :] for g in range(PEER_NKEYS // 8)]
        for i, j in _SORT16:
            v[i], v[j] = jnp.maximum(v[i], v[j]), jnp.minimum(v[i], v[j])
        for k in range(K):
            tops[a, k:k + 1, :] = pop_max(v, K - 1 - k)
        return carry

    lax.fori_loop(0, NA, extract, 0, unroll=4)

    def candidates(h):
        A = tops[2 * h]
        B = tops[2 * h + 1]
        B8 = B[0:8, :]
        limits = (16, 8, 5, 4, 3, 2, 2, 2)
        cands = [A[0:1, :] + B, A[1:2, :] + B8]
        for i in range(2, 8):
            cands.append(jnp.where(row8 < limits[i], A[i:i + 1, :] + B8, ninf))
        cands.append(A[8:16, :] + B[0:1, :])
        return A, B, cands

    def kth(h, carry):
        cands = candidates(h)[2]
        v = [cands[0][0:8, :], cands[0][8:16, :]] + cands[1:]
        for i, j in _SORT10:
            v[i], v[j] = jnp.maximum(v[i], v[j]), jnp.minimum(v[i], v[j])
        for k in range(K):
            tau = pop_max(v, K - 1 - k)
        tau_s[pl.ds(h, 1), :] = tau
        return carry

    lax.fori_loop(0, PEER_HEADS, kth, 0, unroll=4)

    for h in range(PEER_HEADS):
        A, B, cands = candidates(h)
        tau = tau_s[h:h + 1, :]
        cmax = A[0:1, :] + B[0:1, :]
        z = None
        for c in cands:
            zc = jnp.sum(jnp.where(c >= tau, jnp.exp(c - cmax), 0.0), axis=0, keepdims=True)
            z = zc if z is None else z + zc

        j_rows = [jnp.sum(jnp.where(cands[i] >= tau, 1.0, 0.0), axis=0, keepdims=True)
                  for i in range(8)]
        tail = jnp.where(cands[8] >= tau, 1.0, 0.0)
        for i in range(8):
            j_rows.append(tail[i:i + 1, :])

        s1 = st_ref[2 * h]
        s2 = st_ref[2 * h + 1]
        jb = jnp.zeros((PEER_NKEYS, tb), F32)
        r2 = jnp.full((PEER_NKEYS, tb), float(K), F32)
        for i in reversed(range(K)):
            jb = jnp.where(s1 == A[i:i + 1, :], j_rows[i], jb)
            r2 = jnp.where(s2 == B[i:i + 1, :], float(i), r2)
        jb_ref[h] = jb
        r2_ref[h] = r2.astype(BF16)
        e1n_ref[h] = jnp.exp(s1 - A[0:1, :]) / z
        e2_ref[h] = jnp.exp(s2 - B[0:1, :]).astype(BF16)


def _select(st):
    S = st.shape[-1]
    tb = SEL_TB
    out = lambda dt: jax.ShapeDtypeStruct((PEER_HEADS, PEER_NKEYS, S), dt)
    ospec = pl.BlockSpec((PEER_HEADS, PEER_NKEYS, tb), lambda i: (0, 0, i))
    return pl.pallas_call(
        _select_body,
        grid=(S // tb,),
        in_specs=[pl.BlockSpec((2 * PEER_HEADS, PEER_NKEYS, tb), lambda i: (0, 0, i))],
        out_specs=[ospec] * 4,
        out_shape=[out(BF16), out(BF16), out(F32), out(F32)],
        scratch_shapes=[pltpu.VMEM((2 * PEER_HEADS, PEER_TOPK, tb), F32),
                        pltpu.VMEM((PEER_HEADS, tb), F32)],
        compiler_params=_params(("parallel",)),
    )(st)


def _peer_body(hnt_ref, u_ref, vt_ref, r2_ref, e2_ref, jb_ref, e1n_ref, o_ref, w_even, w_odd):
    j = pl.program_id(1)
    ec = PEER_EC
    n_i1 = ec // PEER_NKEYS

    zero = jnp.zeros((), BF16)

    def build_gates(step, w_s, c, kk):
        i1 = jnp.minimum((step * PEER_CHUNKS + c) * n_i1 + kk, PEER_NKEYS - 1)
        w = None
        for h in range(PEER_HEADS):
            jb_b = jb_ref[h, pl.ds(i1, 1), :].astype(BF16)
            e1_b = e1n_ref[h, pl.ds(i1, 1), :].astype(BF16)
            t = jnp.where(r2_ref[h] < jb_b, e2_ref[h], zero) * e1_b
            w = t if w is None else w + t
        w_s[c, kk * PEER_NKEYS:(kk + 1) * PEER_NKEYS, :] = w

    @pl.when(j == 0)
    def _():
        o_ref[...] = jnp.zeros(o_ref.shape, F32)
        for c in range(PEER_CHUNKS):
            for kk in range(n_i1):
                build_gates(0, w_even, c, kk)

    def step(w_cur, w_next):
        pieces = [(c, kk) for c in range(PEER_CHUNKS) for kk in range(n_i1)]
        n_pieces, n_subdots = len(pieces), PEER_CHUNKS * PEER_ROW_BLOCKS

        acts = [jnp.dot(u_ref[c * ec:(c + 1) * ec, :], hnt_ref[...], preferred_element_type=F32)
                for c in range(PEER_CHUNKS)]
        ps = [w_cur[c] * jax.nn.gelu(acts[c]).astype(BF16) for c in range(PEER_CHUNKS)]
        rows = D_MODEL // PEER_ROW_BLOCKS
        for rb in range(PEER_ROW_BLOCKS):
            rsl = slice(rb * rows, (rb + 1) * rows)
            tot = None
            for c in range(PEER_CHUNKS):
                part = jnp.dot(vt_ref[c, rsl, :], ps[c], preferred_element_type=F32)
                tot = part if tot is None else tot + part
                done = rb * PEER_CHUNKS + c + 1
                while n_pieces - len(pieces) < done * n_pieces // n_subdots:
                    build_gates(j + 1, w_next, *pieces.pop(0))
            o_ref[rsl, :] += tot

    @pl.when(j % 2 == 0)
    def _():
        step(w_even, w_odd)

    @pl.when(j % 2 == 1)
    def _():
        step(w_odd, w_even)


def _peer(hnt, u, vt, r2, e2, jb, e1n):
    S = hnt.shape[1]
    tm, ec, nc = PEER_TM, PEER_EC, PEER_CHUNKS
    tokt = pl.BlockSpec((D_MODEL, tm), lambda i, j: (0, i))
    sel = pl.BlockSpec((PEER_HEADS, PEER_NKEYS, tm), lambda i, j: (0, 0, i))
    return pl.pallas_call(
        _peer_body,
        grid=(S // tm, PEER_EXPERTS // (ec * nc)),
        in_specs=[tokt,
                  pl.BlockSpec((ec * nc, D_MODEL), lambda i, j: (j, 0)),
                  pl.BlockSpec((nc, D_MODEL, ec), lambda i, j: (j, 0, 0)),
                  sel, sel, sel, sel],
        out_specs=tokt,
        out_shape=jax.ShapeDtypeStruct((D_MODEL, S), F32),
        scratch_shapes=[pltpu.VMEM((nc, ec, tm), BF16), pltpu.VMEM((nc, ec, tm), BF16)],
        compiler_params=_params(("parallel", "arbitrary")),
    )(hnt, u, vt, r2, e2, jb, e1n)


def _ple_body(h_ref, ft_ref, p_ref, gp_ref, wg_ref, bg_ref, wp_ref, gf_ref, o_ref):
    half = h_ref.shape[0] // 2
    for r in range(2):
        rsl = slice(r * half, (r + 1) * half)
        h = h_ref[rsl, :] + ft_ref[:, rsl].T
        hn = _rms(h, gp_ref[...]).astype(BF16)
        g = jax.nn.sigmoid(jnp.dot(hn, wg_ref[...], preferred_element_type=F32) + bg_ref[...])
        pp = jnp.dot(p_ref[rsl, :].astype(BF16), wp_ref[...], preferred_element_type=F32)
        o_ref[rsl, :] = _rms(h + g * pp, gf_ref[...])


def _ple(h, ft, p, gp, wg, bg, wp, gf):
    S = h.shape[0]
    tm = PLE_TM
    row = lambda n: pl.BlockSpec((tm, n), lambda i: (i, 0))
    vec = _resident((1, D_MODEL))
    return pl.pallas_call(
        _ple_body,
        grid=(S // tm,),
        in_specs=[row(D_MODEL), pl.BlockSpec((D_MODEL, tm), lambda i: (0, i)), row(PLE_DIM), vec,
                  _resident(wg.shape), vec, _resident(wp.shape), vec],
        out_specs=row(D_MODEL),
        out_shape=jax.ShapeDtypeStruct((S, D_MODEL), F32),
        compiler_params=_params(("parallel",)),
    )(h, ft, p, gp, wg, bg, wp, gf)


def _pair_blocks(w):
    w = w.reshape(8, 2, 64, 64)
    z = jnp.zeros((8, 64, 64), w.dtype)
    top = jnp.concatenate([w[:, 0], z], axis=-1)
    bot = jnp.concatenate([z, w[:, 1]], axis=-1)
    return jnp.concatenate([top, bot], axis=-2).astype(BF16)


def kernel(x, p, positions, mix_norm, w_in, conv_w, conv_b, rg_wa, rg_ba, rg_wx, rg_bx, rg_lambda,
           attn_out_norm, rnn_out_norm, w_out, ffn_norm, peer_wq, peer_subkeys, peer_u, peer_v,
           ple_norm, ple_w_gate, ple_b_gate, ple_proj, final_norm):
    B, S, D = x.shape
    assert (B, S, D) == (1, SEQ, D_MODEL) and S % SPAN == 0 and w_in.shape[0] == 1
    vec = lambda a: a.reshape(1, -1).astype(F32)

    half = ROPE_DIM // 2
    inv_freq = ROPE_THETA ** (-jnp.arange(half, dtype=F32) * 2.0 / ROPE_DIM)
    invf = jnp.zeros((1, HEAD_DIM), F32).at[0, :ROPE_DIM].set(jnp.tile(inv_freq, 2))

    q, k, v, xr, gate = _inproj(x[0], positions.reshape(S, 1), invf, vec(mix_norm[0]),
                                w_in[0].astype(BF16))
    attn = _attention(q, k, v)
    rnn = _rnn(xr, gate, conv_w[0], vec(conv_b[0]), _pair_blocks(rg_wa[0]), vec(rg_ba[0]),
               _pair_blocks(rg_wx[0]), vec(rg_bx[0]), vec(rg_lambda[0]), vec(rnn_out_norm[0]))
    sk = peer_subkeys[0].reshape(2 * PEER_HEADS, PEER_NKEYS, -1).astype(BF16)
    h1, hnt, st = _outproj(attn, rnn, x[0], vec(attn_out_norm[0]), w_out[0].astype(BF16),
                           vec(ffn_norm[0]), peer_wq[0].astype(BF16), sk)
    r2, e2, jb, e1n = _select(st)
    vt = peer_v[0].astype(BF16).reshape(PEER_EXPERTS // PEER_EC, PEER_EC, D).transpose(0, 2, 1)
    ft = _peer(hnt, peer_u[0].astype(BF16), vt, r2, e2, jb, e1n)
    out = _ple(h1, ft, p[0, 0], vec(ple_norm[0]), ple_w_gate[0].astype(BF16), vec(ple_b_gate[0]),
               ple_proj[0].astype(BF16), vec(final_norm))
    return out.reshape(B, S, D)
```

```python
import jax
import jax.numpy as jnp
from jax import lax
from jax.experimental import pallas as pl
from jax.experimental.pallas import tpu as pltpu

F32 = jnp.float32
BF16 = jnp.bfloat16

LANES = 128
SUBLANES = 8

D_MODEL = 2048
SEQ = 8192
HEAD_DIM = 128
ATTN_WIDTH = 1024
N_HEADS = 8
ROPE_DIM = 32
ROPE_THETA = 500000.0
ATTN_BLOCK = 128
DILATIONS = (1, 4, 16)
SPAN = ATTN_BLOCK * 16
RNN_WIDTH = 1024
CONV_WIDTH = 4
RGLRU_C = 8.0
PEER_HEADS = 8
PEER_NKEYS = 128
PEER_EXPERTS = PEER_NKEYS * PEER_NKEYS
PEER_TOPK = 16
PEER_KEY_DIM = 128
PLE_DIM = 256
PROJ_CHUNK = 1024
EPS = 1e-6

VMEM_LIMIT = 56 * 1024 * 1024

IN_TM = 512
RNN_TM = 256
OUT_TM = 256
SEL_TB = 256
PEER_TM = 512
PEER_EC = 256
PEER_CHUNKS = 4
PEER_ROW_BLOCKS = 2
PLE_TM = 512


def _params(sem):
    return pltpu.CompilerParams(dimension_semantics=sem, vmem_limit_bytes=VMEM_LIMIT)


def _rms(x, g):
    return x * lax.rsqrt(jnp.mean(x * x, axis=-1, keepdims=True) + EPS) * g


def _resident(shape):
    nd = len(shape)
    return pl.BlockSpec(shape, lambda *_: (0,) * nd, pipeline_mode=pl.Buffered(1))


def _inproj_body(x_ref, pos_ref, invf_ref, g_ref, w_ref, q_ref, k_ref, v_ref, xr_ref, gate_ref):
    tm = x_ref.shape[0]
    xb = _rms(x_ref[...], g_ref[...]).astype(BF16)
    ang = pos_ref[...].astype(F32) * invf_ref[...]
    cosf = jnp.cos(ang)
    sinf = jnp.sin(ang)
    lane = lax.broadcasted_iota(jnp.int32, (tm, HEAD_DIM), 1)
    first_half = lane < ROPE_DIM // 2

    def rope(c):
        partner = jnp.where(first_half,
                            -pltpu.roll(c, HEAD_DIM - ROPE_DIM // 2, 1),
                            pltpu.roll(c, ROPE_DIM // 2, 1))
        return c * cosf + partner * sinf

    outs = (q_ref, k_ref, v_ref, xr_ref, gate_ref)
    for n, o_ref in enumerate(outs):
        y = jnp.dot(xb, w_ref[:, n * PROJ_CHUNK:(n + 1) * PROJ_CHUNK], preferred_element_type=F32)
        if n < 2:
            scale = HEAD_DIM ** -0.5 if n == 0 else 1.0
            for h in range(N_HEADS):
                sl = slice(h * HEAD_DIM, (h + 1) * HEAD_DIM)
                o_ref[:, sl] = rope(y[:, sl]) * scale
        else:
            o_ref[...] = y


def _inproj(x, pos, invf, g, w):
    S = x.shape[0]
    tm = IN_TM
    row = lambda n: pl.BlockSpec((tm, n), lambda i: (i, 0))
    out = jax.ShapeDtypeStruct((S, PROJ_CHUNK), F32)
    return pl.pallas_call(
        _inproj_body,
        grid=(S // tm,),
        in_specs=[row(D_MODEL), row(1), _resident((1, HEAD_DIM)), _resident((1, D_MODEL)),
                  _resident(w.shape)],
        out_specs=[row(PROJ_CHUNK)] * 5,
        out_shape=[out] * 5,
        compiler_params=_params(("parallel",)),
    )(x, pos, invf, g, w)


def _attn_body(q_ref, k_ref, v_ref, o_ref, t4, q4, q16, k1, k4, k16, v1, v4, v16, op, ls):
    c = pl.program_id(1)
    kv = {1: (k1, v1), 4: (k4, v4), 16: (k16, v16)}

    @pl.when(c == 0)
    def _():
        for d in DILATIONS:
            for ref in kv[d]:
                ref[:, 0:ATTN_BLOCK, :] = jnp.zeros((d, ATTN_BLOCK, HEAD_DIM), BF16)

    @pl.when(c > 0)
    def _():
        for d in DILATIONS:
            rows = SPAN // d
            for ref in kv[d]:
                ref[:, 0:ATTN_BLOCK, :] = ref[:, rows:rows + ATTN_BLOCK, :]

    def deinterleave(x_ref, x1, x4, x16, off):
        if x1 is not None:
            x1[0, off:off + SPAN, :] = x_ref[...].astype(BF16)
        for r4 in range(4):
            t = x_ref[pl.ds(r4, SPAN // 4, stride=4), :]
            t4[r4] = t
            x4[r4, off:off + SPAN // 4, :] = t.astype(BF16)
        for r4 in range(4):
            for r2 in range(4):
                t = t4[r4, pl.ds(r2, SPAN // 16, stride=4), :]
                x16[4 * r2 + r4, off:off + SPAN // 16, :] = t.astype(BF16)

    deinterleave(q_ref, None, q4, q16, 0)
    deinterleave(k_ref, k1, k4, k16, ATTN_BLOCK)
    deinterleave(v_ref, v1, v4, v16, ATTN_BLOCK)

    qi = lax.broadcasted_iota(jnp.int32, (ATTN_BLOCK, 2 * ATTN_BLOCK), 0)
    kj = lax.broadcasted_iota(jnp.int32, (ATTN_BLOCK, 2 * ATTN_BLOCK), 1)
    dist = ATTN_BLOCK + qi - kj
    band = (dist >= 0) & (dist <= ATTN_BLOCK)
    bias_band = jnp.where(band, 0.0, -jnp.inf)
    first_key = jnp.where(c > 0, 0, ATTN_BLOCK)
    bias_first = jnp.where(band & (kj >= first_key), 0.0, -jnp.inf)

    def tile(qt, kk, vv, first_block):
        s = lax.dot_general(qt, kk, (((1,), (1,)), ((), ())), preferred_element_type=F32)
        s = s + (bias_first if first_block else bias_band)
        m = jnp.max(s, axis=-1, keepdims=True)
        e = jnp.exp(s - m)
        l = jnp.sum(e, axis=-1, keepdims=True)
        o = jnp.dot(e.astype(BF16), vv, preferred_element_type=F32)
        o = o / l
        lse = jnp.broadcast_to(m + jnp.log(l), (ATTN_BLOCK, HEAD_DIM))
        return o, lse

    def run_pattern(p_idx, d, q_tile):
        k_s, v_s = kv[d]
        nblk = SPAN // d // ATTN_BLOCK
        for r in range(d):
            for n in range(nblk):
                row0 = n * ATTN_BLOCK
                kk = k_s[r, row0:row0 + 2 * ATTN_BLOCK, :]
                vv = v_s[r, row0:row0 + 2 * ATTN_BLOCK, :]
                o, lse = tile(q_tile(r, row0), kk, vv, n == 0)
                start = n * (ATTN_BLOCK * d) + r
                dst = pl.ds(start, ATTN_BLOCK) if d == 1 else pl.ds(start, ATTN_BLOCK, stride=d)
                op[p_idx, dst, :] = o
                ls[p_idx, dst, :] = lse

    run_pattern(0, 1, lambda r, row0: q_ref[row0:row0 + ATTN_BLOCK, :].astype(BF16))
    run_pattern(1, 4, lambda r, row0: q4[r, row0:row0 + ATTN_BLOCK, :])
    run_pattern(2, 16, lambda r, row0: q16[r, row0:row0 + ATTN_BLOCK, :])

    l0, l1, l2 = ls[0], ls[1], ls[2]
    top = jnp.maximum(jnp.maximum(l0, l1), l2)
    w0 = jnp.exp(l0 - top)
    w1 = jnp.exp(l1 - top)
    w2 = jnp.exp(l2 - top)
    o_ref[...] = (w0 * op[0] + w1 * op[1] + w2 * op[2]) / (w0 + w1 + w2)


def _attention(q, k, v):
    S = q.shape[0]
    spec = pl.BlockSpec((SPAN, HEAD_DIM), lambda h, c: (c, h))
    slab = lambda d, off, dt: pltpu.VMEM((d, off + SPAN // d, HEAD_DIM), dt)
    scratch = [
        slab(4, 0, F32), slab(4, 0, BF16), slab(16, 0, BF16),
        slab(1, ATTN_BLOCK, BF16), slab(4, ATTN_BLOCK, BF16), slab(16, ATTN_BLOCK, BF16),
        slab(1, ATTN_BLOCK, BF16), slab(4, ATTN_BLOCK, BF16), slab(16, ATTN_BLOCK, BF16),
        pltpu.VMEM((3, SPAN, HEAD_DIM), F32), pltpu.VMEM((3, SPAN, HEAD_DIM), F32),
    ]
    return pl.pallas_call(
        _attn_body,
        grid=(N_HEADS, S // SPAN),
        in_specs=[spec] * 3,
        out_specs=spec,
        out_shape=jax.ShapeDtypeStruct((S, ATTN_WIDTH), F32),
        scratch_shapes=scratch,
        compiler_params=_params(("arbitrary", "arbitrary")),
    )(q, k, v)


def _rnn_body(xr_ref, gate_ref, cw_ref, cb_ref, wa_ref, ba_ref, wx_ref, bx_ref, lam_ref, g_ref,
              o_ref, xext, hcar, a_s, u_s):
    tm = xr_ref.shape[0]
    i = pl.program_id(0)

    T = SUBLANES

    @pl.when(i == 0)
    def _():
        xext[0:T, :] = jnp.zeros((T, RNN_WIDTH), F32)
        hcar[...] = jnp.zeros((T, RNN_WIDTH), F32)

    x = xr_ref[...]
    xext[T:T + tm, :] = x
    xc = cb_ref[...]
    for tap in range(CONV_WIDTH):
        sh = CONV_WIDTH - 1 - tap
        xc = xc + xext[T - sh:T - sh + tm, :] * cw_ref[tap:tap + 1, :]
    xext[0:T, :] = x[tm - T:tm, :]

    xcb = xc.astype(BF16)
    for j in range(RNN_WIDTH // LANES):
        sl = slice(j * LANES, (j + 1) * LANES)
        blk = xcb[:, sl]
        r = jax.nn.sigmoid(jnp.dot(blk, wa_ref[j], preferred_element_type=F32) + ba_ref[:, sl])
        ig = jax.nn.sigmoid(jnp.dot(blk, wx_ref[j], preferred_element_type=F32) + bx_ref[:, sl])
        log_a = -RGLRU_C * r * jax.nn.softplus(-lam_ref[:, sl])
        a = jnp.exp(log_a)
        a_s[:, sl] = a
        u_s[:, sl] = jnp.sqrt(-jnp.tanh(log_a) * (a * a + 1.0)) * ig * xc[:, sl]

    row = lax.broadcasted_iota(jnp.int32, (T, RNN_WIDTH), 0)

    def scan(t, h):
        base = pl.multiple_of(t * T, T)
        A = a_s[pl.ds(base, T), :]
        B = u_s[pl.ds(base, T), :]
        sft = 1
        while sft < T:
            valid = row >= sft
            A_sh = pltpu.roll(A, sft, 0)
            B_sh = pltpu.roll(B, sft, 0)
            B = jnp.where(valid, A * B_sh + B, B)
            A = jnp.where(valid, A * A_sh, A)
            sft *= 2
        hs = A * h + B
        u_s[pl.ds(base, T), :] = hs
        return jnp.broadcast_to(hs[T - 1:T, :], (T, RNN_WIDTH))

    hcar[...] = lax.fori_loop(0, tm // T, scan, hcar[...])

    rnn = u_s[...] * jax.nn.gelu(gate_ref[...])
    o_ref[...] = _rms(rnn, g_ref[...]).astype(BF16)


def _rnn(xr, gate, cw, cb, wa, ba, wx, bx, lam, g):
    S = xr.shape[0]
    tm = RNN_TM
    row = pl.BlockSpec((tm, RNN_WIDTH), lambda i: (i, 0))
    vec = _resident((1, RNN_WIDTH))
    return pl.pallas_call(
        _rnn_body,
        grid=(S // tm,),
        in_specs=[row, row, _resident(cw.shape), vec, _resident(wa.shape), vec,
                  _resident(wx.shape), vec, vec, vec],
        out_specs=row,
        out_shape=jax.ShapeDtypeStruct((S, RNN_WIDTH), BF16),
        scratch_shapes=[pltpu.VMEM((tm + SUBLANES, RNN_WIDTH), F32), pltpu.VMEM((SUBLANES, RNN_WIDTH), F32),
                        pltpu.VMEM((tm, RNN_WIDTH), F32), pltpu.VMEM((tm, RNN_WIDTH), F32)],
        compiler_params=_params(("arbitrary",)),
    )(xr, gate, cw, cb, wa, ba, wx, bx, lam, g)


def _outproj_body(attn_ref, rnn_ref, x_ref, ga_ref, wo_ref, gf_ref, wq_ref, sk_ref,
                  h1_ref, hnt_ref, st_ref):
    an = _rms(attn_ref[...], ga_ref[...]).astype(BF16)
    mix = jnp.dot(an, wo_ref[0:ATTN_WIDTH, :], preferred_element_type=F32)
    mix = mix + jnp.dot(rnn_ref[...], wo_ref[ATTN_WIDTH:, :], preferred_element_type=F32)
    h1 = x_ref[...] + mix
    h1_ref[...] = h1
    hn32 = _rms(h1, gf_ref[...])
    hn = hn32.astype(BF16)
    hnt_ref[...] = hn32.T.astype(BF16)
    qp = jnp.dot(hn, wq_ref[...], preferred_element_type=F32).astype(BF16)
    for hc in range(2 * PEER_HEADS):
        st_ref[hc] = lax.dot_general(sk_ref[hc], qp[:, hc * PEER_KEY_DIM:(hc + 1) * PEER_KEY_DIM],
                                     (((1,), (1,)), ((), ())), preferred_element_type=F32)


def _outproj(attn, rnn, x, ga, wo, gf, wq, sk):
    S = x.shape[0]
    tm = OUT_TM
    row = lambda n: pl.BlockSpec((tm, n), lambda i: (i, 0))
    return pl.pallas_call(
        _outproj_body,
        grid=(S // tm,),
        in_specs=[row(ATTN_WIDTH), row(RNN_WIDTH), row(D_MODEL), _resident((1, ATTN_WIDTH)),
                  _resident(wo.shape), _resident((1, D_MODEL)), _resident(wq.shape),
                  _resident(sk.shape)],
        out_specs=[row(D_MODEL), pl.BlockSpec((D_MODEL, tm), lambda i: (0, i)),
                   pl.BlockSpec((2 * PEER_HEADS, PEER_NKEYS, tm), lambda i: (0, 0, i))],
        out_shape=[jax.ShapeDtypeStruct((S, D_MODEL), F32), jax.ShapeDtypeStruct((D_MODEL, S), BF16),
                   jax.ShapeDtypeStruct((2 * PEER_HEADS, PEER_NKEYS, S), F32)],
        compiler_params=_params(("parallel",)),
    )(attn, rnn, x, ga, wo, gf, wq, sk)


def _sort16_network():
    pairs = []

    def merge(lo, n, r):
        step = 2 * r
        if step < n:
            merge(lo, n, step)
            merge(lo + r, n, step)
            pairs.extend((i, i + r) for i in range(lo + r, lo + n - r, step))
        else:
            pairs.append((lo, lo + r))

    def sort(lo, n):
        if n > 1:
            sort(lo, n // 2)
            sort(lo + n // 2, n // 2)
            merge(lo, n, 1)

    sort(0, 16)
    return tuple(pairs)


_SORT16 = _sort16_network()
_SORT10 = tuple(p for p in _SORT16 if p[1] < 10)


def _select_body(st_ref, r2_ref, e2_ref, jb_ref, e1n_ref, tops, tau_s):
    tb = st_ref.shape[-1]
    K = PEER_TOPK
    NA = 2 * PEER_HEADS
    ninf = -jnp.inf
    G = SUBLANES
    row8 = lax.broadcasted_iota(jnp.int32, (G, tb), 0)

    def pop_max(v, depth):
        mx = jnp.max(v[0], axis=0, keepdims=True)
        where_hit = jnp.where(v[0] == mx, row8, G)
        hit = where_hit == jnp.min(where_hit, axis=0, keepdims=True)
        for g in range(min(len(v), depth)):
            v[g] = jnp.where(hit, v[g + 1] if g + 1 < len(v) else ninf, v[g])
        return mx

    assert PEER_NKEYS // G == 16 and PEER_TOPK == 16

    def extract(a, carry):
        v = [st_ref[a, G * g:G * (g + 1), :] for g in range(PEER_NKEYS // G)]
        for i, j in _SORT16:
            v[i], v[j] = jnp.maximum(v[i], v[j]), jnp.minimum(v[i], v[j])
        for k in range(K):
            tops[a, k:k + 1, :] = pop_max(v, K - 1 - k)
        return carry

    lax.fori_loop(0, NA, extract, 0, unroll=4)

    def candidates(h):
        A = tops[2 * h]
        B = tops[2 * h + 1]
        B8 = B[0:8, :]
        limits = (16, 8, 5, 4, 3, 2, 2, 2)
        cands = [A[0:1, :] + B, A[1:2, :] + B8]
        for i in range(2, 8):
            cands.append(jnp.where(row8 < limits[i], A[i:i + 1, :] + B8, ninf))
        cands.append(A[8:16, :] + B[0:1, :])
        return A, B, cands

    def kth(h, carry):
        cands = candidates(h)[2]
        v = [cands[0][0:8, :], cands[0][8:16, :]] + cands[1:]
        for i, j in _SORT10:
            v[i], v[j] = jnp.maximum(v[i], v[j]), jnp.minimum(v[i], v[j])
        for k in range(K):
            tau = pop_max(v, K - 1 - k)
        tau_s[pl.ds(h, 1), :] = tau
        return carry

    lax.fori_loop(0, PEER_HEADS, kth, 0, unroll=4)

    for h in range(PEER_HEADS):
        A, B, cands = candidates(h)
        tau = tau_s[h:h + 1, :]
        cmax = A[0:1, :] + B[0:1, :]
        z = None
        for c in cands:
            zc = jnp.sum(jnp.where(c >= tau, jnp.exp(c - cmax), 0.0), axis=0, keepdims=True)
            z = zc if z is None else z + zc

        j_rows = [jnp.sum(jnp.where(cands[i] >= tau, 1.0, 0.0), axis=0, keepdims=True)
                  for i in range(8)]
        tail = jnp.where(cands[8] >= tau, 1.0, 0.0)
        for i in range(8):
            j_rows.append(tail[i:i + 1, :])

        s1 = st_ref[2 * h]
        s2 = st_ref[2 * h + 1]
        jb = jnp.zeros((PEER_NKEYS, tb), F32)
        r2 = jnp.full((PEER_NKEYS, tb), float(K), F32)
        for i in reversed(range(K)):
            jb = jnp.where(s1 == A[i:i + 1, :], j_rows[i], jb)
            r2 = jnp.where(s2 == B[i:i + 1, :], float(i), r2)
        jb_ref[h] = jb
        r2_ref[h] = r2.astype(BF16)
        e1n_ref[h] = jnp.exp(s1 - A[0:1, :]) / z
        e2_ref[h] = jnp.exp(s2 - B[0:1, :]).astype(BF16)


def _select(st):
    S = st.shape[-1]
    tb = SEL_TB
    out = lambda dt: jax.ShapeDtypeStruct((PEER_HEADS, PEER_NKEYS, S), dt)
    ospec = pl.BlockSpec((PEER_HEADS, PEER_NKEYS, tb), lambda i: (0, 0, i))
    return pl.pallas_call(
        _select_body,
        grid=(S // tb,),
        in_specs=[pl.BlockSpec((2 * PEER_HEADS, PEER_NKEYS, tb), lambda i: (0, 0, i))],
        out_specs=[ospec] * 4,
        out_shape=[out(BF16), out(BF16), out(F32), out(F32)],
        scratch_shapes=[pltpu.VMEM((2 * PEER_HEADS, PEER_TOPK, tb), F32),
                        pltpu.VMEM((PEER_HEADS, tb), F32)],
        compiler_params=_params(("parallel",)),
    )(st)


def _peer_body(hnt_ref, u_ref, vt_ref, r2_ref, e2_ref, jb_ref, e1n_ref, o_ref, w_even, w_odd):
    j = pl.program_id(1)
    ec = PEER_EC
    n_i1 = ec // PEER_NKEYS

    zero = jnp.zeros((), BF16)

    def build_gates(step, w_s, c, kk):
        i1 = jnp.minimum((step * PEER_CHUNKS + c) * n_i1 + kk, PEER_NKEYS - 1)
        w = None
        for h in range(PEER_HEADS):
            jb_b = jb_ref[h, pl.ds(i1, 1), :].astype(BF16)
            e1_b = e1n_ref[h, pl.ds(i1, 1), :].astype(BF16)
            t = jnp.where(r2_ref[h] < jb_b, e2_ref[h], zero) * e1_b
            w = t if w is None else w + t
        w_s[c, kk * PEER_NKEYS:(kk + 1) * PEER_NKEYS, :] = w

    @pl.when(j == 0)
    def _():
        o_ref[...] = jnp.zeros(o_ref.shape, F32)
        for c in range(PEER_CHUNKS):
            for kk in range(n_i1):
                build_gates(0, w_even, c, kk)

    def step(w_cur, w_next):
        pieces = [(c, kk) for c in range(PEER_CHUNKS) for kk in range(n_i1)]
        n_pieces, n_subdots = len(pieces), PEER_CHUNKS * PEER_ROW_BLOCKS

        acts = [jnp.dot(u_ref[c * ec:(c + 1) * ec, :], hnt_ref[...], preferred_element_type=F32)
                for c in range(PEER_CHUNKS)]
        ps = [w_cur[c] * jax.nn.gelu(acts[c]).astype(BF16) for c in range(PEER_CHUNKS)]
        rows = D_MODEL // PEER_ROW_BLOCKS
        for rb in range(PEER_ROW_BLOCKS):
            rsl = slice(rb * rows, (rb + 1) * rows)
            tot = None
            for c in range(PEER_CHUNKS):
                part = jnp.dot(vt_ref[c, rsl, :], ps[c], preferred_element_type=F32)
                tot = part if tot is None else tot + part
                done = rb * PEER_CHUNKS + c + 1
                while n_pieces - len(pieces) < done * n_pieces // n_subdots:
                    build_gates(j + 1, w_next, *pieces.pop(0))
            o_ref[rsl, :] += tot

    @pl.when(j % 2 == 0)
    def _():
        step(w_even, w_odd)

    @pl.when(j % 2 == 1)
    def _():
        step(w_odd, w_even)


def _peer(hnt, u, vt, r2, e2, jb, e1n):
    S = hnt.shape[1]
    tm, ec, nc = PEER_TM, PEER_EC, PEER_CHUNKS
    tokt = pl.BlockSpec((D_MODEL, tm), lambda i, j: (0, i))
    sel = pl.BlockSpec((PEER_HEADS, PEER_NKEYS, tm), lambda i, j: (0, 0, i))
    return pl.pallas_call(
        _peer_body,
        grid=(S // tm, PEER_EXPERTS // (ec * nc)),
        in_specs=[tokt,
                  pl.BlockSpec((ec * nc, D_MODEL), lambda i, j: (j, 0)),
                  pl.BlockSpec((nc, D_MODEL, ec), lambda i, j: (j, 0, 0)),
                  sel, sel, sel, sel],
        out_specs=tokt,
        out_shape=jax.ShapeDtypeStruct((D_MODEL, S), F32),
        scratch_shapes=[pltpu.VMEM((nc, ec, tm), BF16), pltpu.VMEM((nc, ec, tm), BF16)],
        compiler_params=_params(("parallel", "arbitrary")),
    )(hnt, u, vt, r2, e2, jb, e1n)


def _ple_body(h_ref, ft_ref, p_ref, gp_ref, wg_ref, bg_ref, wp_ref, gf_ref, o_ref):
    half = h_ref.shape[0] // 2
    for r in range(2):
        rsl = slice(r * half, (r + 1) * half)
        h = h_ref[rsl, :] + ft_ref[:, rsl].T
        hn = _rms(h, gp_ref[...]).astype(BF16)
        g = jax.nn.sigmoid(jnp.dot(hn, wg_ref[...], preferred_element_type=F32) + bg_ref[...])
        pp = jnp.dot(p_ref[rsl, :].astype(BF16), wp_ref[...], preferred_element_type=F32)
        o_ref[rsl, :] = _rms(h + g * pp, gf_ref[...])


def _ple(h, ft, p, gp, wg, bg, wp, gf):
    S = h.shape[0]
    tm = PLE_TM
    row = lambda n: pl.BlockSpec((tm, n), lambda i: (i, 0))
    vec = _resident((1, D_MODEL))
    return pl.pallas_call(
        _ple_body,
        grid=(S // tm,),
        in_specs=[row(D_MODEL), pl.BlockSpec((D_MODEL, tm), lambda i: (0, i)), row(PLE_DIM), vec,
                  _resident(wg.shape), vec, _resident(wp.shape), vec],
        out_specs=row(D_MODEL),
        out_shape=jax.ShapeDtypeStruct((S, D_MODEL), F32),
        compiler_params=_params(("parallel",)),
    )(h, ft, p, gp, wg, bg, wp, gf)


def _pair_blocks(w):
    w = w.reshape(8, 2, 64, 64)
    z = jnp.zeros((8, 64, 64), w.dtype)
    top = jnp.concatenate([w[:, 0], z], axis=-1)
    bot = jnp.concatenate([z, w[:, 1]], axis=-1)
    return jnp.concatenate([top, bot], axis=-2).astype(BF16)


def kernel(x, p, positions, mix_norm, w_in, conv_w, conv_b, rg_wa, rg_ba, rg_wx, rg_bx, rg_lambda,
           attn_out_norm, rnn_out_norm, w_out, ffn_norm, peer_wq, peer_subkeys, peer_u, peer_v,
           ple_norm, ple_w_gate, ple_b_gate, ple_proj, final_norm):
    B, S, D = x.shape
    assert (B, S, D) == (1, SEQ, D_MODEL) and S % SPAN == 0 and w_in.shape[0] == 1
    vec = lambda a: a.reshape(1, -1).astype(F32)

    half = ROPE_DIM // 2
    inv_freq = ROPE_THETA ** (-jnp.arange(half, dtype=F32) * 2.0 / ROPE_DIM)
    invf = jnp.zeros((1, HEAD_DIM), F32).at[0, :ROPE_DIM].set(jnp.tile(inv_freq, 2))

    q, k, v, xr, gate = _inproj(x[0], positions.reshape(S, 1), invf, vec(mix_norm[0]),
                                w_in[0].astype(BF16))
    attn = _attention(q, k, v)
    rnn = _rnn(xr, gate, conv_w[0], vec(conv_b[0]), _pair_blocks(rg_wa[0]), vec(rg_ba[0]),
               _pair_blocks(rg_wx[0]), vec(rg_bx[0]), vec(rg_lambda[0]), vec(rnn_out_norm[0]))
    sk = peer_subkeys[0].reshape(2 * PEER_HEADS, PEER_NKEYS, -1).astype(BF16)
    h1, hnt, st = _outproj(attn, rnn, x[0], vec(attn_out_norm[0]), w_out[0].astype(BF16),
                           vec(ffn_norm[0]), peer_wq[0].astype(BF16), sk)
    r2, e2, jb, e1n = _select(st)
    vt = peer_v[0].astype(BF16).reshape(PEER_EXPERTS // PEER_EC, PEER_EC, D).transpose(0, 2, 1)
    ft = _peer(hnt, peer_u[0].astype(BF16), vt, r2, e2, jb, e1n)
    out = _ple(h1, ft, p[0, 0], vec(ple_norm[0]), ple_w_gate[0].astype(BF16), vec(ple_b_gate[0]),
               ple_proj[0].astype(BF16), vec(final_norm))
    return out.reshape(B, S, D)
```

```python
import jax
import jax.numpy as jnp
from jax import lax
from jax.experimental import pallas as pl
from jax.experimental.pallas import tpu as pltpu

F32 = jnp.float32
BF16 = jnp.bfloat16

LANES = 128
SUBLANES = 8

D_MODEL = 2048
SEQ = 8192
HEAD_DIM = 128
ATTN_WIDTH = 1024
N_HEADS = 8
ROPE_DIM = 32
ROPE_THETA = 500000.0
ATTN_BLOCK = 128
DILATIONS = (1, 4, 16)
SPAN = ATTN_BLOCK * 16
RNN_WIDTH = 1024
CONV_WIDTH = 4
RGLRU_C = 8.0
PEER_HEADS = 8
PEER_NKEYS = 128
PEER_EXPERTS = PEER_NKEYS * PEER_NKEYS
PEER_TOPK = 16
PEER_KEY_DIM = 128
PLE_DIM = 256
PROJ_CHUNK = 1024
EPS = 1e-6

VMEM_LIMIT = 56 * 1024 * 1024

IN_TM = 512
RNN_TM = 256
OUT_TM = 256
SEL_TB = 256
PEER_TM = 512
PEER_EC = 256
PEER_CHUNKS = 4
PEER_ROW_BLOCKS = 2
PLE_TM = 512


def _params(sem):
    return pltpu.CompilerParams(dimension_semantics=sem, vmem_limit_bytes=VMEM_LIMIT)


def _rms(x, g):
    return x * lax.rsqrt(jnp.mean(x * x, axis=-1, keepdims=True) + EPS) * g


def _resident(shape):
    nd = len(shape)
    return pl.BlockSpec(shape, lambda *_: (0,) * nd, pipeline_mode=pl.Buffered(1))


def _inproj_body(x_ref, pos_ref, invf_ref, g_ref, w_ref, q_ref, k_ref, v_ref, xr_ref, gate_ref):
    tm = x_ref.shape[0]
    xb = _rms(x_ref[...], g_ref[...]).astype(BF16)
    ang = pos_ref[...].astype(F32) * invf_ref[...]
    cosf = jnp.cos(ang)
    sinf = jnp.sin(ang)
    lane = lax.broadcasted_iota(jnp.int32, (tm, HEAD_DIM), 1)
    first_half = lane < ROPE_DIM // 2

    def rope(c):
        partner = jnp.where(first_half,
                            -pltpu.roll(c, HEAD_DIM - ROPE_DIM // 2, 1),
                            pltpu.roll(c, ROPE_DIM // 2, 1))
        return c * cosf + partner * sinf

    outs = (q_ref, k_ref, v_ref, xr_ref, gate_ref)
    for n, o_ref in enumerate(outs):
        y = jnp.dot(xb, w_ref[:, n * PROJ_CHUNK:(n + 1) * PROJ_CHUNK], preferred_element_type=F32)
        if n < 2:
            scale = HEAD_DIM ** -0.5 if n == 0 else 1.0
            for h in range(N_HEADS):
                sl = slice(h * HEAD_DIM, (h + 1) * HEAD_DIM)
                o_ref[:, sl] = rope(y[:, sl]) * scale
        else:
            o_ref[...] = y


def _inproj(x, pos, invf, g, w):
    S = x.shape[0]
    tm = IN_TM
    row = lambda n: pl.BlockSpec((tm, n), lambda i: (i, 0))
    out = jax.ShapeDtypeStruct((S, PROJ_CHUNK), F32)
    return pl.pallas_call(
        _inproj_body,
        grid=(S // tm,),
        in_specs=[row(D_MODEL), row(1), _resident((1, HEAD_DIM)), _resident((1, D_MODEL)),
                  _resident(w.shape)],
        out_specs=[row(PROJ_CHUNK)] * 5,
        out_shape=[out] * 5,
        compiler_params=_params(("parallel",)),
    )(x, pos, invf, g, w)


def _attn_body(q_ref, k_ref, v_ref, o_ref, t4, q4, q16, k1, k4, k16, v1, v4, v16, op, ls):
    c = pl.program_id(1)
    kv = {1: (k1, v1), 4: (k4, v4), 16: (k16, v16)}

    @pl.when(c == 0)
    def _():
        for d in DILATIONS:
            for ref in kv[d]:
                ref[:, 0:ATTN_BLOCK, :] = jnp.zeros((d, ATTN_BLOCK, HEAD_DIM), BF16)

    @pl.when(c > 0)
    def _():
        for d in DILATIONS:
            rows = SPAN // d
            for ref in kv[d]:
                ref[:, 0:ATTN_BLOCK, :] = ref[:, rows:rows + ATTN_BLOCK, :]

    def deinterleave(x_ref, x1, x4, x16, off):
        if x1 is not None:
            x1[0, off:off + SPAN, :] = x_ref[...].astype(BF16)
        for r4 in range(4):
            t = x_ref[pl.ds(r4, SPAN // 4, stride=4), :]
            t4[r4] = t
            x4[r4, off:off + SPAN // 4, :] = t.astype(BF16)
        for r4 in range(4):
            for r2 in range(4):
                t = t4[r4, pl.ds(r2, SPAN // 16, stride=4), :]
                x16[4 * r2 + r4, off:off + SPAN // 16, :] = t.astype(BF16)

    deinterleave(q_ref, None, q4, q16, 0)
    deinterleave(k_ref, k1, k4, k16, ATTN_BLOCK)
    deinterleave(v_ref, v1, v4, v16, ATTN_BLOCK)

    qi = lax.broadcasted_iota(jnp.int32, (ATTN_BLOCK, 2 * ATTN_BLOCK), 0)
    kj = lax.broadcasted_iota(jnp.int32, (ATTN_BLOCK, 2 * ATTN_BLOCK), 1)
    dist = ATTN_BLOCK + qi - kj
    band = (dist >= 0) & (dist <= ATTN_BLOCK)
    bias_band = jnp.where(band, 0.0, -jnp.inf)
    first_key = jnp.where(c > 0, 0, ATTN_BLOCK)
    bias_first = jnp.where(band & (kj >= first_key), 0.0, -jnp.inf)

    def tile(qt, kk, vv, first_block):
        s = lax.dot_general(qt, kk, (((1,), (1,)), ((), ())), preferred_element_type=F32)
        s = s + (bias_first if first_block else bias_band)
        m = jnp.max(s, axis=-1, keepdims=True)
        e = jnp.exp(s - m)
        l = jnp.sum(e, axis=-1, keepdims=True)
        o = jnp.dot(e.astype(BF16), vv, preferred_element_type=F32)
        o = o / l
        lse = jnp.broadcast_to(m + jnp.log(l), (ATTN_BLOCK, HEAD_DIM))
        return o, lse

    def run_pattern(p_idx, d, q_tile):
        k_s, v_s = kv[d]
        nblk = SPAN // d // ATTN_BLOCK
        for r in range(d):
            for n in range(nblk):
                row0 = n * ATTN_BLOCK
                kk = k_s[r, row0:row0 + 2 * ATTN_BLOCK, :]
                vv = v_s[r, row0:row0 + 2 * ATTN_BLOCK, :]
                o, lse = tile(q_tile(r, row0), kk, vv, n == 0)
                start = n * (ATTN_BLOCK * d) + r
                dst = pl.ds(start, ATTN_BLOCK) if d == 1 else pl.ds(start, ATTN_BLOCK, stride=d)
                op[p_idx, dst, :] = o
                ls[p_idx, dst, :] = lse

    run_pattern(0, 1, lambda r, row0: q_ref[row0:row0 + ATTN_BLOCK, :].astype(BF16))
    run_pattern(1, 4, lambda r, row0: q4[r, row0:row0 + ATTN_BLOCK, :])
    run_pattern(2, 16, lambda r, row0: q16[r, row0:row0 + ATTN_BLOCK, :])

    l0, l1, l2 = ls[0], ls[1], ls[2]
    top = jnp.maximum(jnp.maximum(l0, l1), l2)
    w0 = jnp.exp(l0 - top)
    w1 = jnp.exp(l1 - top)
    w2 = jnp.exp(l2 - top)
    o_ref[...] = (w0 * op[0] + w1 * op[1] + w2 * op[2]) / (w0 + w1 + w2)


def _attention(q, k, v):
    S = q.shape[0]
    spec = pl.BlockSpec((SPAN, HEAD_DIM), lambda h, c: (c, h))
    slab = lambda d, off, dt: pltpu.VMEM((d, off + SPAN // d, HEAD_DIM), dt)
    scratch = [
        slab(4, 0, F32), slab(4, 0, BF16), slab(16, 0, BF16),
        slab(1, ATTN_BLOCK, BF16), slab(4, ATTN_BLOCK, BF16), slab(16, ATTN_BLOCK, BF16),
        slab(1, ATTN_BLOCK, BF16), slab(4, ATTN_BLOCK, BF16), slab(16, ATTN_BLOCK, BF16),
        pltpu.VMEM((3, SPAN, HEAD_DIM), F32), pltpu.VMEM((3, SPAN, HEAD_DIM), F32),
    ]
    return pl.pallas_call(
        _attn_body,
        grid=(N_HEADS, S // SPAN),
        in_specs=[spec] * 3,
        out_specs=spec,
        out_shape=jax.ShapeDtypeStruct((S, ATTN_WIDTH), F32),
        scratch_shapes=scratch,
        compiler_params=_params(("arbitrary", "arbitrary")),
    )(q, k, v)


def _rnn_body(xr_ref, gate_ref, cw_ref, cb_ref, wa_ref, ba_ref, wx_ref, bx_ref, lam_ref, g_ref,
              o_ref, xext, hcar, a_s, u_s):
    tm = xr_ref.shape[0]
    i = pl.program_id(0)

    T = SUBLANES

    @pl.when(i == 0)
    def _():
        xext[0:T, :] = jnp.zeros((T, RNN_WIDTH), F32)
        hcar[...] = jnp.zeros((T, RNN_WIDTH), F32)

    x = xr_ref[...]
    xext[T:T + tm, :] = x
    xc = cb_ref[...]
    for tap in range(CONV_WIDTH):
        sh = CONV_WIDTH - 1 - tap
        xc = xc + xext[T - sh:T - sh + tm, :] * cw_ref[tap:tap + 1, :]
    xext[0:T, :] = x[tm - T:tm, :]

    xcb = xc.astype(BF16)
    for j in range(RNN_WIDTH // LANES):
        sl = slice(j * LANES, (j + 1) * LANES)
        blk = xcb[:, sl]
        r = jax.nn.sigmoid(jnp.dot(blk, wa_ref[j], preferred_element_type=F32) + ba_ref[:, sl])
        ig = jax.nn.sigmoid(jnp.dot(blk, wx_ref[j], preferred_element_type=F32) + bx_ref[:, sl])
        log_a = -RGLRU_C * r * jax.nn.softplus(-lam_ref[:, sl])
        a = jnp.exp(log_a)
        a_s[:, sl] = a
        u_s[:, sl] = jnp.sqrt(-jnp.tanh(log_a) * (a * a + 1.0)) * ig * xc[:, sl]

    row = lax.broadcasted_iota(jnp.int32, (T, RNN_WIDTH), 0)

    def scan(t, h):
        base = pl.multiple_of(t * T, T)
        A = a_s[pl.ds(base, T), :]
        B = u_s[pl.ds(base, T), :]
        sft = 1
        while sft < T:
            valid = row >= sft
            A_sh = pltpu.roll(A, sft, 0)
            B_sh = pltpu.roll(B, sft, 0)
            B = jnp.where(valid, A * B_sh + B, B)
            A = jnp.where(valid, A * A_sh, A)
            sft *= 2
        hs = A * h + B
        u_s[pl.ds(base, T), :] = hs
        return jnp.broadcast_to(hs[T - 1:T, :], (T, RNN_WIDTH))

    hcar[...] = lax.fori_loop(0, tm // T, scan, hcar[...])

    rnn = u_s[...] * jax.nn.gelu(gate_ref[...])
    o_ref[...] = _rms(rnn, g_ref[...]).astype(BF16)


def _rnn(xr, gate, cw, cb, wa, ba, wx, bx, lam, g):
    S = xr.shape[0]
    tm = RNN_TM
    row = pl.BlockSpec((tm, RNN_WIDTH), lambda i: (i, 0))
    vec = _resident((1, RNN_WIDTH))
    return pl.pallas_call(
        _rnn_body,
        grid=(S // tm,),
        in_specs=[row, row, _resident(cw.shape), vec, _resident(wa.shape), vec,
                  _resident(wx.shape), vec, vec, vec],
        out_specs=row,
        out_shape=jax.ShapeDtypeStruct((S, RNN_WIDTH), BF16),
        scratch_shapes=[pltpu.VMEM((tm + SUBLANES, RNN_WIDTH), F32), pltpu.VMEM((SUBLANES, RNN_WIDTH), F32),
                        pltpu.VMEM((tm, RNN_WIDTH), F32), pltpu.VMEM((tm, RNN_WIDTH), F32)],
        compiler_params=_params(("arbitrary",)),
    )(xr, gate, cw, cb, wa, ba, wx, bx, lam, g)


def _outproj_body(attn_ref, rnn_ref, x_ref, ga_ref, wo_ref, gf_ref, wq_ref, sk_ref,
                  h1_ref, hnt_ref, st_ref):
    an = _rms(attn_ref[...], ga_ref[...]).astype(BF16)
    mix = jnp.dot(an, wo_ref[0:ATTN_WIDTH, :], preferred_element_type=F32)
    mix = mix + jnp.dot(rnn_ref[...], wo_ref[ATTN_WIDTH:, :], preferred_element_type=F32)
    h1 = x_ref[...] + mix
    h1_ref[...] = h1
    hn32 = _rms(h1, gf_ref[...])
    hn = hn32.astype(BF16)
    hnt_ref[...] = hn32.T.astype(BF16)
    qp = jnp.dot(hn, wq_ref[...], preferred_element_type=F32).astype(BF16)
    for hc in range(2 * PEER_HEADS):
        st_ref[hc] = lax.dot_general(sk_ref[hc], qp[:, hc * PEER_KEY_DIM:(hc + 1) * PEER_KEY_DIM],
                                     (((1,), (1,)), ((), ())), preferred_element_type=F32)


def _outproj(attn, rnn, x, ga, wo, gf, wq, sk):
    S = x.shape[0]
    tm = OUT_TM
    row = lambda n: pl.BlockSpec((tm, n), lambda i: (i, 0))
    return pl.pallas_call(
        _outproj_body,
        grid=(S // tm,),
        in_specs=[row(ATTN_WIDTH), row(RNN_WIDTH), row(D_MODEL), _resident((1, ATTN_WIDTH)),
                  _resident(wo.shape), _resident((1, D_MODEL)), _resident(wq.shape),
                  _resident(sk.shape)],
        out_specs=[row(D_MODEL), pl.BlockSpec((D_MODEL, tm), lambda i: (0, i)),
                   pl.BlockSpec((2 * PEER_HEADS, PEER_NKEYS, tm), lambda i: (0, 0, i))],
        out_shape=[jax.ShapeDtypeStruct((S, D_MODEL), F32), jax.ShapeDtypeStruct((D_MODEL, S), BF16),
                   jax.ShapeDtypeStruct((2 * PEER_HEADS, PEER_NKEYS, S), F32)],
        compiler_params=_params(("parallel",)),
    )(attn, rnn, x, ga, wo, gf, wq, sk)


def _sort16_network():
    pairs = []

    def merge(lo, n, r):
        step = 2 * r
        if step < n:
            merge(lo, n, step)
            merge(lo + r, n, step)
            pairs.extend((i, i + r) for i in range(lo + r, lo + n - r, step))
        else:
            pairs.append((lo, lo + r))

    def sort(lo, n):
        if n > 1:
            sort(lo, n // 2)
            sort(lo + n // 2, n // 2)
            merge(lo, n, 1)

    sort(0, 16)
    return tuple(pairs)


_SORT16 = _sort16_network()
_SORT10 = tuple(p for p in _SORT16 if p[1] < 10)


def _select_body(st_ref, r2_ref, e2_ref, jb_ref, e1n_ref, tops, tau_s):
    tb = st_ref.shape[-1]
    K = PEER_TOPK
    NA = 2 * PEER_HEADS
    ninf = -jnp.inf
    G = SUBLANES
    row8 = lax.broadcasted_iota(jnp.int32, (G, tb), 0)

    def pop_max(v, depth):
        mx = jnp.max(v[0], axis=0, keepdims=True)
        where_hit = jnp.where(v[0] == mx, row8, G)
        hit = where_hit == jnp.min(where_hit, axis=0, keepdims=True)
        for g in range(min(len(v), depth)):
            v[g] = jnp.where(hit, v[g + 1] if g + 1 < len(v) else ninf, v[g])
        return mx

    assert PEER_NKEYS // G == 16 and PEER_TOPK == 16

    def extract(a, carry):
        v = [st_ref[a, G * g:G * (g + 1), :] for g in range(PEER_NKEYS // G)]
        for i, j in _SORT16:
            v[i], v[j] = jnp.maximum(v[i], v[j]), jnp.minimum(v[i], v[j])
        for k in range(K):
            tops[a, k:k + 1, :] = pop_max(v, K - 1 - k)
        return carry

    lax.fori_loop(0, NA, extract, 0, unroll=4)

    def candidates(h):
        A = tops[2 * h]
        B = tops[2 * h + 1]
        B8 = B[0:8, :]
        limits = (16, 8, 5, 4, 3, 2, 2, 2)
        cands = [A[0:1, :] + B, A[1:2, :] + B8]
        for i in range(2, 8):
            cands.append(jnp.where(row8 < limits[i], A[i:i + 1, :] + B8, ninf))
        cands.append(A[8:16, :] + B[0:1, :])
        return A, B, cands

    def kth(h, carry):
        cands = candidates(h)[2]
        v = [cands[0][0:8, :], cands[0][8:16, :]] + cands[1:]
        for i, j in _SORT10:
            v[i], v[j] = jnp.maximum(v[i], v[j]), jnp.minimum(v[i], v[j])
        for k in range(K):
            tau = pop_max(v, K - 1 - k)
        tau_s[pl.ds(h, 1), :] = tau
        return carry

    lax.fori_loop(0, PEER_HEADS, kth, 0, unroll=8)

    for h in range(PEER_HEADS):
        A, B, cands = candidates(h)
        tau = tau_s[h:h + 1, :]
        cmax = A[0:1, :] + B[0:1, :]
        z = None
        for c in cands:
            zc = jnp.sum(jnp.where(c >= tau, jnp.exp(c - cmax), 0.0), axis=0, keepdims=True)
            z = zc if z is None else z + zc

        j_rows = [jnp.sum(jnp.where(cands[i] >= tau, 1.0, 0.0), axis=0, keepdims=True)
                  for i in range(8)]
        tail = jnp.where(cands[8] >= tau, 1.0, 0.0)
        for i in range(8):
            j_rows.append(tail[i:i + 1, :])

        s1 = st_ref[2 * h]
        s2 = st_ref[2 * h + 1]
        jb = jnp.zeros((PEER_NKEYS, tb), F32)
        r2 = jnp.full((PEER_NKEYS, tb), float(K), F32)
        for i in reversed(range(K)):
            jb = jnp.where(s1 == A[i:i + 1, :], j_rows[i], jb)
            r2 = jnp.where(s2 == B[i:i + 1, :], float(i), r2)
        jb_ref[h] = jb
        r2_ref[h] = r2.astype(BF16)
        e1n_ref[h] = jnp.exp(s1 - A[0:1, :]) / z
        e2_ref[h] = jnp.exp(s2 - B[0:1, :]).astype(BF16)


def _select(st):
    S = st.shape[-1]
    tb = SEL_TB
    out = lambda dt: jax.ShapeDtypeStruct((PEER_HEADS, PEER_NKEYS, S), dt)
    ospec = pl.BlockSpec((PEER_HEADS, PEER_NKEYS, tb), lambda i: (0, 0, i))
    return pl.pallas_call(
        _select_body,
        grid=(S // tb,),
        in_specs=[pl.BlockSpec((2 * PEER_HEADS, PEER_NKEYS, tb), lambda i: (0, 0, i))],
        out_specs=[ospec] * 4,
        out_shape=[out(BF16), out(BF16), out(F32), out(F32)],
        scratch_shapes=[pltpu.VMEM((2 * PEER_HEADS, PEER_TOPK, tb), F32),
                        pltpu.VMEM((PEER_HEADS, tb), F32)],
        compiler_params=_params(("parallel",)),
    )(st)


def _peer_body(hnt_ref, u_ref, vt_ref, r2_ref, e2_ref, jb_ref, e1n_ref, o_ref, w_even, w_odd):
    j = pl.program_id(1)
    ec = PEER_EC
    n_i1 = ec // PEER_NKEYS

    zero = jnp.zeros((), BF16)

    def build_gates(step, w_s, c, s):
        half = hnt_ref.shape[1] // 2
        ssl = slice(s * half, (s + 1) * half)
        i1_0 = jnp.minimum((step * PEER_CHUNKS + c) * n_i1, PEER_NKEYS - n_i1)
        acc = [None] * n_i1
        for h in range(PEER_HEADS):
            r2t = r2_ref[h, :, ssl]
            e2t = e2_ref[h, :, ssl]
            for kk in range(n_i1):
                jb_b = jb_ref[h, pl.ds(i1_0 + kk, 1), :][:, ssl].astype(BF16)
                e1_b = e1n_ref[h, pl.ds(i1_0 + kk, 1), :][:, ssl].astype(BF16)
                t = jnp.where(r2t < jb_b, e2t, zero) * e1_b
                acc[kk] = t if acc[kk] is None else acc[kk] + t
        for kk in range(n_i1):
            w_s[c, kk * PEER_NKEYS:(kk + 1) * PEER_NKEYS, ssl] = acc[kk]

    @pl.when(j == 0)
    def _():
        o_ref[...] = jnp.zeros(o_ref.shape, F32)
        for c in range(PEER_CHUNKS):
            for s in range(2):
                build_gates(0, w_even, c, s)

    def step(w_cur, w_next):
        pieces = [(c, s) for c in range(PEER_CHUNKS) for s in range(2)]
        n_pieces, n_subdots = len(pieces), PEER_CHUNKS * PEER_ROW_BLOCKS

        acts = [jnp.dot(u_ref[c * ec:(c + 1) * ec, :], hnt_ref[...], preferred_element_type=F32)
                for c in range(PEER_CHUNKS)]
        ps = [w_cur[c] * jax.nn.gelu(acts[c]).astype(BF16) for c in range(PEER_CHUNKS)]
        rows = D_MODEL // PEER_ROW_BLOCKS
        for rb in range(PEER_ROW_BLOCKS):
            rsl = slice(rb * rows, (rb + 1) * rows)
            tot = None
            for c in range(PEER_CHUNKS):
                part = jnp.dot(vt_ref[c, rsl, :], ps[c], preferred_element_type=F32)
                tot = part if tot is None else tot + part
                done = rb * PEER_CHUNKS + c + 1
                while n_pieces - len(pieces) < done * n_pieces // n_subdots:
                    build_gates(j + 1, w_next, *pieces.pop(0))
            o_ref[rsl, :] += tot

    @pl.when(j % 2 == 0)
    def _():
        step(w_even, w_odd)

    @pl.when(j % 2 == 1)
    def _():
        step(w_odd, w_even)


def _peer(hnt, u, vt, r2, e2, jb, e1n):
    S = hnt.shape[1]
    tm, ec, nc = PEER_TM, PEER_EC, PEER_CHUNKS
    tokt = pl.BlockSpec((D_MODEL, tm), lambda i, j: (0, i))
    sel = pl.BlockSpec((PEER_HEADS, PEER_NKEYS, tm), lambda i, j: (0, 0, i))
    return pl.pallas_call(
        _peer_body,
        grid=(S // tm, PEER_EXPERTS // (ec * nc)),
        in_specs=[tokt,
                  pl.BlockSpec((ec * nc, D_MODEL), lambda i, j: (j, 0)),
                  pl.BlockSpec((nc, D_MODEL, ec), lambda i, j: (j, 0, 0)),
                  sel, sel, sel, sel],
        out_specs=tokt,
        out_shape=jax.ShapeDtypeStruct((D_MODEL, S), F32),
        scratch_shapes=[pltpu.VMEM((nc, ec, tm), BF16), pltpu.VMEM((nc, ec, tm), BF16)],
        compiler_params=_params(("parallel", "arbitrary")),
    )(hnt, u, vt, r2, e2, jb, e1n)


def _ple_body(h_ref, ft_ref, p_ref, gp_ref, wg_ref, bg_ref, wp_ref, gf_ref, o_ref):
    half = h_ref.shape[0] // 2
    for r in range(2):
        rsl = slice(r * half, (r + 1) * half)
        h = h_ref[rsl, :] + ft_ref[:, rsl].T
        hn = _rms(h, gp_ref[...]).astype(BF16)
        g = jax.nn.sigmoid(jnp.dot(hn, wg_ref[...], preferred_element_type=F32) + bg_ref[...])
        pp = jnp.dot(p_ref[rsl, :].astype(BF16), wp_ref[...], preferred_element_type=F32)
        o_ref[rsl, :] = _rms(h + g * pp, gf_ref[...])


def _ple(h, ft, p, gp, wg, bg, wp, gf):
    S = h.shape[0]
    tm = PLE_TM
    row = lambda n: pl.BlockSpec((tm, n), lambda i: (i, 0))
    vec = _resident((1, D_MODEL))
    return pl.pallas_call(
        _ple_body,
        grid=(S // tm,),
        in_specs=[row(D_MODEL), pl.BlockSpec((D_MODEL, tm), lambda i: (0, i)), row(PLE_DIM), vec,
                  _resident(wg.shape), vec, _resident(wp.shape), vec],
        out_specs=row(D_MODEL),
        out_shape=jax.ShapeDtypeStruct((S, D_MODEL), F32),
        compiler_params=_params(("parallel",)),
    )(h, ft, p, gp, wg, bg, wp, gf)


def _pair_blocks(w):
    w = w.reshape(8, 2, 64, 64)
    z = jnp.zeros((8, 64, 64), w.dtype)
    top = jnp.concatenate([w[:, 0], z], axis=-1)
    bot = jnp.concatenate([z, w[:, 1]], axis=-1)
    return jnp.concatenate([top, bot], axis=-2).astype(BF16)


def kernel(x, p, positions, mix_norm, w_in, conv_w, conv_b, rg_wa, rg_ba, rg_wx, rg_bx, rg_lambda,
           attn_out_norm, rnn_out_norm, w_out, ffn_norm, peer_wq, peer_subkeys, peer_u, peer_v,
           ple_norm, ple_w_gate, ple_b_gate, ple_proj, final_norm):
    B, S, D = x.shape
    assert (B, S, D) == (1, SEQ, D_MODEL) and S % SPAN == 0 and w_in.shape[0] == 1
    vec = lambda a: a.reshape(1, -1).astype(F32)

    half = ROPE_DIM // 2
    inv_freq = ROPE_THETA ** (-jnp.arange(half, dtype=F32) * 2.0 / ROPE_DIM)
    invf = jnp.zeros((1, HEAD_DIM), F32).at[0, :ROPE_DIM].set(jnp.tile(inv_freq, 2))

    q, k, v, xr, gate = _inproj(x[0], positions.reshape(S, 1), invf, vec(mix_norm[0]),
                                w_in[0].astype(BF16))
    attn = _attention(q, k, v)
    rnn = _rnn(xr, gate, conv_w[0], vec(conv_b[0]), _pair_blocks(rg_wa[0]), vec(rg_ba[0]),
               _pair_blocks(rg_wx[0]), vec(rg_bx[0]), vec(rg_lambda[0]), vec(rnn_out_norm[0]))
    sk = peer_subkeys[0].reshape(2 * PEER_HEADS, PEER_NKEYS, -1).astype(BF16)
    h1, hnt, st = _outproj(attn, rnn, x[0], vec(attn_out_norm[0]), w_out[0].astype(BF16),
                           vec(ffn_norm[0]), peer_wq[0].astype(BF16), sk)
    r2, e2, jb, e1n = _select(st)
    vt = peer_v[0].astype(BF16).reshape(PEER_EXPERTS // PEER_EC, PEER_EC, D).transpose(0, 2, 1)
    ft = _peer(hnt, peer_u[0].astype(BF16), vt, r2, e2, jb, e1n)
    out = _ple(h1, ft, p[0, 0], vec(ple_norm[0]), ple_w_gate[0].astype(BF16), vec(ple_b_gate[0]),
               ple_proj[0].astype(BF16), vec(final_norm))
    return out.reshape(B, S, D)
```

```python
import jax
import jax.numpy as jnp
from jax import lax
from jax.experimental import pallas as pl
from jax.experimental.pallas import tpu as pltpu

F32 = jnp.float32
BF16 = jnp.bfloat16

LANES = 128
SUBLANES = 8

D_MODEL = 2048
SEQ = 8192
HEAD_DIM = 128
ATTN_WIDTH = 1024
N_HEADS = 8
ROPE_DIM = 32
ROPE_THETA = 500000.0
ATTN_BLOCK = 128
DILATIONS = (1, 4, 16)
SPAN = ATTN_BLOCK * 16
RNN_WIDTH = 1024
CONV_WIDTH = 4
RGLRU_C = 8.0
PEER_HEADS = 8
PEER_NKEYS = 128
PEER_EXPERTS = PEER_NKEYS * PEER_NKEYS
PEER_TOPK = 16
PEER_KEY_DIM = 128
PLE_DIM = 256
PROJ_CHUNK = 1024
EPS = 1e-6

VMEM_LIMIT = 56 * 1024 * 1024

IN_TM = 512
RNN_TM = 256
OUT_TM = 256
SEL_TB = 256
PEER_TM = 512
PEER_EC = 256
PEER_CHUNKS = 4
PEER_ROW_BLOCKS = 2
PEER_GATE_STRIPS = 2
PLE_TM = 512


def _params(sem):
    return pltpu.CompilerParams(dimension_semantics=sem, vmem_limit_bytes=VMEM_LIMIT)


def _rms(x, g):
    return x * lax.rsqrt(jnp.mean(x * x, axis=-1, keepdims=True) + EPS) * g


def _resident(shape):
    nd = len(shape)
    return pl.BlockSpec(shape, lambda *_: (0,) * nd, pipeline_mode=pl.Buffered(1))


def _inproj_body(x_ref, pos_ref, invf_ref, g_ref, w_ref, q_ref, k_ref, v_ref, xr_ref, gate_ref):
    tm = x_ref.shape[0]
    xb = _rms(x_ref[...], g_ref[...]).astype(BF16)
    ang = pos_ref[...].astype(F32) * invf_ref[...]
    cosf = jnp.cos(ang)
    sinf = jnp.sin(ang)
    lane = lax.broadcasted_iota(jnp.int32, (tm, HEAD_DIM), 1)
    first_half = lane < ROPE_DIM // 2

    def rope(c):
        partner = jnp.where(first_half,
                            -pltpu.roll(c, HEAD_DIM - ROPE_DIM // 2, 1),
                            pltpu.roll(c, ROPE_DIM // 2, 1))
        return c * cosf + partner * sinf

    outs = (q_ref, k_ref, v_ref, xr_ref, gate_ref)
    for n, o_ref in enumerate(outs):
        y = jnp.dot(xb, w_ref[:, n * PROJ_CHUNK:(n + 1) * PROJ_CHUNK], preferred_element_type=F32)
        if n < 2:
            scale = HEAD_DIM ** -0.5 if n == 0 else 1.0
            for h in range(N_HEADS):
                sl = slice(h * HEAD_DIM, (h + 1) * HEAD_DIM)
                o_ref[:, sl] = rope(y[:, sl]) * scale
        else:
            o_ref[...] = y


def _inproj(x, pos, invf, g, w):
    S = x.shape[0]
    tm = IN_TM
    row = lambda n: pl.BlockSpec((tm, n), lambda i: (i, 0))
    out = jax.ShapeDtypeStruct((S, PROJ_CHUNK), F32)
    return pl.pallas_call(
        _inproj_body,
        grid=(S // tm,),
        in_specs=[row(D_MODEL), row(1), _resident((1, HEAD_DIM)), _resident((1, D_MODEL)),
                  _resident(w.shape)],
        out_specs=[row(PROJ_CHUNK)] * 5,
        out_shape=[out] * 5,
        compiler_params=_params(("parallel",)),
    )(x, pos, invf, g, w)


def _attn_body(q_ref, k_ref, v_ref, o_ref, t4, q4, q16, k1, k4, k16, v1, v4, v16, op, ls):
    c = pl.program_id(1)
    kv = {1: (k1, v1), 4: (k4, v4), 16: (k16, v16)}

    @pl.when(c == 0)
    def _():
        for d in DILATIONS:
            for ref in kv[d]:
                ref[:, 0:ATTN_BLOCK, :] = jnp.zeros((d, ATTN_BLOCK, HEAD_DIM), BF16)

    @pl.when(c > 0)
    def _():
        for d in DILATIONS:
            rows = SPAN // d
            for ref in kv[d]:
                ref[:, 0:ATTN_BLOCK, :] = ref[:, rows:rows + ATTN_BLOCK, :]

    def deinterleave(x_ref, x1, x4, x16, off):
        if x1 is not None:
            x1[0, off:off + SPAN, :] = x_ref[...].astype(BF16)
        for r4 in range(4):
            t = x_ref[pl.ds(r4, SPAN // 4, stride=4), :]
            t4[r4] = t
            x4[r4, off:off + SPAN // 4, :] = t.astype(BF16)
        for r4 in range(4):
            for r2 in range(4):
                t = t4[r4, pl.ds(r2, SPAN // 16, stride=4), :]
                x16[4 * r2 + r4, off:off + SPAN // 16, :] = t.astype(BF16)

    deinterleave(q_ref, None, q4, q16, 0)
    deinterleave(k_ref, k1, k4, k16, ATTN_BLOCK)
    deinterleave(v_ref, v1, v4, v16, ATTN_BLOCK)

    qi = lax.broadcasted_iota(jnp.int32, (ATTN_BLOCK, 2 * ATTN_BLOCK), 0)
    kj = lax.broadcasted_iota(jnp.int32, (ATTN_BLOCK, 2 * ATTN_BLOCK), 1)
    dist = ATTN_BLOCK + qi - kj
    band = (dist >= 0) & (dist <= ATTN_BLOCK)
    bias_band = jnp.where(band, 0.0, -jnp.inf)
    first_key = jnp.where(c > 0, 0, ATTN_BLOCK)
    bias_first = jnp.where(band & (kj >= first_key), 0.0, -jnp.inf)

    def tile(qt, kk, vv, first_block):
        s = lax.dot_general(qt, kk, (((1,), (1,)), ((), ())), preferred_element_type=F32)
        s = s + (bias_first if first_block else bias_band)
        m = jnp.max(s, axis=-1, keepdims=True)
        e = jnp.exp(s - m)
        l = jnp.sum(e, axis=-1, keepdims=True)
        o = jnp.dot(e.astype(BF16), vv, preferred_element_type=F32)
        o = o / l
        lse = jnp.broadcast_to(m + jnp.log(l), (ATTN_BLOCK, HEAD_DIM))
        return o, lse

    def run_pattern(p_idx, d, q_tile):
        k_s, v_s = kv[d]
        nblk = SPAN // d // ATTN_BLOCK
        for r in range(d):
            for n in range(nblk):
                row0 = n * ATTN_BLOCK
                kk = k_s[r, row0:row0 + 2 * ATTN_BLOCK, :]
                vv = v_s[r, row0:row0 + 2 * ATTN_BLOCK, :]
                o, lse = tile(q_tile(r, row0), kk, vv, n == 0)
                start = n * (ATTN_BLOCK * d) + r
                dst = pl.ds(start, ATTN_BLOCK) if d == 1 else pl.ds(start, ATTN_BLOCK, stride=d)
                op[p_idx, dst, :] = o
                ls[p_idx, dst, :] = lse

    run_pattern(0, 1, lambda r, row0: q_ref[row0:row0 + ATTN_BLOCK, :].astype(BF16))
    run_pattern(1, 4, lambda r, row0: q4[r, row0:row0 + ATTN_BLOCK, :])
    run_pattern(2, 16, lambda r, row0: q16[r, row0:row0 + ATTN_BLOCK, :])

    l0, l1, l2 = ls[0], ls[1], ls[2]
    top = jnp.maximum(jnp.maximum(l0, l1), l2)
    w0 = jnp.exp(l0 - top)
    w1 = jnp.exp(l1 - top)
    w2 = jnp.exp(l2 - top)
    o_ref[...] = (w0 * op[0] + w1 * op[1] + w2 * op[2]) / (w0 + w1 + w2)


def _attention(q, k, v):
    S = q.shape[0]
    spec = pl.BlockSpec((SPAN, HEAD_DIM), lambda h, c: (c, h))
    slab = lambda d, off, dt: pltpu.VMEM((d, off + SPAN // d, HEAD_DIM), dt)
    scratch = [
        slab(4, 0, F32), slab(4, 0, BF16), slab(16, 0, BF16),
        slab(1, ATTN_BLOCK, BF16), slab(4, ATTN_BLOCK, BF16), slab(16, ATTN_BLOCK, BF16),
        slab(1, ATTN_BLOCK, BF16), slab(4, ATTN_BLOCK, BF16), slab(16, ATTN_BLOCK, BF16),
        pltpu.VMEM((3, SPAN, HEAD_DIM), F32), pltpu.VMEM((3, SPAN, HEAD_DIM), F32),
    ]
    return pl.pallas_call(
        _attn_body,
        grid=(N_HEADS, S // SPAN),
        in_specs=[spec] * 3,
        out_specs=spec,
        out_shape=jax.ShapeDtypeStruct((S, ATTN_WIDTH), F32),
        scratch_shapes=scratch,
        compiler_params=_params(("arbitrary", "arbitrary")),
    )(q, k, v)


def _rnn_body(xr_ref, gate_ref, cw_ref, cb_ref, wa_ref, ba_ref, wx_ref, bx_ref, lam_ref, g_ref,
              o_ref, xext, hcar, a_s, u_s):
    tm = xr_ref.shape[0]
    i = pl.program_id(0)

    T = SUBLANES

    @pl.when(i == 0)
    def _():
        xext[0:T, :] = jnp.zeros((T, RNN_WIDTH), F32)
        hcar[...] = jnp.zeros((T, RNN_WIDTH), F32)

    x = xr_ref[...]
    xext[T:T + tm, :] = x
    xc = cb_ref[...]
    for tap in range(CONV_WIDTH):
        sh = CONV_WIDTH - 1 - tap
        xc = xc + xext[T - sh:T - sh + tm, :] * cw_ref[tap:tap + 1, :]
    xext[0:T, :] = x[tm - T:tm, :]

    xcb = xc.astype(BF16)
    for j in range(RNN_WIDTH // LANES):
        sl = slice(j * LANES, (j + 1) * LANES)
        blk = xcb[:, sl]
        r = jax.nn.sigmoid(jnp.dot(blk, wa_ref[j], preferred_element_type=F32) + ba_ref[:, sl])
        ig = jax.nn.sigmoid(jnp.dot(blk, wx_ref[j], preferred_element_type=F32) + bx_ref[:, sl])
        log_a = -RGLRU_C * r * jax.nn.softplus(-lam_ref[:, sl])
        a = jnp.exp(log_a)
        a_s[:, sl] = a
        u_s[:, sl] = jnp.sqrt(-jnp.tanh(log_a) * (a * a + 1.0)) * ig * xc[:, sl]

    row = lax.broadcasted_iota(jnp.int32, (T, RNN_WIDTH), 0)

    def scan(t, h):
        base = pl.multiple_of(t * T, T)
        A = a_s[pl.ds(base, T), :]
        B = u_s[pl.ds(base, T), :]
        sft = 1
        while sft < T:
            valid = row >= sft
            A_sh = pltpu.roll(A, sft, 0)
            B_sh = pltpu.roll(B, sft, 0)
            B = jnp.where(valid, A * B_sh + B, B)
            A = jnp.where(valid, A * A_sh, A)
            sft *= 2
        hs = A * h + B
        u_s[pl.ds(base, T), :] = hs
        return jnp.broadcast_to(hs[T - 1:T, :], (T, RNN_WIDTH))

    hcar[...] = lax.fori_loop(0, tm // T, scan, hcar[...])

    rnn = u_s[...] * jax.nn.gelu(gate_ref[...])
    o_ref[...] = _rms(rnn, g_ref[...]).astype(BF16)


def _rnn(xr, gate, cw, cb, wa, ba, wx, bx, lam, g):
    S = xr.shape[0]
    tm = RNN_TM
    row = pl.BlockSpec((tm, RNN_WIDTH), lambda i: (i, 0))
    vec = _resident((1, RNN_WIDTH))
    return pl.pallas_call(
        _rnn_body,
        grid=(S // tm,),
        in_specs=[row, row, _resident(cw.shape), vec, _resident(wa.shape), vec,
                  _resident(wx.shape), vec, vec, vec],
        out_specs=row,
        out_shape=jax.ShapeDtypeStruct((S, RNN_WIDTH), BF16),
        scratch_shapes=[pltpu.VMEM((tm + SUBLANES, RNN_WIDTH), F32), pltpu.VMEM((SUBLANES, RNN_WIDTH), F32),
                        pltpu.VMEM((tm, RNN_WIDTH), F32), pltpu.VMEM((tm, RNN_WIDTH), F32)],
        compiler_params=_params(("arbitrary",)),
    )(xr, gate, cw, cb, wa, ba, wx, bx, lam, g)


def _outproj_body(attn_ref, rnn_ref, x_ref, ga_ref, wo_ref, gf_ref, wq_ref, sk_ref,
                  h1_ref, hnt_ref, st_ref):
    an = _rms(attn_ref[...], ga_ref[...]).astype(BF16)
    mix = jnp.dot(an, wo_ref[0:ATTN_WIDTH, :], preferred_element_type=F32)
    mix = mix + jnp.dot(rnn_ref[...], wo_ref[ATTN_WIDTH:, :], preferred_element_type=F32)
    h1 = x_ref[...] + mix
    h1_ref[...] = h1
    hn32 = _rms(h1, gf_ref[...])
    hn = hn32.astype(BF16)
    hnt_ref[...] = hn32.T.astype(BF16)
    qp = jnp.dot(hn, wq_ref[...], preferred_element_type=F32).astype(BF16)
    for hc in range(2 * PEER_HEADS):
        st_ref[hc] = lax.dot_general(sk_ref[hc], qp[:, hc * PEER_KEY_DIM:(hc + 1) * PEER_KEY_DIM],
                                     (((1,), (1,)), ((), ())), preferred_element_type=F32)


def _outproj(attn, rnn, x, ga, wo, gf, wq, sk):
    S = x.shape[0]
    tm = OUT_TM
    row = lambda n: pl.BlockSpec((tm, n), lambda i: (i, 0))
    return pl.pallas_call(
        _outproj_body,
        grid=(S // tm,),
        in_specs=[row(ATTN_WIDTH), row(RNN_WIDTH), row(D_MODEL), _resident((1, ATTN_WIDTH)),
                  _resident(wo.shape), _resident((1, D_MODEL)), _resident(wq.shape),
                  _resident(sk.shape)],
        out_specs=[row(D_MODEL), pl.BlockSpec((D_MODEL, tm), lambda i: (0, i)),
                   pl.BlockSpec((2 * PEER_HEADS, PEER_NKEYS, tm), lambda i: (0, 0, i))],
        out_shape=[jax.ShapeDtypeStruct((S, D_MODEL), F32), jax.ShapeDtypeStruct((D_MODEL, S), BF16),
                   jax.ShapeDtypeStruct((2 * PEER_HEADS, PEER_NKEYS, S), F32)],
        compiler_params=_params(("parallel",)),
    )(attn, rnn, x, ga, wo, gf, wq, sk)


def _sort16_network():
    pairs = []

    def merge(lo, n, r):
        step = 2 * r
        if step < n:
            merge(lo, n, step)
            merge(lo + r, n, step)
            pairs.extend((i, i + r) for i in range(lo + r, lo + n - r, step))
        else:
            pairs.append((lo, lo + r))

    def sort(lo, n):
        if n > 1:
            sort(lo, n // 2)
            sort(lo + n // 2, n // 2)
            merge(lo, n, 1)

    sort(0, 16)
    return tuple(pairs)


_SORT16 = _sort16_network()
_SORT10 = tuple(p for p in _SORT16 if p[1] < 10)


def _select_body(st_ref, r2_ref, e2_ref, jb_ref, e1n_ref, tops, tau_s):
    tb = st_ref.shape[-1]
    K = PEER_TOPK
    NA = 2 * PEER_HEADS
    ninf = -jnp.inf
    G = SUBLANES
    row8 = lax.broadcasted_iota(jnp.int32, (G, tb), 0)

    def pop_max(v, depth):
        mx = jnp.max(v[0], axis=0, keepdims=True)
        where_hit = jnp.where(v[0] == mx, row8, G)
        hit = where_hit == jnp.min(where_hit, axis=0, keepdims=True)
        for g in range(min(len(v), depth)):
            v[g] = jnp.where(hit, v[g + 1] if g + 1 < len(v) else ninf, v[g])
        return mx

    assert PEER_NKEYS // G == 16 and PEER_TOPK == 16

    def extract(a, carry):
        v = [st_ref[a, G * g:G * (g + 1), :] for g in range(PEER_NKEYS // G)]
        for i, j in _SORT16:
            v[i], v[j] = jnp.maximum(v[i], v[j]), jnp.minimum(v[i], v[j])
        for k in range(K):
            tops[a, k:k + 1, :] = pop_max(v, K - 1 - k)
        return carry

    lax.fori_loop(0, NA, extract, 0, unroll=8)

    def candidates(h):
        A = tops[2 * h]
        B = tops[2 * h + 1]
        B8 = B[0:8, :]
        limits = (16, 8, 5, 4, 3, 2, 2, 2)
        cands = [A[0:1, :] + B, A[1:2, :] + B8]
        for i in range(2, 8):
            cands.append(jnp.where(row8 < limits[i], A[i:i + 1, :] + B8, ninf))
        cands.append(A[8:16, :] + B[0:1, :])
        return A, B, cands

    def kth(h, carry):
        cands = candidates(h)[2]
        v = [cands[0][0:8, :], cands[0][8:16, :]] + cands[1:]
        for i, j in _SORT10:
            v[i], v[j] = jnp.maximum(v[i], v[j]), jnp.minimum(v[i], v[j])
        for k in range(K):
            tau = pop_max(v, K - 1 - k)
        tau_s[pl.ds(h, 1), :] = tau
        return carry

    lax.fori_loop(0, PEER_HEADS, kth, 0, unroll=8)

    for h in range(PEER_HEADS):
        A, B, cands = candidates(h)
        tau = tau_s[h:h + 1, :]
        cmax = A[0:1, :] + B[0:1, :]
        z = None
        for c in cands:
            zc = jnp.sum(jnp.where(c >= tau, jnp.exp(c - cmax), 0.0), axis=0, keepdims=True)
            z = zc if z is None else z + zc

        j_rows = [jnp.sum(jnp.where(cands[i] >= tau, 1.0, 0.0), axis=0, keepdims=True)
                  for i in range(8)]
        tail = jnp.where(cands[8] >= tau, 1.0, 0.0)
        for i in range(8):
            j_rows.append(tail[i:i + 1, :])

        s1 = st_ref[2 * h]
        s2 = st_ref[2 * h + 1]
        jb = jnp.zeros((PEER_NKEYS, tb), F32)
        r2 = jnp.full((PEER_NKEYS, tb), float(K), F32)
        for i in reversed(range(K)):
            jb = jnp.where(s1 == A[i:i + 1, :], j_rows[i], jb)
            r2 = jnp.where(s2 == B[i:i + 1, :], float(i), r2)
        jb_ref[h] = jb
        r2_ref[h] = r2.astype(BF16)
        e1n_ref[h] = jnp.exp(s1 - A[0:1, :]) / z
        e2_ref[h] = jnp.exp(s2 - B[0:1, :]).astype(BF16)


def _select(st):
    S = st.shape[-1]
    tb = SEL_TB
    out = lambda dt: jax.ShapeDtypeStruct((PEER_HEADS, PEER_NKEYS, S), dt)
    ospec = pl.BlockSpec((PEER_HEADS, PEER_NKEYS, tb), lambda i: (0, 0, i))
    return pl.pallas_call(
        _select_body,
        grid=(S // tb,),
        in_specs=[pl.BlockSpec((2 * PEER_HEADS, PEER_NKEYS, tb), lambda i: (0, 0, i))],
        out_specs=[ospec] * 4,
        out_shape=[out(BF16), out(BF16), out(F32), out(F32)],
        scratch_shapes=[pltpu.VMEM((2 * PEER_HEADS, PEER_TOPK, tb), F32),
                        pltpu.VMEM((PEER_HEADS, tb), F32)],
        compiler_params=_params(("parallel",)),
    )(st)


def _peer_body(hnt_ref, u_ref, vt_ref, r2_ref, e2_ref, jb_ref, e1n_ref, o_ref, w_even, w_odd):
    j = pl.program_id(1)
    ec = PEER_EC
    n_i1 = ec // PEER_NKEYS

    zero = jnp.zeros((), BF16)

    def build_gates(step, w_s, c, s):
        half = hnt_ref.shape[1] // PEER_GATE_STRIPS
        ssl = slice(s * half, (s + 1) * half)
        i1_0 = jnp.minimum((step * PEER_CHUNKS + c) * n_i1, PEER_NKEYS - n_i1)
        acc = [None] * n_i1
        for h in range(PEER_HEADS):
            r2t = r2_ref[h, :, ssl]
            e2t = e2_ref[h, :, ssl]
            for kk in range(n_i1):
                jb_b = jb_ref[h, pl.ds(i1_0 + kk, 1), :][:, ssl].astype(BF16)
                e1_b = e1n_ref[h, pl.ds(i1_0 + kk, 1), :][:, ssl].astype(BF16)
                t = jnp.where(r2t < jb_b, e2t, zero) * e1_b
                acc[kk] = t if acc[kk] is None else acc[kk] + t
        for kk in range(n_i1):
            w_s[c, kk * PEER_NKEYS:(kk + 1) * PEER_NKEYS, ssl] = acc[kk]

    @pl.when(j == 0)
    def _():
        o_ref[...] = jnp.zeros(o_ref.shape, F32)
        for c in range(PEER_CHUNKS):
            for s in range(PEER_GATE_STRIPS):
                build_gates(0, w_even, c, s)

    def step(w_cur, w_next):
        pieces = [(c, s) for c in range(PEER_CHUNKS) for s in range(PEER_GATE_STRIPS)]
        n_pieces, n_subdots = len(pieces), PEER_CHUNKS * PEER_ROW_BLOCKS

        acts = [jnp.dot(u_ref[c * ec:(c + 1) * ec, :], hnt_ref[...], preferred_element_type=F32)
                for c in range(PEER_CHUNKS)]
        ps = [w_cur[c] * jax.nn.gelu(acts[c]).astype(BF16) for c in range(PEER_CHUNKS)]
        rows = D_MODEL // PEER_ROW_BLOCKS
        for rb in range(PEER_ROW_BLOCKS):
            rsl = slice(rb * rows, (rb + 1) * rows)
            tot = None
            for c in range(PEER_CHUNKS):
                part = jnp.dot(vt_ref[c, rsl, :], ps[c], preferred_element_type=F32)
                tot = part if tot is None else tot + part
                done = rb * PEER_CHUNKS + c + 1
                while n_pieces - len(pieces) < done * n_pieces // n_subdots:
                    build_gates(j + 1, w_next, *pieces.pop(0))
            o_ref[rsl, :] += tot

    @pl.when(j % 2 == 0)
    def _():
        step(w_even, w_odd)

    @pl.when(j % 2 == 1)
    def _():
        step(w_odd, w_even)


def _peer(hnt, u, vt, r2, e2, jb, e1n):
    S = hnt.shape[1]
    tm, ec, nc = PEER_TM, PEER_EC, PEER_CHUNKS
    tokt = pl.BlockSpec((D_MODEL, tm), lambda i, j: (0, i))
    sel = pl.BlockSpec((PEER_HEADS, PEER_NKEYS, tm), lambda i, j: (0, 0, i))
    return pl.pallas_call(
        _peer_body,
        grid=(S // tm, PEER_EXPERTS // (ec * nc)),
        in_specs=[tokt,
                  pl.BlockSpec((ec * nc, D_MODEL), lambda i, j: (j, 0)),
                  pl.BlockSpec((nc, D_MODEL, ec), lambda i, j: (j, 0, 0)),
                  sel, sel, sel, sel],
        out_specs=tokt,
        out_shape=jax.ShapeDtypeStruct((D_MODEL, S), F32),
        scratch_shapes=[pltpu.VMEM((nc, ec, tm), BF16), pltpu.VMEM((nc, ec, tm), BF16)],
        compiler_params=_params(("parallel", "arbitrary")),
    )(hnt, u, vt, r2, e2, jb, e1n)


def _ple_body(h_ref, ft_ref, p_ref, gp_ref, wg_ref, bg_ref, wp_ref, gf_ref, o_ref):
    half = h_ref.shape[0] // 2
    for r in range(2):
        rsl = slice(r * half, (r + 1) * half)
        h = h_ref[rsl, :] + ft_ref[:, rsl].T
        hn = _rms(h, gp_ref[...]).astype(BF16)
        g = jax.nn.sigmoid(jnp.dot(hn, wg_ref[...], preferred_element_type=F32) + bg_ref[...])
        pp = jnp.dot(p_ref[rsl, :].astype(BF16), wp_ref[...], preferred_element_type=F32)
        o_ref[rsl, :] = _rms(h + g * pp, gf_ref[...])


def _ple(h, ft, p, gp, wg, bg, wp, gf):
    S = h.shape[0]
    tm = PLE_TM
    row = lambda n: pl.BlockSpec((tm, n), lambda i: (i, 0))
    vec = _resident((1, D_MODEL))
    return pl.pallas_call(
        _ple_body,
        grid=(S // tm,),
        in_specs=[row(D_MODEL), pl.BlockSpec((D_MODEL, tm), lambda i: (0, i)), row(PLE_DIM), vec,
                  _resident(wg.shape), vec, _resident(wp.shape), vec],
        out_specs=row(D_MODEL),
        out_shape=jax.ShapeDtypeStruct((S, D_MODEL), F32),
        compiler_params=_params(("parallel",)),
    )(h, ft, p, gp, wg, bg, wp, gf)


def _pair_blocks(w):
    w = w.reshape(8, 2, 64, 64)
    z = jnp.zeros((8, 64, 64), w.dtype)
    top = jnp.concatenate([w[:, 0], z], axis=-1)
    bot = jnp.concatenate([z, w[:, 1]], axis=-1)
    return jnp.concatenate([top, bot], axis=-2).astype(BF16)


def kernel(x, p, positions, mix_norm, w_in, conv_w, conv_b, rg_wa, rg_ba, rg_wx, rg_bx, rg_lambda,
           attn_out_norm, rnn_out_norm, w_out, ffn_norm, peer_wq, peer_subkeys, peer_u, peer_v,
           ple_norm, ple_w_gate, ple_b_gate, ple_proj, final_norm):
    B, S, D = x.shape
    assert (B, S, D) == (1, SEQ, D_MODEL) and S % SPAN == 0 and w_in.shape[0] == 1
    vec = lambda a: a.reshape(1, -1).astype(F32)

    half = ROPE_DIM // 2
    inv_freq = ROPE_THETA ** (-jnp.arange(half, dtype=F32) * 2.0 / ROPE_DIM)
    invf = jnp.zeros((1, HEAD_DIM), F32).at[0, :ROPE_DIM].set(jnp.tile(inv_freq, 2))

    q, k, v, xr, gate = _inproj(x[0], positions.reshape(S, 1), invf, vec(mix_norm[0]),
                                w_in[0].astype(BF16))
    attn = _attention(q, k, v)
    rnn = _rnn(xr, gate, conv_w[0], vec(conv_b[0]), _pair_blocks(rg_wa[0]), vec(rg_ba[0]),
               _pair_blocks(rg_wx[0]), vec(rg_bx[0]), vec(rg_lambda[0]), vec(rnn_out_norm[0]))
    sk = peer_subkeys[0].reshape(2 * PEER_HEADS, PEER_NKEYS, -1).astype(BF16)
    h1, hnt, st = _outproj(attn, rnn, x[0], vec(attn_out_norm[0]), w_out[0].astype(BF16),
                           vec(ffn_norm[0]), peer_wq[0].astype(BF16), sk)
    r2, e2, jb, e1n = _select(st)
    vt = peer_v[0].astype(BF16).reshape(PEER_EXPERTS // PEER_EC, PEER_EC, D).transpose(0, 2, 1)
    ft = _peer(hnt, peer_u[0].astype(BF16), vt, r2, e2, jb, e1n)
    out = _ple(h1, ft, p[0, 0], vec(ple_norm[0]), ple_w_gate[0].astype(BF16), vec(ple_b_gate[0]),
               ple_proj[0].astype(BF16), vec(final_norm))
    return out.reshape(B, S, D)
```

```python
import jax
import jax.numpy as jnp
from jax import lax
from jax.experimental import pallas as pl
from jax.experimental.pallas import tpu as pltpu

F32 = jnp.float32
BF16 = jnp.bfloat16

LANES = 128
SUBLANES = 8

D_MODEL = 2048
SEQ = 8192
HEAD_DIM = 128
ATTN_WIDTH = 1024
N_HEADS = 8
ROPE_DIM = 32
ROPE_THETA = 500000.0
ATTN_BLOCK = 128
DILATIONS = (1, 4, 16)
SPAN = ATTN_BLOCK * 16
RNN_WIDTH = 1024
CONV_WIDTH = 4
RGLRU_C = 8.0
PEER_HEADS = 8
PEER_NKEYS = 128
PEER_EXPERTS = PEER_NKEYS * PEER_NKEYS
PEER_TOPK = 16
PEER_KEY_DIM = 128
PLE_DIM = 256
PROJ_CHUNK = 1024
EPS = 1e-6

VMEM_LIMIT = 56 * 1024 * 1024

IN_TM = 512
RNN_TM = 256
OUT_TM = 256
SEL_TB = 256
PEER_TM = 512
PEER_EC = 256
PEER_CHUNKS = 4
PEER_ROW_BLOCKS = 2
PEER_GATE_STRIPS = 2
PLE_TM = 512


def _params(sem):
    return pltpu.CompilerParams(dimension_semantics=sem, vmem_limit_bytes=VMEM_LIMIT)


def _rms(x, g):
    return x * lax.rsqrt(jnp.mean(x * x, axis=-1, keepdims=True) + EPS) * g


def _resident(shape):
    nd = len(shape)
    return pl.BlockSpec(shape, lambda *_: (0,) * nd, pipeline_mode=pl.Buffered(1))


def _inproj_body(x_ref, pos_ref, invf_ref, g_ref, w_ref, q_ref, k_ref, v_ref, xr_ref, gate_ref):
    tm = x_ref.shape[0]
    xb = _rms(x_ref[...], g_ref[...]).astype(BF16)
    ang = pos_ref[...].astype(F32) * invf_ref[...]
    cosf = jnp.cos(ang)
    sinf = jnp.sin(ang)
    lane = lax.broadcasted_iota(jnp.int32, (tm, HEAD_DIM), 1)
    first_half = lane < ROPE_DIM // 2

    def rope(c):
        partner = jnp.where(first_half,
                            -pltpu.roll(c, HEAD_DIM - ROPE_DIM // 2, 1),
                            pltpu.roll(c, ROPE_DIM // 2, 1))
        return c * cosf + partner * sinf

    outs = (q_ref, k_ref, v_ref, xr_ref, gate_ref)
    for n, o_ref in enumerate(outs):
        y = jnp.dot(xb, w_ref[:, n * PROJ_CHUNK:(n + 1) * PROJ_CHUNK], preferred_element_type=F32)
        if n < 2:
            scale = HEAD_DIM ** -0.5 if n == 0 else 1.0
            for h in range(N_HEADS):
                sl = slice(h * HEAD_DIM, (h + 1) * HEAD_DIM)
                o_ref[:, sl] = rope(y[:, sl]) * scale
        else:
            o_ref[...] = y


def _inproj(x, pos, invf, g, w):
    S = x.shape[0]
    tm = IN_TM
    row = lambda n: pl.BlockSpec((tm, n), lambda i: (i, 0))
    out = jax.ShapeDtypeStruct((S, PROJ_CHUNK), F32)
    return pl.pallas_call(
        _inproj_body,
        grid=(S // tm,),
        in_specs=[row(D_MODEL), row(1), _resident((1, HEAD_DIM)), _resident((1, D_MODEL)),
                  _resident(w.shape)],
        out_specs=[row(PROJ_CHUNK)] * 5,
        out_shape=[out] * 5,
        compiler_params=_params(("parallel",)),
    )(x, pos, invf, g, w)


def _attn_body(q_ref, k_ref, v_ref, o_ref, t4, q4, q16, k1, k4, k16, v1, v4, v16, op, ls):
    c = pl.program_id(1)
    kv = {1: (k1, v1), 4: (k4, v4), 16: (k16, v16)}

    @pl.when(c == 0)
    def _():
        for d in DILATIONS:
            for ref in kv[d]:
                ref[:, 0:ATTN_BLOCK, :] = jnp.zeros((d, ATTN_BLOCK, HEAD_DIM), BF16)

    @pl.when(c > 0)
    def _():
        for d in DILATIONS:
            rows = SPAN // d
            for ref in kv[d]:
                ref[:, 0:ATTN_BLOCK, :] = ref[:, rows:rows + ATTN_BLOCK, :]

    def deinterleave(x_ref, x1, x4, x16, off):
        if x1 is not None:
            x1[0, off:off + SPAN, :] = x_ref[...].astype(BF16)
        for r4 in range(4):
            t = x_ref[pl.ds(r4, SPAN // 4, stride=4), :]
            t4[r4] = t
            x4[r4, off:off + SPAN // 4, :] = t.astype(BF16)
        for r4 in range(4):
            for r2 in range(4):
                t = t4[r4, pl.ds(r2, SPAN // 16, stride=4), :]
                x16[4 * r2 + r4, off:off + SPAN // 16, :] = t.astype(BF16)

    deinterleave(q_ref, None, q4, q16, 0)
    deinterleave(k_ref, k1, k4, k16, ATTN_BLOCK)
    deinterleave(v_ref, v1, v4, v16, ATTN_BLOCK)

    qi = lax.broadcasted_iota(jnp.int32, (ATTN_BLOCK, 2 * ATTN_BLOCK), 0)
    kj = lax.broadcasted_iota(jnp.int32, (ATTN_BLOCK, 2 * ATTN_BLOCK), 1)
    dist = ATTN_BLOCK + qi - kj
    band = (dist >= 0) & (dist <= ATTN_BLOCK)
    bias_band = jnp.where(band, 0.0, -jnp.inf)
    first_key = jnp.where(c > 0, 0, ATTN_BLOCK)
    bias_first = jnp.where(band & (kj >= first_key), 0.0, -jnp.inf)

    def tile(qt, kk, vv, first_block):
        s = lax.dot_general(qt, kk, (((1,), (1,)), ((), ())), preferred_element_type=F32)
        s = s + (bias_first if first_block else bias_band)
        m = jnp.max(s, axis=-1, keepdims=True)
        e = jnp.exp(s - m)
        l = jnp.sum(e, axis=-1, keepdims=True)
        o = jnp.dot(e.astype(BF16), vv, preferred_element_type=F32)
        o = o / l
        lse = jnp.broadcast_to(m + jnp.log(l), (ATTN_BLOCK, HEAD_DIM))
        return o, lse

    def run_pattern(p_idx, d, q_tile):
        k_s, v_s = kv[d]
        nblk = SPAN // d // ATTN_BLOCK
        for r in range(d):
            for n in range(nblk):
                row0 = n * ATTN_BLOCK
                kk = k_s[r, row0:row0 + 2 * ATTN_BLOCK, :]
                vv = v_s[r, row0:row0 + 2 * ATTN_BLOCK, :]
                o, lse = tile(q_tile(r, row0), kk, vv, n == 0)
                if d == 1:
                    dst = pl.ds(row0, ATTN_BLOCK)
                elif d == 4:
                    dst = pl.ds(r * quarter + row0, ATTN_BLOCK)
                else:
                    dst = pl.ds((r % 4) * quarter + r // 4, ATTN_BLOCK, stride=4)
                op[p_idx, dst, :] = o
                ls[p_idx, dst, :] = lse

    quarter = SPAN // 4
    run_pattern(0, 1, lambda r, row0: q_ref[row0:row0 + ATTN_BLOCK, :].astype(BF16))
    run_pattern(1, 4, lambda r, row0: q4[r, row0:row0 + ATTN_BLOCK, :])
    run_pattern(2, 16, lambda r, row0: q16[r, row0:row0 + ATTN_BLOCK, :])

    for r4 in range(4):
        nat = pl.ds(r4, quarter, stride=4)
        blk = pl.ds(r4 * quarter, quarter)
        l0, l1, l2 = ls[0, nat, :], ls[1, blk, :], ls[2, blk, :]
        top = jnp.maximum(jnp.maximum(l0, l1), l2)
        w0 = jnp.exp(l0 - top)
        w1 = jnp.exp(l1 - top)
        w2 = jnp.exp(l2 - top)
        o_ref[nat, :] = (w0 * op[0, nat, :] + w1 * op[1, blk, :] + w2 * op[2, blk, :]) / (w0 + w1 + w2)


def _attention(q, k, v):
    S = q.shape[0]
    spec = pl.BlockSpec((SPAN, HEAD_DIM), lambda h, c: (c, h))
    slab = lambda d, off, dt: pltpu.VMEM((d, off + SPAN // d, HEAD_DIM), dt)
    scratch = [
        slab(4, 0, F32), slab(4, 0, BF16), slab(16, 0, BF16),
        slab(1, ATTN_BLOCK, BF16), slab(4, ATTN_BLOCK, BF16), slab(16, ATTN_BLOCK, BF16),
        slab(1, ATTN_BLOCK, BF16), slab(4, ATTN_BLOCK, BF16), slab(16, ATTN_BLOCK, BF16),
        pltpu.VMEM((3, SPAN, HEAD_DIM), F32), pltpu.VMEM((3, SPAN, HEAD_DIM), F32),
    ]
    return pl.pallas_call(
        _attn_body,
        grid=(N_HEADS, S // SPAN),
        in_specs=[spec] * 3,
        out_specs=spec,
        out_shape=jax.ShapeDtypeStruct((S, ATTN_WIDTH), F32),
        scratch_shapes=scratch,
        compiler_params=_params(("arbitrary", "arbitrary")),
    )(q, k, v)


def _rnn_body(xr_ref, gate_ref, cw_ref, cb_ref, wa_ref, ba_ref, wx_ref, bx_ref, lam_ref, g_ref,
              o_ref, xext, hcar, a_s, u_s):
    tm = xr_ref.shape[0]
    i = pl.program_id(0)

    T = SUBLANES

    @pl.when(i == 0)
    def _():
        xext[0:T, :] = jnp.zeros((T, RNN_WIDTH), F32)
        hcar[...] = jnp.zeros((T, RNN_WIDTH), F32)

    x = xr_ref[...]
    xext[T:T + tm, :] = x
    xc = cb_ref[...]
    for tap in range(CONV_WIDTH):
        sh = CONV_WIDTH - 1 - tap
        xc = xc + xext[T - sh:T - sh + tm, :] * cw_ref[tap:tap + 1, :]
    xext[0:T, :] = x[tm - T:tm, :]

    xcb = xc.astype(BF16)
    for j in range(RNN_WIDTH // LANES):
        sl = slice(j * LANES, (j + 1) * LANES)
        blk = xcb[:, sl]
        r = jax.nn.sigmoid(jnp.dot(blk, wa_ref[j], preferred_element_type=F32) + ba_ref[:, sl])
        ig = jax.nn.sigmoid(jnp.dot(blk, wx_ref[j], preferred_element_type=F32) + bx_ref[:, sl])
        log_a = -RGLRU_C * r * jax.nn.softplus(-lam_ref[:, sl])
        a = jnp.exp(log_a)
        a_s[:, sl] = a
        u_s[:, sl] = jnp.sqrt(-jnp.tanh(log_a) * (a * a + 1.0)) * ig * xc[:, sl]

    row = lax.broadcasted_iota(jnp.int32, (T, RNN_WIDTH), 0)

    def scan(t, h):
        base = pl.multiple_of(t * T, T)
        A = a_s[pl.ds(base, T), :]
        B = u_s[pl.ds(base, T), :]
        sft = 1
        while sft < T:
            valid = row >= sft
            A_sh = pltpu.roll(A, sft, 0)
            B_sh = pltpu.roll(B, sft, 0)
            B = jnp.where(valid, A * B_sh + B, B)
            A = jnp.where(valid, A * A_sh, A)
            sft *= 2
        hs = A * h + B
        u_s[pl.ds(base, T), :] = hs
        return jnp.broadcast_to(hs[T - 1:T, :], (T, RNN_WIDTH))

    hcar[...] = lax.fori_loop(0, tm // T, scan, hcar[...])

    rnn = u_s[...] * jax.nn.gelu(gate_ref[...])
    o_ref[...] = _rms(rnn, g_ref[...]).astype(BF16)


def _rnn(xr, gate, cw, cb, wa, ba, wx, bx, lam, g):
    S = xr.shape[0]
    tm = RNN_TM
    row = pl.BlockSpec((tm, RNN_WIDTH), lambda i: (i, 0))
    vec = _resident((1, RNN_WIDTH))
    return pl.pallas_call(
        _rnn_body,
        grid=(S // tm,),
        in_specs=[row, row, _resident(cw.shape), vec, _resident(wa.shape), vec,
                  _resident(wx.shape), vec, vec, vec],
        out_specs=row,
        out_shape=jax.ShapeDtypeStruct((S, RNN_WIDTH), BF16),
        scratch_shapes=[pltpu.VMEM((tm + SUBLANES, RNN_WIDTH), F32), pltpu.VMEM((SUBLANES, RNN_WIDTH), F32),
                        pltpu.VMEM((tm, RNN_WIDTH), F32), pltpu.VMEM((tm, RNN_WIDTH), F32)],
        compiler_params=_params(("arbitrary",)),
    )(xr, gate, cw, cb, wa, ba, wx, bx, lam, g)


def _outproj_body(attn_ref, rnn_ref, x_ref, ga_ref, wo_ref, gf_ref, wq_ref, sk_ref,
                  h1_ref, hnt_ref, st_ref):
    an = _rms(attn_ref[...], ga_ref[...]).astype(BF16)
    mix = jnp.dot(an, wo_ref[0:ATTN_WIDTH, :], preferred_element_type=F32)
    mix = mix + jnp.dot(rnn_ref[...], wo_ref[ATTN_WIDTH:, :], preferred_element_type=F32)
    h1 = x_ref[...] + mix
    h1_ref[...] = h1
    hn32 = _rms(h1, gf_ref[...])
    hn = hn32.astype(BF16)
    hnt_ref[...] = hn32.T.astype(BF16)
    qp = jnp.dot(hn, wq_ref[...], preferred_element_type=F32).astype(BF16)
    for hc in range(2 * PEER_HEADS):
        st_ref[hc] = lax.dot_general(sk_ref[hc], qp[:, hc * PEER_KEY_DIM:(hc + 1) * PEER_KEY_DIM],
                                     (((1,), (1,)), ((), ())), preferred_element_type=F32)


def _outproj(attn, rnn, x, ga, wo, gf, wq, sk):
    S = x.shape[0]
    tm = OUT_TM
    row = lambda n: pl.BlockSpec((tm, n), lambda i: (i, 0))
    return pl.pallas_call(
        _outproj_body,
        grid=(S // tm,),
        in_specs=[row(ATTN_WIDTH), row(RNN_WIDTH), row(D_MODEL), _resident((1, ATTN_WIDTH)),
                  _resident(wo.shape), _resident((1, D_MODEL)), _resident(wq.shape),
                  _resident(sk.shape)],
        out_specs=[row(D_MODEL), pl.BlockSpec((D_MODEL, tm), lambda i: (0, i)),
                   pl.BlockSpec((2 * PEER_HEADS, PEER_NKEYS, tm), lambda i: (0, 0, i))],
        out_shape=[jax.ShapeDtypeStruct((S, D_MODEL), F32), jax.ShapeDtypeStruct((D_MODEL, S), BF16),
                   jax.ShapeDtypeStruct((2 * PEER_HEADS, PEER_NKEYS, S), F32)],
        compiler_params=_params(("parallel",)),
    )(attn, rnn, x, ga, wo, gf, wq, sk)


def _sort16_network():
    pairs = []

    def merge(lo, n, r):
        step = 2 * r
        if step < n:
            merge(lo, n, step)
            merge(lo + r, n, step)
            pairs.extend((i, i + r) for i in range(lo + r, lo + n - r, step))
        else:
            pairs.append((lo, lo + r))

    def sort(lo, n):
        if n > 1:
            sort(lo, n // 2)
            sort(lo + n // 2, n // 2)
            merge(lo, n, 1)

    sort(0, 16)
    return tuple(pairs)


_SORT16 = _sort16_network()
_SORT10 = tuple(p for p in _SORT16 if p[1] < 10)


def _select_body(st_ref, r2_ref, e2_ref, jb_ref, e1n_ref, tops, tau_s):
    tb = st_ref.shape[-1]
    K = PEER_TOPK
    NA = 2 * PEER_HEADS
    ninf = -jnp.inf
    G = SUBLANES
    row8 = lax.broadcasted_iota(jnp.int32, (G, tb), 0)

    def pop_max(v, depth):
        mx = jnp.max(v[0], axis=0, keepdims=True)
        where_hit = jnp.where(v[0] == mx, row8, G)
        hit = where_hit == jnp.min(where_hit, axis=0, keepdims=True)
        for g in range(min(len(v), depth)):
            v[g] = jnp.where(hit, v[g + 1] if g + 1 < len(v) else ninf, v[g])
        return mx

    assert PEER_NKEYS // G == 16 and PEER_TOPK == 16

    def extract(a, carry):
        v = [st_ref[a, G * g:G * (g + 1), :] for g in range(PEER_NKEYS // G)]
        for i, j in _SORT16:
            v[i], v[j] = jnp.maximum(v[i], v[j]), jnp.minimum(v[i], v[j])
        for k in range(K):
            tops[a, k:k + 1, :] = pop_max(v, K - 1 - k)
        return carry

    lax.fori_loop(0, NA, extract, 0, unroll=8)

    def candidates(h):
        A = tops[2 * h]
        B = tops[2 * h + 1]
        B8 = B[0:8, :]
        limits = (16, 8, 5, 4, 3, 2, 2, 2)
        cands = [A[0:1, :] + B, A[1:2, :] + B8]
        for i in range(2, 8):
            cands.append(jnp.where(row8 < limits[i], A[i:i + 1, :] + B8, ninf))
        cands.append(A[8:16, :] + B[0:1, :])
        return A, B, cands

    def kth(h, carry):
        cands = candidates(h)[2]
        v = [cands[0][0:8, :], cands[0][8:16, :]] + cands[1:]
        for i, j in _SORT10:
            v[i], v[j] = jnp.maximum(v[i], v[j]), jnp.minimum(v[i], v[j])
        for k in range(K):
            tau = pop_max(v, K - 1 - k)
        tau_s[pl.ds(h, 1), :] = tau
        return carry

    lax.fori_loop(0, PEER_HEADS, kth, 0, unroll=8)

    for h in range(PEER_HEADS):
        A, B, cands = candidates(h)
        tau = tau_s[h:h + 1, :]
        cmax = A[0:1, :] + B[0:1, :]
        z = None
        for c in cands:
            zc = jnp.sum(jnp.where(c >= tau, jnp.exp(c - cmax), 0.0), axis=0, keepdims=True)
            z = zc if z is None else z + zc

        j_rows = [jnp.sum(jnp.where(cands[i] >= tau, 1.0, 0.0), axis=0, keepdims=True)
                  for i in range(8)]
        tail = jnp.where(cands[8] >= tau, 1.0, 0.0)
        for i in range(8):
            j_rows.append(tail[i:i + 1, :])

        s1 = st_ref[2 * h]
        s2 = st_ref[2 * h + 1]
        jb = jnp.zeros((PEER_NKEYS, tb), F32)
        r2 = jnp.full((PEER_NKEYS, tb), float(K), F32)
        for i in reversed(range(K)):
            jb = jnp.where(s1 == A[i:i + 1, :], j_rows[i], jb)
            r2 = jnp.where(s2 == B[i:i + 1, :], float(i), r2)
        jb_ref[h] = jb
        r2_ref[h] = r2.astype(BF16)
        e1n_ref[h] = jnp.exp(s1 - A[0:1, :]) / z
        e2_ref[h] = jnp.exp(s2 - B[0:1, :]).astype(BF16)


def _select(st):
    S = st.shape[-1]
    tb = SEL_TB
    out = lambda dt: jax.ShapeDtypeStruct((PEER_HEADS, PEER_NKEYS, S), dt)
    ospec = pl.BlockSpec((PEER_HEADS, PEER_NKEYS, tb), lambda i: (0, 0, i))
    return pl.pallas_call(
        _select_body,
        grid=(S // tb,),
        in_specs=[pl.BlockSpec((2 * PEER_HEADS, PEER_NKEYS, tb), lambda i: (0, 0, i))],
        out_specs=[ospec] * 4,
        out_shape=[out(BF16), out(BF16), out(F32), out(F32)],
        scratch_shapes=[pltpu.VMEM((2 * PEER_HEADS, PEER_TOPK, tb), F32),
                        pltpu.VMEM((PEER_HEADS, tb), F32)],
        compiler_params=_params(("parallel",)),
    )(st)


def _peer_body(hnt_ref, u_ref, vt_ref, r2_ref, e2_ref, jb_ref, e1n_ref, o_ref, w_even, w_odd):
    j = pl.program_id(1)
    ec = PEER_EC
    n_i1 = ec // PEER_NKEYS

    zero = jnp.zeros((), BF16)

    def build_gates(step, w_s, c, s):
        half = hnt_ref.shape[1] // PEER_GATE_STRIPS
        ssl = slice(s * half, (s + 1) * half)
        i1_0 = jnp.minimum((step * PEER_CHUNKS + c) * n_i1, PEER_NKEYS - n_i1)
        acc = [None] * n_i1
        for h in range(PEER_HEADS):
            r2t = r2_ref[h, :, ssl]
            e2t = e2_ref[h, :, ssl]
            for kk in range(n_i1):
                jb_b = jb_ref[h, pl.ds(i1_0 + kk, 1), :][:, ssl].astype(BF16)
                e1_b = e1n_ref[h, pl.ds(i1_0 + kk, 1), :][:, ssl].astype(BF16)
                t = jnp.where(r2t < jb_b, e2t, zero) * e1_b
                acc[kk] = t if acc[kk] is None else acc[kk] + t
        for kk in range(n_i1):
            w_s[c, kk * PEER_NKEYS:(kk + 1) * PEER_NKEYS, ssl] = acc[kk]

    @pl.when(j == 0)
    def _():
        o_ref[...] = jnp.zeros(o_ref.shape, F32)
        for c in range(PEER_CHUNKS):
            for s in range(PEER_GATE_STRIPS):
                build_gates(0, w_even, c, s)

    def step(w_cur, w_next):
        pieces = [(c, s) for c in range(PEER_CHUNKS) for s in range(PEER_GATE_STRIPS)]
        n_pieces, n_subdots = len(pieces), PEER_CHUNKS * PEER_ROW_BLOCKS

        acts = [jnp.dot(u_ref[c * ec:(c + 1) * ec, :], hnt_ref[...], preferred_element_type=F32)
                for c in range(PEER_CHUNKS)]
        ps = [w_cur[c] * jax.nn.gelu(acts[c]).astype(BF16) for c in range(PEER_CHUNKS)]
        rows = D_MODEL // PEER_ROW_BLOCKS
        for rb in range(PEER_ROW_BLOCKS):
            rsl = slice(rb * rows, (rb + 1) * rows)
            tot = None
            for c in range(PEER_CHUNKS):
                part = jnp.dot(vt_ref[c, rsl, :], ps[c], preferred_element_type=F32)
                tot = part if tot is None else tot + part
                done = rb * PEER_CHUNKS + c + 1
                while n_pieces - len(pieces) < done * n_pieces // n_subdots:
                    build_gates(j + 1, w_next, *pieces.pop(0))
            o_ref[rsl, :] += tot

    @pl.when(j % 2 == 0)
    def _():
        step(w_even, w_odd)

    @pl.when(j % 2 == 1)
    def _():
        step(w_odd, w_even)


def _peer(hnt, u, vt, r2, e2, jb, e1n):
    S = hnt.shape[1]
    tm, ec, nc = PEER_TM, PEER_EC, PEER_CHUNKS
    tokt = pl.BlockSpec((D_MODEL, tm), lambda i, j: (0, i))
    sel = pl.BlockSpec((PEER_HEADS, PEER_NKEYS, tm), lambda i, j: (0, 0, i))
    return pl.pallas_call(
        _peer_body,
        grid=(S // tm, PEER_EXPERTS // (ec * nc)),
        in_specs=[tokt,
                  pl.BlockSpec((ec * nc, D_MODEL), lambda i, j: (j, 0)),
                  pl.BlockSpec((nc, D_MODEL, ec), lambda i, j: (j, 0, 0)),
                  sel, sel, sel, sel],
        out_specs=tokt,
        out_shape=jax.ShapeDtypeStruct((D_MODEL, S), F32),
        scratch_shapes=[pltpu.VMEM((nc, ec, tm), BF16), pltpu.VMEM((nc, ec, tm), BF16)],
        compiler_params=_params(("parallel", "arbitrary")),
    )(hnt, u, vt, r2, e2, jb, e1n)


def _ple_body(h_ref, ft_ref, p_ref, gp_ref, wg_ref, bg_ref, wp_ref, gf_ref, o_ref):
    half = h_ref.shape[0] // 2
    for r in range(2):
        rsl = slice(r * half, (r + 1) * half)
        h = h_ref[rsl, :] + ft_ref[:, rsl].T
        hn = _rms(h, gp_ref[...]).astype(BF16)
        g = jax.nn.sigmoid(jnp.dot(hn, wg_ref[...], preferred_element_type=F32) + bg_ref[...])
        pp = jnp.dot(p_ref[rsl, :].astype(BF16), wp_ref[...], preferred_element_type=F32)
        o_ref[rsl, :] = _rms(h + g * pp, gf_ref[...])


def _ple(h, ft, p, gp, wg, bg, wp, gf):
    S = h.shape[0]
    tm = PLE_TM
    row = lambda n: pl.BlockSpec((tm, n), lambda i: (i, 0))
    vec = _resident((1, D_MODEL))
    return pl.pallas_call(
        _ple_body,
        grid=(S // tm,),
        in_specs=[row(D_MODEL), pl.BlockSpec((D_MODEL, tm), lambda i: (0, i)), row(PLE_DIM), vec,
                  _resident(wg.shape), vec, _resident(wp.shape), vec],
        out_specs=row(D_MODEL),
        out_shape=jax.ShapeDtypeStruct((S, D_MODEL), F32),
        compiler_params=_params(("parallel",)),
    )(h, ft, p, gp, wg, bg, wp, gf)


def _pair_blocks(w):
    w = w.reshape(8, 2, 64, 64)
    z = jnp.zeros((8, 64, 64), w.dtype)
    top = jnp.concatenate([w[:, 0], z], axis=-1)
    bot = jnp.concatenate([z, w[:, 1]], axis=-1)
    return jnp.concatenate([top, bot], axis=-2).astype(BF16)


def kernel(x, p, positions, mix_norm, w_in, conv_w, conv_b, rg_wa, rg_ba, rg_wx, rg_bx, rg_lambda,
           attn_out_norm, rnn_out_norm, w_out, ffn_norm, peer_wq, peer_subkeys, peer_u, peer_v,
           ple_norm, ple_w_gate, ple_b_gate, ple_proj, final_norm):
    B, S, D = x.shape
    assert (B, S, D) == (1, SEQ, D_MODEL) and S % SPAN == 0 and w_in.shape[0] == 1
    vec = lambda a: a.reshape(1, -1).astype(F32)

    half = ROPE_DIM // 2
    inv_freq = ROPE_THETA ** (-jnp.arange(half, dtype=F32) * 2.0 / ROPE_DIM)
    invf = jnp.zeros((1, HEAD_DIM), F32).at[0, :ROPE_DIM].set(jnp.tile(inv_freq, 2))

    q, k, v, xr, gate = _inproj(x[0], positions.reshape(S, 1), invf, vec(mix_norm[0]),
                                w_in[0].astype(BF16))
    attn = _attention(q, k, v)
    rnn = _rnn(xr, gate, conv_w[0], vec(conv_b[0]), _pair_blocks(rg_wa[0]), vec(rg_ba[0]),
               _pair_blocks(rg_wx[0]), vec(rg_bx[0]), vec(rg_lambda[0]), vec(rnn_out_norm[0]))
    sk = peer_subkeys[0].reshape(2 * PEER_HEADS, PEER_NKEYS, -1).astype(BF16)
    h1, hnt, st = _outproj(attn, rnn, x[0], vec(attn_out_norm[0]), w_out[0].astype(BF16),
                           vec(ffn_norm[0]), peer_wq[0].astype(BF16), sk)
    r2, e2, jb, e1n = _select(st)
    vt = peer_v[0].astype(BF16).reshape(PEER_EXPERTS // PEER_EC, PEER_EC, D).transpose(0, 2, 1)
    ft = _peer(hnt, peer_u[0].astype(BF16), vt, r2, e2, jb, e1n)
    out = _ple(h1, ft, p[0, 0], vec(ple_norm[0]), ple_w_gate[0].astype(BF16), vec(ple_b_gate[0]),
               ple_proj[0].astype(BF16), vec(final_norm))
    return out.reshape(B, S, D)
```

```python
import jax
import jax.numpy as jnp
from jax import lax
from jax.experimental import pallas as pl
from jax.experimental.pallas import tpu as pltpu

F32 = jnp.float32
BF16 = jnp.bfloat16

LANES = 128
SUBLANES = 8

D_MODEL = 2048
SEQ = 8192
HEAD_DIM = 128
ATTN_WIDTH = 1024
N_HEADS = 8
ROPE_DIM = 32
ROPE_THETA = 500000.0
ATTN_BLOCK = 128
DILATIONS = (1, 4, 16)
SPAN = ATTN_BLOCK * 16
RNN_WIDTH = 1024
CONV_WIDTH = 4
RGLRU_C = 8.0
PEER_HEADS = 8
PEER_NKEYS = 128
PEER_EXPERTS = PEER_NKEYS * PEER_NKEYS
PEER_TOPK = 16
PEER_KEY_DIM = 128
PLE_DIM = 256
PROJ_CHUNK = 1024
EPS = 1e-6

VMEM_LIMIT = 56 * 1024 * 1024

IN_TM = 512
RNN_TM = 256
OUT_TM = 256
SEL_TB = 256
PEER_TM = 512
PEER_EC = 256
PEER_CHUNKS = 4
PEER_ROW_BLOCKS = 2
PEER_GATE_STRIPS = 2
PLE_TM = 512


def _params(sem):
    return pltpu.CompilerParams(dimension_semantics=sem, vmem_limit_bytes=VMEM_LIMIT)


def _rms(x, g):
    return x * lax.rsqrt(jnp.mean(x * x, axis=-1, keepdims=True) + EPS) * g


def _resident(shape):
    nd = len(shape)
    return pl.BlockSpec(shape, lambda *_: (0,) * nd, pipeline_mode=pl.Buffered(1))


def _inproj_body(x_ref, pos_ref, invf_ref, g_ref, w_ref, q_ref, k_ref, v_ref, xr_ref, gate_ref):
    tm = x_ref.shape[0]
    xb = _rms(x_ref[...], g_ref[...]).astype(BF16)
    ang = pos_ref[...].astype(F32) * invf_ref[...]
    cosf = jnp.cos(ang)
    sinf = jnp.sin(ang)
    lane = lax.broadcasted_iota(jnp.int32, (tm, HEAD_DIM), 1)
    first_half = lane < ROPE_DIM // 2

    def rope(c):
        partner = jnp.where(first_half,
                            -pltpu.roll(c, HEAD_DIM - ROPE_DIM // 2, 1),
                            pltpu.roll(c, ROPE_DIM // 2, 1))
        return c * cosf + partner * sinf

    outs = (q_ref, k_ref, v_ref, xr_ref, gate_ref)
    for n, o_ref in enumerate(outs):
        y = jnp.dot(xb, w_ref[:, n * PROJ_CHUNK:(n + 1) * PROJ_CHUNK], preferred_element_type=F32)
        if n < 2:
            scale = HEAD_DIM ** -0.5 if n == 0 else 1.0
            for h in range(N_HEADS):
                sl = slice(h * HEAD_DIM, (h + 1) * HEAD_DIM)
                o_ref[:, sl] = rope(y[:, sl]) * scale
        else:
            o_ref[...] = y


def _inproj(x, pos, invf, g, w):
    S = x.shape[0]
    tm = IN_TM
    row = lambda n: pl.BlockSpec((tm, n), lambda i: (i, 0))
    out = jax.ShapeDtypeStruct((S, PROJ_CHUNK), F32)
    return pl.pallas_call(
        _inproj_body,
        grid=(S // tm,),
        in_specs=[row(D_MODEL), row(1), _resident((1, HEAD_DIM)), _resident((1, D_MODEL)),
                  _resident(w.shape)],
        out_specs=[row(PROJ_CHUNK)] * 5,
        out_shape=[out] * 5,
        compiler_params=_params(("parallel",)),
    )(x, pos, invf, g, w)


def _attn_body(q_ref, k_ref, v_ref, o_ref, t4, q4, q16, k1, k4, k16, v1, v4, v16, op, ls):
    c = pl.program_id(1)
    kv = {1: (k1, v1), 4: (k4, v4), 16: (k16, v16)}

    @pl.when(c == 0)
    def _():
        for d in DILATIONS:
            k_s, v_s = kv[d]
            k_s[:, 0:ATTN_BLOCK, :] = jnp.zeros((d, ATTN_BLOCK, HEAD_DIM), BF16)
            v_s[:, 0:ATTN_BLOCK, 0:HEAD_DIM] = jnp.zeros((d, ATTN_BLOCK, HEAD_DIM), BF16)
            v_s[:, :, HEAD_DIM:] = jnp.ones((d, ATTN_BLOCK + SPAN // d, HEAD_DIM), BF16)

    @pl.when(c > 0)
    def _():
        for d in DILATIONS:
            rows = SPAN // d
            for ref in kv[d]:
                ref[:, 0:ATTN_BLOCK, :] = ref[:, rows:rows + ATTN_BLOCK, :]

    def deinterleave(x_ref, x1, x4, x16, off):
        lanes = slice(0, HEAD_DIM)
        if x1 is not None:
            x1[0, off:off + SPAN, lanes] = x_ref[...].astype(BF16)
        for r4 in range(4):
            t = x_ref[pl.ds(r4, SPAN // 4, stride=4), :]
            t4[r4] = t
            x4[r4, off:off + SPAN // 4, lanes] = t.astype(BF16)
        for r4 in range(4):
            for r2 in range(4):
                t = t4[r4, pl.ds(r2, SPAN // 16, stride=4), :]
                x16[4 * r2 + r4, off:off + SPAN // 16, lanes] = t.astype(BF16)

    deinterleave(q_ref, None, q4, q16, 0)
    deinterleave(k_ref, k1, k4, k16, ATTN_BLOCK)
    deinterleave(v_ref, v1, v4, v16, ATTN_BLOCK)

    qi = lax.broadcasted_iota(jnp.int32, (ATTN_BLOCK, 2 * ATTN_BLOCK), 0)
    kj = lax.broadcasted_iota(jnp.int32, (ATTN_BLOCK, 2 * ATTN_BLOCK), 1)
    dist = ATTN_BLOCK + qi - kj
    band = (dist >= 0) & (dist <= ATTN_BLOCK)
    bias_band = jnp.where(band, 0.0, -jnp.inf)
    first_key = jnp.where(c > 0, 0, ATTN_BLOCK)
    bias_first = jnp.where(band & (kj >= first_key), 0.0, -jnp.inf)

    def tile(qt, kk, vv, first_block):
        s = lax.dot_general(qt, kk, (((1,), (1,)), ((), ())), preferred_element_type=F32)
        s = s + (bias_first if first_block else bias_band)
        m = jnp.max(s, axis=-1, keepdims=True)
        e = jnp.exp(s - m).astype(BF16)
        ol = jnp.dot(e, vv, preferred_element_type=F32)
        l = ol[:, HEAD_DIM:]
        return ol[:, 0:HEAD_DIM] / l, m + jnp.log(l)

    def run_pattern(p_idx, d, q_tile):
        k_s, v_s = kv[d]
        nblk = SPAN // d // ATTN_BLOCK
        for r in range(d):
            for n in range(nblk):
                row0 = n * ATTN_BLOCK
                kk = k_s[r, row0:row0 + 2 * ATTN_BLOCK, :]
                vv = v_s[r, row0:row0 + 2 * ATTN_BLOCK, :]
                o, lse = tile(q_tile(r, row0), kk, vv, n == 0)
                if d == 1:
                    dst = pl.ds(row0, ATTN_BLOCK)
                elif d == 4:
                    dst = pl.ds(r * quarter + row0, ATTN_BLOCK)
                else:
                    dst = pl.ds((r % 4) * quarter + r // 4, ATTN_BLOCK, stride=4)
                op[p_idx, dst, :] = o
                ls[p_idx, dst, :] = lse

    quarter = SPAN // 4
    run_pattern(0, 1, lambda r, row0: q_ref[row0:row0 + ATTN_BLOCK, :].astype(BF16))
    run_pattern(1, 4, lambda r, row0: q4[r, row0:row0 + ATTN_BLOCK, :])
    run_pattern(2, 16, lambda r, row0: q16[r, row0:row0 + ATTN_BLOCK, :])

    for r4 in range(4):
        nat = pl.ds(r4, quarter, stride=4)
        blk = pl.ds(r4 * quarter, quarter)
        l0, l1, l2 = ls[0, nat, :], ls[1, blk, :], ls[2, blk, :]
        top = jnp.maximum(jnp.maximum(l0, l1), l2)
        w0 = jnp.exp(l0 - top)
        w1 = jnp.exp(l1 - top)
        w2 = jnp.exp(l2 - top)
        o_ref[nat, :] = (w0 * op[0, nat, :] + w1 * op[1, blk, :] + w2 * op[2, blk, :]) / (w0 + w1 + w2)


def _attention(q, k, v):
    S = q.shape[0]
    spec = pl.BlockSpec((SPAN, HEAD_DIM), lambda h, c: (c, h))
    slab = lambda d, off, dt, w=HEAD_DIM: pltpu.VMEM((d, off + SPAN // d, w), dt)
    vslab = lambda d: slab(d, ATTN_BLOCK, BF16, 2 * HEAD_DIM)
    scratch = [
        slab(4, 0, F32), slab(4, 0, BF16), slab(16, 0, BF16),
        slab(1, ATTN_BLOCK, BF16), slab(4, ATTN_BLOCK, BF16), slab(16, ATTN_BLOCK, BF16),
        vslab(1), vslab(4), vslab(16),
        pltpu.VMEM((3, SPAN, HEAD_DIM), F32), pltpu.VMEM((3, SPAN, HEAD_DIM), F32),
    ]
    return pl.pallas_call(
        _attn_body,
        grid=(N_HEADS, S // SPAN),
        in_specs=[spec] * 3,
        out_specs=spec,
        out_shape=jax.ShapeDtypeStruct((S, ATTN_WIDTH), F32),
        scratch_shapes=scratch,
        compiler_params=_params(("arbitrary", "arbitrary")),
    )(q, k, v)


def _rnn_body(xr_ref, gate_ref, cw_ref, cb_ref, wa_ref, ba_ref, wx_ref, bx_ref, lam_ref, g_ref,
              o_ref, xext, hcar, a_s, u_s):
    tm = xr_ref.shape[0]
    i = pl.program_id(0)

    T = SUBLANES

    @pl.when(i == 0)
    def _():
        xext[0:T, :] = jnp.zeros((T, RNN_WIDTH), F32)
        hcar[...] = jnp.zeros((T, RNN_WIDTH), F32)

    x = xr_ref[...]
    xext[T:T + tm, :] = x
    xc = cb_ref[...]
    for tap in range(CONV_WIDTH):
        sh = CONV_WIDTH - 1 - tap
        xc = xc + xext[T - sh:T - sh + tm, :] * cw_ref[tap:tap + 1, :]
    xext[0:T, :] = x[tm - T:tm, :]

    xcb = xc.astype(BF16)
    for j in range(RNN_WIDTH // LANES):
        sl = slice(j * LANES, (j + 1) * LANES)
        blk = xcb[:, sl]
        r = jax.nn.sigmoid(jnp.dot(blk, wa_ref[j], preferred_element_type=F32) + ba_ref[:, sl])
        ig = jax.nn.sigmoid(jnp.dot(blk, wx_ref[j], preferred_element_type=F32) + bx_ref[:, sl])
        log_a = -RGLRU_C * r * jax.nn.softplus(-lam_ref[:, sl])
        a = jnp.exp(log_a)
        a_s[:, sl] = a
        u_s[:, sl] = jnp.sqrt(-jnp.tanh(log_a) * (a * a + 1.0)) * ig * xc[:, sl]

    row = lax.broadcasted_iota(jnp.int32, (T, RNN_WIDTH), 0)

    def scan(t, h):
        base = pl.multiple_of(t * T, T)
        A = a_s[pl.ds(base, T), :]
        B = u_s[pl.ds(base, T), :]
        sft = 1
        while sft < T:
            valid = row >= sft
            A_sh = pltpu.roll(A, sft, 0)
            B_sh = pltpu.roll(B, sft, 0)
            B = jnp.where(valid, A * B_sh + B, B)
            A = jnp.where(valid, A * A_sh, A)
            sft *= 2
        hs = A * h + B
        u_s[pl.ds(base, T), :] = hs
        return jnp.broadcast_to(hs[T - 1:T, :], (T, RNN_WIDTH))

    hcar[...] = lax.fori_loop(0, tm // T, scan, hcar[...])

    rnn = u_s[...] * jax.nn.gelu(gate_ref[...])
    o_ref[...] = _rms(rnn, g_ref[...]).astype(BF16)


def _rnn(xr, gate, cw, cb, wa, ba, wx, bx, lam, g):
    S = xr.shape[0]
    tm = RNN_TM
    row = pl.BlockSpec((tm, RNN_WIDTH), lambda i: (i, 0))
    vec = _resident((1, RNN_WIDTH))
    return pl.pallas_call(
        _rnn_body,
        grid=(S // tm,),
        in_specs=[row, row, _resident(cw.shape), vec, _resident(wa.shape), vec,
                  _resident(wx.shape), vec, vec, vec],
        out_specs=row,
        out_shape=jax.ShapeDtypeStruct((S, RNN_WIDTH), BF16),
        scratch_shapes=[pltpu.VMEM((tm + SUBLANES, RNN_WIDTH), F32), pltpu.VMEM((SUBLANES, RNN_WIDTH), F32),
                        pltpu.VMEM((tm, RNN_WIDTH), F32), pltpu.VMEM((tm, RNN_WIDTH), F32)],
        compiler_params=_params(("arbitrary",)),
    )(xr, gate, cw, cb, wa, ba, wx, bx, lam, g)


def _outproj_body(attn_ref, rnn_ref, x_ref, ga_ref, wo_ref, gf_ref, wq_ref, sk_ref,
                  h1_ref, hnt_ref, st_ref):
    an = _rms(attn_ref[...], ga_ref[...]).astype(BF16)
    mix = jnp.dot(an, wo_ref[0:ATTN_WIDTH, :], preferred_element_type=F32)
    mix = mix + jnp.dot(rnn_ref[...], wo_ref[ATTN_WIDTH:, :], preferred_element_type=F32)
    h1 = x_ref[...] + mix
    h1_ref[...] = h1
    hn32 = _rms(h1, gf_ref[...])
    hn = hn32.astype(BF16)
    hnt_ref[...] = hn32.T.astype(BF16)
    qp = jnp.dot(hn, wq_ref[...], preferred_element_type=F32).astype(BF16)
    for hc in range(2 * PEER_HEADS):
        st_ref[hc] = lax.dot_general(sk_ref[hc], qp[:, hc * PEER_KEY_DIM:(hc + 1) * PEER_KEY_DIM],
                                     (((1,), (1,)), ((), ())), preferred_element_type=F32)


def _outproj(attn, rnn, x, ga, wo, gf, wq, sk):
    S = x.shape[0]
    tm = OUT_TM
    row = lambda n: pl.BlockSpec((tm, n), lambda i: (i, 0))
    return pl.pallas_call(
        _outproj_body,
        grid=(S // tm,),
        in_specs=[row(ATTN_WIDTH), row(RNN_WIDTH), row(D_MODEL), _resident((1, ATTN_WIDTH)),
                  _resident(wo.shape), _resident((1, D_MODEL)), _resident(wq.shape),
                  _resident(sk.shape)],
        out_specs=[row(D_MODEL), pl.BlockSpec((D_MODEL, tm), lambda i: (0, i)),
                   pl.BlockSpec((2 * PEER_HEADS, PEER_NKEYS, tm), lambda i: (0, 0, i))],
        out_shape=[jax.ShapeDtypeStruct((S, D_MODEL), F32), jax.ShapeDtypeStruct((D_MODEL, S), BF16),
                   jax.ShapeDtypeStruct((2 * PEER_HEADS, PEER_NKEYS, S), F32)],
        compiler_params=_params(("parallel",)),
    )(attn, rnn, x, ga, wo, gf, wq, sk)


def _sort16_network():
    pairs = []

    def merge(lo, n, r):
        step = 2 * r
        if step < n:
            merge(lo, n, step)
            merge(lo + r, n, step)
            pairs.extend((i, i + r) for i in range(lo + r, lo + n - r, step))
        else:
            pairs.append((lo, lo + r))

    def sort(lo, n):
        if n > 1:
            sort(lo, n // 2)
            sort(lo + n // 2, n // 2)
            merge(lo, n, 1)

    sort(0, 16)
    return tuple(pairs)


_SORT16 = _sort16_network()
_SORT10 = tuple(p for p in _SORT16 if p[1] < 10)


def _select_body(st_ref, r2_ref, e2_ref, jb_ref, e1n_ref, tops, tau_s):
    tb = st_ref.shape[-1]
    K = PEER_TOPK
    NA = 2 * PEER_HEADS
    ninf = -jnp.inf
    G = SUBLANES
    row8 = lax.broadcasted_iota(jnp.int32, (G, tb), 0)

    def pop_max(v, depth):
        mx = jnp.max(v[0], axis=0, keepdims=True)
        where_hit = jnp.where(v[0] == mx, row8, G)
        hit = where_hit == jnp.min(where_hit, axis=0, keepdims=True)
        for g in range(min(len(v), depth)):
            v[g] = jnp.where(hit, v[g + 1] if g + 1 < len(v) else ninf, v[g])
        return mx

    assert PEER_NKEYS // G == 16 and PEER_TOPK == 16

    def extract(a, carry):
        v = [st_ref[a, G * g:G * (g + 1), :] for g in range(PEER_NKEYS // G)]
        for i, j in _SORT16:
            v[i], v[j] = jnp.maximum(v[i], v[j]), jnp.minimum(v[i], v[j])
        for k in range(K):
            tops[a, k:k + 1, :] = pop_max(v, K - 1 - k)
        return carry

    lax.fori_loop(0, NA, extract, 0, unroll=8)

    def candidates(h):
        A = tops[2 * h]
        B = tops[2 * h + 1]
        B8 = B[0:8, :]
        limits = (16, 8, 5, 4, 3, 2, 2, 2)
        cands = [A[0:1, :] + B, A[1:2, :] + B8]
        for i in range(2, 8):
            cands.append(jnp.where(row8 < limits[i], A[i:i + 1, :] + B8, ninf))
        cands.append(A[8:16, :] + B[0:1, :])
        return A, B, cands

    def kth(h, carry):
        cands = candidates(h)[2]
        v = [cands[0][0:8, :], cands[0][8:16, :]] + cands[1:]
        for i, j in _SORT10:
            v[i], v[j] = jnp.maximum(v[i], v[j]), jnp.minimum(v[i], v[j])
        for k in range(K):
            tau = pop_max(v, K - 1 - k)
        tau_s[pl.ds(h, 1), :] = tau
        return carry

    lax.fori_loop(0, PEER_HEADS, kth, 0, unroll=8)

    for h in range(PEER_HEADS):
        A, B, cands = candidates(h)
        tau = tau_s[h:h + 1, :]
        cmax = A[0:1, :] + B[0:1, :]
        z = None
        for c in cands:
            zc = jnp.sum(jnp.where(c >= tau, jnp.exp(c - cmax), 0.0), axis=0, keepdims=True)
            z = zc if z is None else z + zc

        j_rows = [jnp.sum(jnp.where(cands[i] >= tau, 1.0, 0.0), axis=0, keepdims=True)
                  for i in range(8)]
        tail = jnp.where(cands[8] >= tau, 1.0, 0.0)
        for i in range(8):
            j_rows.append(tail[i:i + 1, :])

        s1 = st_ref[2 * h]
        s2 = st_ref[2 * h + 1]
        jb = jnp.zeros((PEER_NKEYS, tb), F32)
        r2 = jnp.full((PEER_NKEYS, tb), float(K), F32)
        for i in reversed(range(K)):
            jb = jnp.where(s1 == A[i:i + 1, :], j_rows[i], jb)
            r2 = jnp.where(s2 == B[i:i + 1, :], float(i), r2)
        jb_ref[h] = jb
        r2_ref[h] = r2.astype(BF16)
        e1n_ref[h] = jnp.exp(s1 - A[0:1, :]) / z
        e2_ref[h] = jnp.exp(s2 - B[0:1, :]).astype(BF16)


def _select(st):
    S = st.shape[-1]
    tb = SEL_TB
    out = lambda dt: jax.ShapeDtypeStruct((PEER_HEADS, PEER_NKEYS, S), dt)
    ospec = pl.BlockSpec((PEER_HEADS, PEER_NKEYS, tb), lambda i: (0, 0, i))
    return pl.pallas_call(
        _select_body,
        grid=(S // tb,),
        in_specs=[pl.BlockSpec((2 * PEER_HEADS, PEER_NKEYS, tb), lambda i: (0, 0, i))],
        out_specs=[ospec] * 4,
        out_shape=[out(BF16), out(BF16), out(F32), out(F32)],
        scratch_shapes=[pltpu.VMEM((2 * PEER_HEADS, PEER_TOPK, tb), F32),
                        pltpu.VMEM((PEER_HEADS, tb), F32)],
        compiler_params=_params(("parallel",)),
    )(st)


def _peer_body(hnt_ref, u_ref, vt_ref, r2_ref, e2_ref, jb_ref, e1n_ref, o_ref, w_even, w_odd):
    j = pl.program_id(1)
    ec = PEER_EC
    n_i1 = ec // PEER_NKEYS

    zero = jnp.zeros((), BF16)

    def build_gates(step, w_s, c, s):
        half = hnt_ref.shape[1] // PEER_GATE_STRIPS
        ssl = slice(s * half, (s + 1) * half)
        i1_0 = jnp.minimum((step * PEER_CHUNKS + c) * n_i1, PEER_NKEYS - n_i1)
        acc = [None] * n_i1
        for h in range(PEER_HEADS):
            r2t = r2_ref[h, :, ssl]
            e2t = e2_ref[h, :, ssl]
            for kk in range(n_i1):
                jb_b = jb_ref[h, pl.ds(i1_0 + kk, 1), :][:, ssl].astype(BF16)
                e1_b = e1n_ref[h, pl.ds(i1_0 + kk, 1), :][:, ssl].astype(BF16)
                t = jnp.where(r2t < jb_b, e2t, zero) * e1_b
                acc[kk] = t if acc[kk] is None else acc[kk] + t
        for kk in range(n_i1):
            w_s[c, kk * PEER_NKEYS:(kk + 1) * PEER_NKEYS, ssl] = acc[kk]

    @pl.when(j == 0)
    def _():
        o_ref[...] = jnp.zeros(o_ref.shape, F32)
        for c in range(PEER_CHUNKS):
            for s in range(PEER_GATE_STRIPS):
                build_gates(0, w_even, c, s)

    def step(w_cur, w_next):
        pieces = [(c, s) for c in range(PEER_CHUNKS) for s in range(PEER_GATE_STRIPS)]
        n_pieces, n_subdots = len(pieces), PEER_CHUNKS * PEER_ROW_BLOCKS

        acts = [jnp.dot(u_ref[c * ec:(c + 1) * ec, :], hnt_ref[...], preferred_element_type=F32)
                for c in range(PEER_CHUNKS)]
        ps = [w_cur[c] * jax.nn.gelu(acts[c]).astype(BF16) for c in range(PEER_CHUNKS)]
        rows = D_MODEL // PEER_ROW_BLOCKS
        for rb in range(PEER_ROW_BLOCKS):
            rsl = slice(rb * rows, (rb + 1) * rows)
            tot = None
            for c in range(PEER_CHUNKS):
                part = jnp.dot(vt_ref[c, rsl, :], ps[c], preferred_element_type=F32)
                tot = part if tot is None else tot + part
                done = rb * PEER_CHUNKS + c + 1
                while n_pieces - len(pieces) < done * n_pieces // n_subdots:
                    build_gates(j + 1, w_next, *pieces.pop(0))
            o_ref[rsl, :] += tot

    @pl.when(j % 2 == 0)
    def _():
        step(w_even, w_odd)

    @pl.when(j % 2 == 1)
    def _():
        step(w_odd, w_even)


def _peer(hnt, u, vt, r2, e2, jb, e1n):
    S = hnt.shape[1]
    tm, ec, nc = PEER_TM, PEER_EC, PEER_CHUNKS
    tokt = pl.BlockSpec((D_MODEL, tm), lambda i, j: (0, i))
    sel = pl.BlockSpec((PEER_HEADS, PEER_NKEYS, tm), lambda i, j: (0, 0, i))
    return pl.pallas_call(
        _peer_body,
        grid=(S // tm, PEER_EXPERTS // (ec * nc)),
        in_specs=[tokt,
                  pl.BlockSpec((ec * nc, D_MODEL), lambda i, j: (j, 0)),
                  pl.BlockSpec((nc, D_MODEL, ec), lambda i, j: (j, 0, 0)),
                  sel, sel, sel, sel],
        out_specs=tokt,
        out_shape=jax.ShapeDtypeStruct((D_MODEL, S), F32),
        scratch_shapes=[pltpu.VMEM((nc, ec, tm), BF16), pltpu.VMEM((nc, ec, tm), BF16)],
        compiler_params=_params(("parallel", "arbitrary")),
    )(hnt, u, vt, r2, e2, jb, e1n)


def _ple_body(h_ref, ft_ref, p_ref, gp_ref, wg_ref, bg_ref, wp_ref, gf_ref, o_ref):
    half = h_ref.shape[0] // 2
    for r in range(2):
        rsl = slice(r * half, (r + 1) * half)
        h = h_ref[rsl, :] + ft_ref[:, rsl].T
        hn = _rms(h, gp_ref[...]).astype(BF16)
        g = jax.nn.sigmoid(jnp.dot(hn, wg_ref[...], preferred_element_type=F32) + bg_ref[...])
        pp = jnp.dot(p_ref[rsl, :].astype(BF16), wp_ref[...], preferred_element_type=F32)
        o_ref[rsl, :] = _rms(h + g * pp, gf_ref[...])


def _ple(h, ft, p, gp, wg, bg, wp, gf):
    S = h.shape[0]
    tm = PLE_TM
    row = lambda n: pl.BlockSpec((tm, n), lambda i: (i, 0))
    vec = _resident((1, D_MODEL))
    return pl.pallas_call(
        _ple_body,
        grid=(S // tm,),
        in_specs=[row(D_MODEL), pl.BlockSpec((D_MODEL, tm), lambda i: (0, i)), row(PLE_DIM), vec,
                  _resident(wg.shape), vec, _resident(wp.shape), vec],
        out_specs=row(D_MODEL),
        out_shape=jax.ShapeDtypeStruct((S, D_MODEL), F32),
        compiler_params=_params(("parallel",)),
    )(h, ft, p, gp, wg, bg, wp, gf)


def _pair_blocks(w):
    w = w.reshape(8, 2, 64, 64)
    z = jnp.zeros((8, 64, 64), w.dtype)
    top = jnp.concatenate([w[:, 0], z], axis=-1)
    bot = jnp.concatenate([z, w[:, 1]], axis=-1)
    return jnp.concatenate([top, bot], axis=-2).astype(BF16)


def kernel(x, p, positions, mix_norm, w_in, conv_w, conv_b, rg_wa, rg_ba, rg_wx, rg_bx, rg_lambda,
           attn_out_norm, rnn_out_norm, w_out, ffn_norm, peer_wq, peer_subkeys, peer_u, peer_v,
           ple_norm, ple_w_gate, ple_b_gate, ple_proj, final_norm):
    B, S, D = x.shape
    assert (B, S, D) == (1, SEQ, D_MODEL) and S % SPAN == 0 and w_in.shape[0] == 1
    vec = lambda a: a.reshape(1, -1).astype(F32)

    half = ROPE_DIM // 2
    inv_freq = ROPE_THETA ** (-jnp.arange(half, dtype=F32) * 2.0 / ROPE_DIM)
    invf = jnp.zeros((1, HEAD_DIM), F32).at[0, :ROPE_DIM].set(jnp.tile(inv_freq, 2))

    q, k, v, xr, gate = _inproj(x[0], positions.reshape(S, 1), invf, vec(mix_norm[0]),
                                w_in[0].astype(BF16))
    attn = _attention(q, k, v)
    rnn = _rnn(xr, gate, conv_w[0], vec(conv_b[0]), _pair_blocks(rg_wa[0]), vec(rg_ba[0]),
               _pair_blocks(rg_wx[0]), vec(rg_bx[0]), vec(rg_lambda[0]), vec(rnn_out_norm[0]))
    sk = peer_subkeys[0].reshape(2 * PEER_HEADS, PEER_NKEYS, -1).astype(BF16)
    h1, hnt, st = _outproj(attn, rnn, x[0], vec(attn_out_norm[0]), w_out[0].astype(BF16),
                           vec(ffn_norm[0]), peer_wq[0].astype(BF16), sk)
    r2, e2, jb, e1n = _select(st)
    vt = peer_v[0].astype(BF16).reshape(PEER_EXPERTS // PEER_EC, PEER_EC, D).transpose(0, 2, 1)
    ft = _peer(hnt, peer_u[0].astype(BF16), vt, r2, e2, jb, e1n)
    out = _ple(h1, ft, p[0, 0], vec(ple_norm[0]), ple_w_gate[0].astype(BF16), vec(ple_b_gate[0]),
               ple_proj[0].astype(BF16), vec(final_norm))
    return out.reshape(B, S, D)
```
